```python
import math
import jax
import jax.numpy as jnp
from jax import lax
import numpy as np

D_MODEL = 1024
BATCH = 8
SEQ = 2048
DEPTH = 4
DEC_BATCH = 128
DEC_SEQ = 4
PAST_LEN = 16384
PAGE_SIZE = 128

MIX_W = D_MODEL
HEAD_DIM = 64
RWKV_W = MIX_W // 4
RWKV_HEADS = RWKV_W // HEAD_DIM
RWKV_DECAY_RANK = 64
RWKV_ICLR_RANK = 64
RWKV_GATE_RANK = 128
RWKV_LN_EPS = 1e-5 * HEAD_DIM
MLSTM_W = MIX_W // 4
MLSTM_HEADS = MLSTM_W // HEAD_DIM
SSD_W = MIX_W // 2
SSD_HEADS = SSD_W // HEAD_DIM
SSD_STATE = 128
SSD_GROUPS = 2
SSD_HEADS_PER_GROUP = SSD_HEADS // SSD_GROUPS
CONV_WIDTH = 4
CHUNK = 64
MLSTM_CONV_W = 2 * MLSTM_W
SSD_CONV_W = SSD_W + 2 * SSD_GROUPS * SSD_STATE
RWKV_PROJ = 3 * RWKV_W + RWKV_DECAY_RANK + RWKV_ICLR_RANK + RWKV_GATE_RANK
MLSTM_PROJ = MLSTM_CONV_W + 2 * MLSTM_W + 2 * MLSTM_HEADS
SSD_PROJ = SSD_W + SSD_CONV_W + SSD_HEADS
PROJ_W = RWKV_PROJ + MLSTM_PROJ + SSD_PROJ
PEER_KEYS = 128
PEER_EXPERTS = PEER_KEYS * PEER_KEYS
PEER_HEADS = 8
PEER_TOPK = 16
PEER_QDIM = 256
PEER_HALF = PEER_QDIM // 2
PEER_BLOCK = 256
NORM_EPS = 1e-6

kernel_name = 'hybrid_rwkv7_mlstm_ssd_peer_step'


def _split(a, sizes):
    return jnp.split(a, np.cumsum(sizes)[:-1].tolist(), axis=-1)


def _chunk_len(T):
    return CHUNK if T % CHUNK == 0 else T


def rms_norm(x, g):
    xf = x.astype(jnp.float32)
    y = xf * lax.rsqrt(jnp.mean(xf * xf, axis=-1, keepdims=True) + NORM_EPS)
    return (y * g.astype(jnp.float32)).astype(x.dtype)


def head_layer_norm(y, eps):
    yc = y - jnp.mean(y, axis=-1, keepdims=True)
    yn = yc * lax.rsqrt(jnp.mean(yc * yc, axis=-1, keepdims=True) + eps)
    return yn.reshape(*y.shape[:-2], y.shape[-2] * y.shape[-1])


def causal_conv(u, buf, w, b):
    T = u.shape[1]
    full = jnp.concatenate([buf.astype(u.dtype), u], axis=1)
    out = b + sum(full[:, j:j + T] * w[j] for j in range(CONV_WIDTH))
    return out, full[:, T:]


def rwkv7_mix(P, shift, S0, p):
    Bsz, T, _ = P.shape
    prev = jnp.concatenate([shift[:, None].astype(P.dtype), P[:, :-1]], axis=1)
    X = P + (prev - P) * p['rwkv_mu']
    r, k, v, xw, xa, xg = _split(X, [RWKV_W, RWKV_W, RWKV_W, RWKV_DECAY_RANK, RWKV_ICLR_RANK, RWKV_GATE_RANK])
    w_log = -jax.nn.softplus(-(p['rwkv_w0'] + jnp.tanh(xw) @ p['rwkv_w_up'])) - 0.5
    decay = jnp.exp(-jnp.exp(w_log))
    a = jax.nn.sigmoid(p['rwkv_a0'] + xa @ p['rwkv_a_up'])
    g = jax.nn.sigmoid(xg) @ p['rwkv_g_up']

    def hs(t):
        return t.reshape(Bsz, T, RWKV_HEADS, HEAD_DIM)

    kk = hs(k * p['rwkv_k_k'])
    kk = kk / jnp.maximum(jnp.sqrt(jnp.sum(kk * kk, axis=-1, keepdims=True)), 1e-12)
    k = hs(k * (1 + (a - 1) * p['rwkv_k_a']))
    r, v, decay, a = hs(r), hs(v), hs(decay), hs(a)

    def step(S, inp):
        r_t, w_t, k_t, v_t, kk_t, a_t = inp
        sa = jnp.einsum('bhvk,bhk->bhv', S, -kk_t)
        S = (S * w_t[:, :, None, :] + sa[..., None] * (kk_t * a_t)[:, :, None, :]
             + v_t[..., None] * k_t[:, :, None, :])
        return S, jnp.einsum('bhvk,bhk->bhv', S, r_t)

    def tm(t):
        return jnp.swapaxes(t, 0, 1)

    S, y = lax.scan(step, S0.astype(jnp.float32), (tm(r), tm(decay), tm(k), tm(v), tm(kk), tm(a)))
    y = head_layer_norm(tm(y), RWKV_LN_EPS) * p['rwkv_ln_g'] + p['rwkv_ln_b']
    bonus = jnp.sum(r * k * p['rwkv_r_k'], axis=-1, keepdims=True) * v
    out = (y + bonus.reshape(Bsz, T, RWKV_W)) * g
    return out, P[:, -1], S


def mlstm_mix(P, conv_buf, C0, n0, m0, p):
    Bsz, T, _ = P.shape
    qk, v, o, i_pre, f_pre = _split(P, [MLSTM_CONV_W, MLSTM_W, MLSTM_W, MLSTM_HEADS, MLSTM_HEADS])
    qk, new_buf = causal_conv(qk, conv_buf, p['mlstm_conv_w'], p['mlstm_conv_b'])
    q, k = jnp.split(jax.nn.silu(qk), 2, axis=-1)
    k = k * HEAD_DIM ** -0.5
    i_pre = i_pre + p['mlstm_i_b']
    logf = jax.nn.log_sigmoid(f_pre + p['mlstm_f_b'])
    L = _chunk_len(T)
    nc = T // L

    def ch(t):
        return t.reshape(Bsz, nc, L, MLSTM_HEADS, HEAD_DIM).transpose(1, 0, 3, 2, 4)

    def cg(t):
        return t.reshape(Bsz, nc, L, MLSTM_HEADS).transpose(1, 0, 3, 2)

    tril = jnp.tril(jnp.ones((L, L), dtype=bool))

    def step(carry, inp):
        C, n, m = carry
        q_, k_, v_, i_, f_ = inp
        b = jnp.cumsum(f_, axis=-1)
        D = jnp.where(tril, b[..., :, None] - b[..., None, :] + i_[..., None, :], -jnp.inf)
        inter = b + m[..., None]
        m_t = jnp.maximum(inter, jnp.max(D, axis=-1))
        s = jnp.einsum('bhtd,bhsd->bhts', q_, k_) * jnp.exp(D - m_t[..., None])
        a_in = jnp.exp(inter - m_t)
        num = (jnp.einsum('bhts,bhsd->bhtd', s, v_)
               + a_in[..., None] * jnp.einsum('bhvk,bhtk->bhtv', C, q_))
        den = jnp.sum(s, axis=-1) + a_in * jnp.einsum('bhk,bhtk->bht', n, q_)
        h = num / jnp.maximum(jnp.abs(den), jnp.exp(-m_t))[..., None]
        b_end = b[..., -1]
        w_s = b_end[..., None] - b + i_
        m_new = jnp.maximum(b_end + m, jnp.max(w_s, axis=-1))
        w_s = jnp.exp(w_s - m_new[..., None])
        a_end = jnp.exp(b_end + m - m_new)
        C = a_end[..., None, None] * C + jnp.einsum('bhs,bhsv,bhsk->bhvk', w_s, v_, k_)
        n = a_end[..., None] * n + jnp.einsum('bhs,bhsk->bhk', w_s, k_)
        return (C, n, m_new), h

    init = (C0.astype(jnp.float32), n0.astype(jnp.float32), m0.astype(jnp.float32))
    (C, n, m), h = lax.scan(step, init, (ch(q), ch(k), ch(v), cg(i_pre), cg(logf)))
    h = h.transpose(1, 0, 3, 2, 4).reshape(Bsz, T, MLSTM_HEADS, HEAD_DIM)
    out = head_layer_norm(h, NORM_EPS) * p['mlstm_norm_g'] * jax.nn.sigmoid(o)
    return out, new_buf, C, n, m


def ssd_mix(P, conv_buf, h0, p):
    Bsz, T, _ = P.shape
    G, J = SSD_GROUPS, SSD_HEADS_PER_GROUP
    z, xbc, dt_raw = _split(P, [SSD_W, SSD_CONV_W, SSD_HEADS])
    xbc, new_buf = causal_conv(xbc, conv_buf, p['ssd_conv_w'], p['ssd_conv_b'])
    xs, Bm, Cm = _split(jax.nn.silu(xbc), [SSD_W, G * SSD_STATE, G * SSD_STATE])
    dt = jax.nn.softplus(dt_raw + p['ssd_dt_bias'])
    A = -jnp.exp(p['ssd_A_log'])
    L = _chunk_len(T)
    nc = T // L
    xc = xs.reshape(Bsz, nc, L, G, J, HEAD_DIM).transpose(1, 0, 3, 4, 2, 5)
    dtc = dt.reshape(Bsz, nc, L, G, J).transpose(1, 0, 3, 4, 2)
    lac = (dt * A).reshape(Bsz, nc, L, G, J).transpose(1, 0, 3, 4, 2)
    Bc = Bm.reshape(Bsz, nc, L, G, SSD_STATE).transpose(1, 0, 3, 2, 4)
    Cc = Cm.reshape(Bsz, nc, L, G, SSD_STATE).transpose(1, 0, 3, 2, 4)
    tril = jnp.tril(jnp.ones((L, L), dtype=bool))

    def step(h, inp):
        x_, dt_, la_, B_, C_ = inp
        acum = jnp.cumsum(la_, axis=-1)
        seg = jnp.exp(jnp.where(tril, acum[..., :, None] - acum[..., None, :], -jnp.inf))
        W = seg * jnp.einsum('bgtn,bgsn->bgts', C_, B_)[:, :, None] * dt_[..., None, :]
        y = (jnp.einsum('bgjts,bgjsp->bgjtp', W, x_)
             + jnp.exp(acum)[..., None] * jnp.einsum('bgtn,bgjpn->bgjtp', C_, h))
        w_end = jnp.exp(acum[..., -1:] - acum) * dt_
        h = (jnp.exp(acum[..., -1])[..., None, None] * h
             + jnp.einsum('bgjs,bgjsp,bgsn->bgjpn', w_end, x_, B_))
        return h, y

    h_init = h0.reshape(Bsz, G, J, HEAD_DIM, SSD_STATE).astype(jnp.float32)
    hG, y = lax.scan(step, h_init, (xc, dtc, lac, Bc, Cc))
    y = y.transpose(1, 0, 4, 2, 3, 5).reshape(Bsz, T, SSD_HEADS, HEAD_DIM)
    y = y + p['ssd_D'][:, None] * xs.reshape(Bsz, T, SSD_HEADS, HEAD_DIM)
    y = rms_norm(y.reshape(Bsz, T, SSD_W) * jax.nn.silu(z), p['ssd_norm_g'])
    return y, new_buf, hG.reshape(Bsz, SSD_HEADS, HEAD_DIM, SSD_STATE)


def peer_ffn(xn, wq, subkeys, U, V):
    Bsz, T, D = xn.shape
    n = Bsz * T
    pad = (-n) % PEER_BLOCK
    tok = jnp.pad(xn.reshape(n, D), ((0, pad), (0, 0))).reshape(-1, PEER_BLOCK, D)

    def block(xb):
        q = (xb @ wq).reshape(PEER_BLOCK, PEER_HEADS, 2, PEER_HALF)
        s = jnp.einsum('thcd,ckd->thck', q, subkeys).astype(jnp.float32)
        s1, i1 = lax.top_k(s[:, :, 0], PEER_TOPK)
        s2, i2 = lax.top_k(s[:, :, 1], PEER_TOPK)
        cand = (s1[..., :, None] + s2[..., None, :]).reshape(PEER_BLOCK, PEER_HEADS, -1)
        cidx = (i1[..., :, None] * PEER_KEYS + i2[..., None, :]).reshape(PEER_BLOCK, PEER_HEADS, -1)
        top, pos = lax.top_k(cand, PEER_TOPK)
        idx = jnp.take_along_axis(cidx, pos, axis=-1)
        gate = jax.nn.softmax(top, axis=-1)
        act = jax.nn.gelu(jnp.einsum('thkd,td->thk', U[idx], xb).astype(jnp.float32))
        return jnp.einsum('thk,thkd->td', (gate * act).astype(xb.dtype), V[idx])

    out = lax.map(block, tok).reshape(-1, D)[:n]
    return out.reshape(Bsz, T, D)


def zero_state(bsz, dtype):
    return (jnp.zeros((bsz, RWKV_PROJ), dtype),
            jnp.zeros((bsz, RWKV_HEADS, HEAD_DIM, HEAD_DIM), dtype),
            jnp.zeros((bsz, CONV_WIDTH - 1, MLSTM_CONV_W), dtype),
            jnp.zeros((bsz, MLSTM_HEADS, HEAD_DIM, HEAD_DIM), dtype),
            jnp.zeros((bsz, MLSTM_HEADS, HEAD_DIM), dtype),
            jnp.zeros((bsz, MLSTM_HEADS), dtype),
            jnp.zeros((bsz, CONV_WIDTH - 1, SSD_CONV_W), dtype),
            jnp.zeros((bsz, SSD_HEADS, HEAD_DIM, SSD_STATE), dtype))


def layer(x, c, state, p):
    shift, wkv, mconv, mC, mn, mm, sconv, sst = state
    mod = jax.nn.silu(c) @ p['ada_w'] + p['ada_b']
    sh1, sc1, g1, sh2, sc2, g2 = jnp.split(mod[:, None, :], 6, axis=-1)
    h = rms_norm(x, p['norm1_g']) * (1 + sc1) + sh1
    P = (h @ p['w_in']).astype(jnp.float32)
    P_r, P_m, P_s = _split(P, [RWKV_PROJ, MLSTM_PROJ, SSD_PROJ])
    o_r, shift, wkv = rwkv7_mix(P_r, shift, wkv, p)
    o_m, mconv, mC, mn, mm = mlstm_mix(P_m, mconv, mC, mn, mm, p)
    o_s, sconv, sst = ssd_mix(P_s, sconv, sst, p)
    mix = jnp.concatenate([o_r, o_m, o_s], axis=-1).astype(x.dtype)
    x = x + g1 * (mix @ p['w_out'])
    h2 = rms_norm(x, p['norm2_g']) * (1 + sc2) + sh2
    x = x + g2 * peer_ffn(h2, p['peer_wq'], p['peer_subkeys'], p['peer_u'], p['peer_v'])
    new = (shift, wkv, mconv, mC, mn, mm, sconv, sst)
    return x, tuple(s.astype(x.dtype) for s in new)


def setup_inputs(seed: int = 0) -> dict:
    key = jax.random.key(seed)
    keys = jax.random.split(key, 64)
    count = [0]

    def nk():
        count[0] += 1
        return keys[count[0] - 1]

    def nrm(shape, scale=1.0, mean=0.0):
        return mean + scale * jax.random.normal(nk(), shape, jnp.float32)

    def unif(shape, lo, hi):
        return jax.random.uniform(nk(), shape, jnp.float32, lo, hi)

    L, D = DEPTH, D_MODEL
    dt0 = jnp.exp(unif((L, SSD_HEADS), math.log(1e-3), math.log(1e-1)))
    return {
        'x_prompt': nrm((BATCH, SEQ, D)),
        'x_sample': nrm((DEC_BATCH, DEC_SEQ, D)),
        'state_rwkv_shift': nrm((L, DEC_BATCH, RWKV_PROJ)),
        'state_rwkv_wkv': nrm((L, DEC_BATCH, RWKV_HEADS, HEAD_DIM, HEAD_DIM), 0.3),
        'state_mlstm_conv': nrm((L, DEC_BATCH, CONV_WIDTH - 1, MLSTM_CONV_W)),
        'state_mlstm_C': nrm((L, DEC_BATCH, MLSTM_HEADS, HEAD_DIM, HEAD_DIM), 0.3),
        'state_mlstm_n': nrm((L, DEC_BATCH, MLSTM_HEADS, HEAD_DIM), 0.3),
        'state_mlstm_m': nrm((L, DEC_BATCH, MLSTM_HEADS)),
        'state_ssd_conv': nrm((L, DEC_BATCH, CONV_WIDTH - 1, SSD_CONV_W)),
        'state_ssd': nrm((L, DEC_BATCH, SSD_HEADS, HEAD_DIM, SSD_STATE), 0.3),
        'c_prompt': nrm((BATCH, D)),
        'c_sample': nrm((DEC_BATCH, D)),
        'ada_w': nrm((L, D, 6 * D), 0.5 * D ** -0.5),
        'ada_b': nrm((L, 6 * D), 0.02),
        'norm1_g': nrm((L, D), 0.02, 1.0),
        'norm2_g': nrm((L, D), 0.02, 1.0),
        'w_in': nrm((L, D, PROJ_W), D ** -0.5),
        'w_out': nrm((L, MIX_W, D), MIX_W ** -0.5),
        'rwkv_mu': unif((L, RWKV_PROJ), 0.0, 1.0),
        'rwkv_w0': nrm((L, RWKV_W), 0.5, -1.5),
        'rwkv_w_up': nrm((L, RWKV_DECAY_RANK, RWKV_W), 0.5 * RWKV_DECAY_RANK ** -0.5),
        'rwkv_a0': nrm((L, RWKV_W), 0.5),
        'rwkv_a_up': nrm((L, RWKV_ICLR_RANK, RWKV_W), RWKV_ICLR_RANK ** -0.5),
        'rwkv_g_up': nrm((L, RWKV_GATE_RANK, RWKV_W), RWKV_GATE_RANK ** -0.5),
        'rwkv_k_k': nrm((L, RWKV_W), 0.02, 0.85),
        'rwkv_k_a': nrm((L, RWKV_W), 0.02, 1.0),
        'rwkv_r_k': nrm((L, RWKV_HEADS, HEAD_DIM), 0.1),
        'rwkv_ln_g': nrm((L, RWKV_W), 0.02, 1.0),
        'rwkv_ln_b': nrm((L, RWKV_W), 0.02),
        'mlstm_conv_w': nrm((L, CONV_WIDTH, MLSTM_CONV_W), 0.5),
        'mlstm_conv_b': nrm((L, MLSTM_CONV_W), 0.02),
        'mlstm_i_b': nrm((L, MLSTM_HEADS), 0.1),
        'mlstm_f_b': unif((L, MLSTM_HEADS), 3.0, 6.0),
        'mlstm_norm_g': nrm((L, MLSTM_W), 0.02, 1.0),
        'ssd_conv_w': nrm((L, CONV_WIDTH, SSD_CONV_W), 0.5),
        'ssd_conv_b': nrm((L, SSD_CONV_W), 0.02),
        'ssd_dt_bias': dt0 + jnp.log(-jnp.expm1(-dt0)),
        'ssd_A_log': jnp.log(unif((L, SSD_HEADS), 1.0, 16.0)),
        'ssd_D': nrm((L, SSD_HEADS), 0.1, 1.0),
        'ssd_norm_g': nrm((L, SSD_W), 0.02, 1.0),
        'peer_wq': nrm((L, D, PEER_HEADS * PEER_QDIM), D ** -0.5),
        'peer_subkeys': nrm((L, 2, PEER_KEYS, PEER_HALF), PEER_HALF ** -0.5),
        'peer_u': nrm((L, PEER_EXPERTS, D), D ** -0.5),
        'peer_v': nrm((L, PEER_EXPERTS, D), 0.5),
        'final_g': nrm((D,), 0.02, 1.0),
    }


def reference(x_prompt, x_sample, state_rwkv_shift, state_rwkv_wkv, state_mlstm_conv,
              state_mlstm_C, state_mlstm_n, state_mlstm_m, state_ssd_conv, state_ssd,
              c_prompt, c_sample, ada_w, ada_b, norm1_g, norm2_g, w_in, w_out,
              rwkv_mu, rwkv_w0, rwkv_w_up, rwkv_a0, rwkv_a_up, rwkv_g_up, rwkv_k_k,
              rwkv_k_a, rwkv_r_k, rwkv_ln_g, rwkv_ln_b, mlstm_conv_w, mlstm_conv_b,
              mlstm_i_b, mlstm_f_b, mlstm_norm_g, ssd_conv_w, ssd_conv_b, ssd_dt_bias,
              ssd_A_log, ssd_D, ssd_norm_g, peer_wq, peer_subkeys, peer_u, peer_v, final_g):
    params = {'ada_w': ada_w, 'ada_b': ada_b, 'norm1_g': norm1_g, 'norm2_g': norm2_g,
              'w_in': w_in, 'w_out': w_out, 'rwkv_mu': rwkv_mu, 'rwkv_w0': rwkv_w0,
              'rwkv_w_up': rwkv_w_up, 'rwkv_a0': rwkv_a0, 'rwkv_a_up': rwkv_a_up,
              'rwkv_g_up': rwkv_g_up, 'rwkv_k_k': rwkv_k_k, 'rwkv_k_a': rwkv_k_a,
              'rwkv_r_k': rwkv_r_k, 'rwkv_ln_g': rwkv_ln_g, 'rwkv_ln_b': rwkv_ln_b,
              'mlstm_conv_w': mlstm_conv_w, 'mlstm_conv_b': mlstm_conv_b,
              'mlstm_i_b': mlstm_i_b, 'mlstm_f_b': mlstm_f_b, 'mlstm_norm_g': mlstm_norm_g,
              'ssd_conv_w': ssd_conv_w, 'ssd_conv_b': ssd_conv_b, 'ssd_dt_bias': ssd_dt_bias,
              'ssd_A_log': ssd_A_log, 'ssd_D': ssd_D, 'ssd_norm_g': ssd_norm_g,
              'peer_wq': peer_wq, 'peer_subkeys': peer_subkeys, 'peer_u': peer_u,
              'peer_v': peer_v}
    cache = (state_rwkv_shift, state_rwkv_wkv, state_mlstm_conv, state_mlstm_C,
             state_mlstm_n, state_mlstm_m, state_ssd_conv, state_ssd)
    xp, xs = x_prompt, x_sample
    new_p, new_s = [], []
    for l in range(DEPTH):
        p = {name: w[l] for name, w in params.items()}
        xp, sp = layer(xp, c_prompt, zero_state(xp.shape[0], xp.dtype), p)
        xs, ss = layer(xs, c_sample, tuple(s[l] for s in cache), p)
        new_p.append(sp)
        new_s.append(ss)
    y_prompt = rms_norm(xp, final_g)
    y_sample = rms_norm(xs, final_g)

    def stk(states, i):
        return jnp.stack([st[i] for st in states])

    return (y_prompt, y_sample,
            stk(new_p, 0), stk(new_p, 1), stk(new_p, 2), stk(new_p, 3),
            stk(new_p, 4), stk(new_p, 5), stk(new_p, 6), stk(new_p, 7),
            stk(new_s, 0), stk(new_s, 1), stk(new_s, 2), stk(new_s, 3),
            stk(new_s, 4), stk(new_s, 5), stk(new_s, 6), stk(new_s, 7))
```

```python
import functools
import math

import jax
import jax.numpy as jnp
from jax import lax
from jax.experimental import pallas as pl
from jax.experimental.pallas import tpu as pltpu

F32 = jnp.float32
BF16 = jnp.bfloat16
HI = lax.Precision.HIGHEST

D_MODEL = 1024
DEPTH = 4
HEAD_DIM = 64
RWKV_W = 256
RWKV_HEADS = 4
RWKV_DECAY_RANK = 64
RWKV_ICLR_RANK = 64
RWKV_GATE_RANK = 128
RWKV_LN_EPS = 1e-5 * HEAD_DIM
RWKV_PROJ = 3 * RWKV_W + RWKV_DECAY_RANK + RWKV_ICLR_RANK + RWKV_GATE_RANK
MLSTM_W = 256
MLSTM_HEADS = 4
MLSTM_CONV_W = 2 * MLSTM_W
MLSTM_MAIN = MLSTM_CONV_W + 2 * MLSTM_W
SSD_W = 512
SSD_HEADS = 8
SSD_STATE = 128
SSD_GROUPS = 2
SSD_HEADS_PER_GROUP = SSD_HEADS // SSD_GROUPS
SSD_CONV_W = SSD_W + 2 * SSD_GROUPS * SSD_STATE
SSD_MAIN = SSD_W + SSD_CONV_W
CONV_WIDTH = 4
CHUNK = 64
GATE_W = 3 * 128
PEER_KEYS = 128
PEER_EXPERTS = PEER_KEYS * PEER_KEYS
PEER_HEADS = 8
PEER_TOPK = 16
PEER_QDIM = 256
PEER_HALF = 128
NORM_EPS = 1e-6
NEG_BIG = -1e30

VMEM_LIMIT_BYTES = 56 * 1024 * 1024
HIST = 8


def _dot(a, b, prec=None):
    return jnp.dot(a, b, preferred_element_type=F32, precision=prec)


def _dot_nt(a, b, prec=None):
    return lax.dot_general(a, b, (((1,), (1,)), ((), ())), preferred_element_type=F32, precision=prec)


def _dot_tn(a, b, prec=None):
    return lax.dot_general(a, b, (((0,), (0,)), ((), ())), preferred_element_type=F32, precision=prec)


def _sigmoid(x):
    return 1.0 / (1.0 + jnp.exp(-x))


def _silu(x):
    return x * _sigmoid(x)


def _softplus(x):
    return jnp.maximum(x, 0.0) + jnp.log(1.0 + jnp.exp(-jnp.abs(x)))


def _tri(n, strict=False):
    r = lax.broadcasted_iota(jnp.int32, (n, n), 0)
    c = lax.broadcasted_iota(jnp.int32, (n, n), 1)
    return (r > c) if strict else (r >= c)


def _to_row(col, eye):
    return jnp.sum(jnp.where(eye, col, 0.0), axis=0, keepdims=True)


def _params(n_parallel, n_arbitrary=0):
    sem = ("parallel",) * n_parallel + ("arbitrary",) * n_arbitrary
    return pltpu.CompilerParams(dimension_semantics=sem, vmem_limit_bytes=VMEM_LIMIT_BYTES)


def _ada_kernel(c_ref, w_ref, b_ref, o_ref):
    c = _silu(c_ref[...]).astype(BF16)
    o_ref[0] = _dot(c, w_ref[0].astype(BF16)) + b_ref[0]


def _ada_mod(c_all, ada_w, ada_b):
    nb = c_all.shape[0]
    tn = 1536
    return pl.pallas_call(
        _ada_kernel,
        grid=(DEPTH, 6 * D_MODEL // tn),
        in_specs=[pl.BlockSpec((nb, D_MODEL), lambda l, j: (0, 0)),
                  pl.BlockSpec((1, D_MODEL, tn), lambda l, j: (l, 0, j)),
                  pl.BlockSpec((1, 1, tn), lambda l, j: (l, 0, j))],
        out_specs=pl.BlockSpec((1, nb, tn), lambda l, j: (l, 0, j)),
        out_shape=jax.ShapeDtypeStruct((DEPTH, nb, 6 * D_MODEL), F32),
        compiler_params=_params(2),
        name="ada_mod",
    )(c_all, ada_w, ada_b.reshape(DEPTH, 1, 6 * D_MODEL))


def _rms(x, g):
    return x * lax.rsqrt(jnp.mean(x * x, axis=-1, keepdims=True) + NORM_EPS) * g


def _inproj_kernel(x_ref, sh_ref, sc_ref, g_ref, w_ref, pr_ref, pm_ref, ps_ref, pg_ref, *, per_token):
    sh = sh_ref[...] if per_token else sh_ref[0]
    sc = sc_ref[...] if per_token else sc_ref[0]
    h = _rms(x_ref[...], g_ref[...]) * (1.0 + sc) + sh
    p = _dot(h.astype(BF16), w_ref[...])
    o = 0
    for ref, w in ((pr_ref, RWKV_PROJ), (pm_ref, MLSTM_MAIN), (ps_ref, SSD_MAIN), (pg_ref, GATE_W)):
        ref[...] = p[:, o:o + w]
        o += w


def _mod_spec(tm, rows_per_seq, per_token):
    if per_token:
        return pl.BlockSpec((tm, D_MODEL), lambda i: (i, 0))
    return pl.BlockSpec((1, 1, D_MODEL), lambda i: (i * tm // rows_per_seq, 0, 0))


def _inproj(x, sh, sc, g, w, *, tm, rows_per_seq, per_token):
    n = x.shape[0]
    widths = (RWKV_PROJ, MLSTM_MAIN, SSD_MAIN, GATE_W)
    wtot = sum(widths)
    return pl.pallas_call(
        functools.partial(_inproj_kernel, per_token=per_token),
        grid=(n // tm,),
        in_specs=[pl.BlockSpec((tm, D_MODEL), lambda i: (i, 0)),
                  _mod_spec(tm, rows_per_seq, per_token), _mod_spec(tm, rows_per_seq, per_token),
                  pl.BlockSpec((1, D_MODEL), lambda i: (0, 0)),
                  pl.BlockSpec((D_MODEL, wtot), lambda i: (0, 0))],
        out_specs=[pl.BlockSpec((tm, w_), lambda i: (i, 0)) for w_ in widths],
        out_shape=[jax.ShapeDtypeStruct((n, w_), F32) for w_ in widths],
        compiler_params=_params(1),
        name="in_proj",
    )(x, sh, sc, g, w)


def _causal_conv(ext_ref, u, w_ref, b_ref, rows):
    ext_ref[pl.ds(HIST, rows), :] = u
    out = b_ref[...]
    for j in range(CONV_WIDTH):
        out = out + ext_ref[pl.ds(HIST - (CONV_WIDTH - 1) + j, rows), :] * w_ref[pl.ds(j, 1), :]
    return out


def _row_mask(rows, t_valid):
    if t_valid == rows:
        return None
    return lax.broadcasted_iota(jnp.int32, (rows, 1), 0) < t_valid


def _mlstm_kernel(pm_ref, pg_ref, conv0_ref, c0_ref, n0_ref, m0_ref, cw_ref, cb_ref, gb_ref, ng_ref,
                  out_ref, conv_ref, c_ref, n_ref, m_ref, ext_ref, *, rows, t_valid):
    ck = pl.program_id(1)
    hist0 = HIST - (CONV_WIDTH - 1)

    @pl.when(ck == 0)
    def _():
        ext_ref[pl.ds(hist0, CONV_WIDTH - 1), :] = conv0_ref[0]
        c_ref[...] = c0_ref[...]
        n_ref[...] = n0_ref[...]
        m_ref[...] = m0_ref[...]

    pm = pm_ref[0]
    valid = _row_mask(rows, t_valid)
    qk = _silu(_causal_conv(ext_ref, pm[:, :MLSTM_CONV_W], cw_ref, cb_ref, rows))
    conv_ref[0] = ext_ref[pl.ds(hist0 + t_valid, CONV_WIDTH - 1), :]
    ext_ref[pl.ds(hist0, CONV_WIDTH - 1), :] = ext_ref[pl.ds(hist0 + t_valid, CONV_WIDTH - 1), :]
    q_all = qk[:, :MLSTM_W]
    k_all = qk[:, MLSTM_W:] * HEAD_DIM ** -0.5
    v_all = pm[:, MLSTM_CONV_W:MLSTM_CONV_W + MLSTM_W]
    o_all = pm[:, MLSTM_CONV_W + MLSTM_W:]

    gates = pg_ref[0]
    gb = gb_ref[...]
    ig = gates[:, :128] + gb[:, :128]
    fg = gates[:, 128:256] + gb[:, 128:256]
    logf = -_softplus(-fg)
    if valid is not None:
        ig = jnp.where(valid, ig, NEG_BIG)
        logf = jnp.where(valid, logf, 0.0)
    tril = _tri(rows)
    eye = lax.broadcasted_iota(jnp.int32, (rows, rows), 0) == lax.broadcasted_iota(jnp.int32, (rows, rows), 1)
    bcum = _dot(tril.astype(F32), logf, HI)
    m_prev = m_ref[0]
    inter = bcum + m_prev
    b_end = bcum[rows - 1:rows, :]
    ws = b_end - bcum + ig
    m_new = jnp.maximum(b_end + m_prev, jnp.max(ws, axis=0, keepdims=True))
    ws = jnp.exp(ws - m_new)
    a_end = jnp.exp(b_end + m_prev - m_new)
    m_ref[0] = m_new

    for h in range(MLSTM_HEADS):
        sl = slice(h * HEAD_DIM, (h + 1) * HEAD_DIM)
        q, k, v = q_all[:, sl], k_all[:, sl], v_all[:, sl]
        b_col = bcum[:, h:h + 1]
        i_col = ig[:, h:h + 1]
        dmat = jnp.where(tril, b_col - _to_row(b_col, eye) + _to_row(i_col, eye), NEG_BIG)
        m_t = jnp.maximum(inter[:, h:h + 1], jnp.max(dmat, axis=1, keepdims=True))
        s = _dot_nt(q, k) * jnp.exp(dmat - m_t)
        a_in = jnp.exp(inter[:, h:h + 1] - m_t)
        c_h = c_ref[0, h]
        n_h = n_ref[0, pl.ds(h, 1), :]
        num = _dot(s, v) + a_in * _dot_nt(q, c_h)
        den = jnp.sum(s, axis=1, keepdims=True) + a_in * jnp.sum(q * n_h, axis=1, keepdims=True)
        hh = num / jnp.maximum(jnp.abs(den), jnp.exp(-m_t))
        w_col = ws[:, h:h + 1]
        ae = a_end[:, h:h + 1]
        c_ref[0, h] = ae * c_h + _dot_tn(w_col * v, k)
        n_ref[0, pl.ds(h, 1), :] = ae * n_h + jnp.sum(w_col * k, axis=0, keepdims=True)
        hc = hh - jnp.mean(hh, axis=1, keepdims=True)
        hn = hc * lax.rsqrt(jnp.mean(hc * hc, axis=1, keepdims=True) + NORM_EPS)
        out_ref[0, :, sl] = hn * ng_ref[:, sl] * _sigmoid(o_all[:, sl])


def _mlstm(pm, pg, conv0, c0, n0, m0, cw, cb, gb, ng, *, rows, t_valid):
    b, t, _ = pm.shape
    nc = t // rows
    full = lambda *s: pl.BlockSpec(s, lambda i, c: (0,) * len(s))
    per_b = lambda *s: pl.BlockSpec((1,) + s, lambda i, c: (i,) + (0,) * len(s))
    return pl.pallas_call(
        functools.partial(_mlstm_kernel, rows=rows, t_valid=t_valid),
        grid=(b, nc),
        in_specs=[pl.BlockSpec((1, rows, MLSTM_MAIN), lambda i, c: (i, c, 0)),
                  pl.BlockSpec((1, rows, 256), lambda i, c: (i, c, 0)),
                  per_b(CONV_WIDTH - 1, MLSTM_CONV_W), per_b(MLSTM_HEADS, HEAD_DIM, HEAD_DIM),
                  per_b(MLSTM_HEADS, HEAD_DIM), per_b(1, 128),
                  full(CONV_WIDTH, MLSTM_CONV_W), full(1, MLSTM_CONV_W), full(1, GATE_W), full(1, MLSTM_W)],
        out_specs=[pl.BlockSpec((1, rows, MLSTM_W), lambda i, c: (i, c, 0)),
                   per_b(CONV_WIDTH - 1, MLSTM_CONV_W), per_b(MLSTM_HEADS, HEAD_DIM, HEAD_DIM),
                   per_b(MLSTM_HEADS, HEAD_DIM), per_b(1, 128)],
        out_shape=[jax.ShapeDtypeStruct((b, t, MLSTM_W), F32),
                   jax.ShapeDtypeStruct((b, CONV_WIDTH - 1, MLSTM_CONV_W), F32),
                   jax.ShapeDtypeStruct((b, MLSTM_HEADS, HEAD_DIM, HEAD_DIM), F32),
                   jax.ShapeDtypeStruct((b, MLSTM_HEADS, HEAD_DIM), F32),
                   jax.ShapeDtypeStruct((b, 1, 128), F32)],
        scratch_shapes=[pltpu.VMEM((HIST + rows, MLSTM_CONV_W), F32)],
        compiler_params=_params(1, 1),
        name="mlstm_mix",
    )(pm, pg, conv0, c0, n0, m0, cw, cb, gb, ng)


def _ssd_kernel(ps_ref, dt_ref, conv0_ref, h0_ref, cw_ref, cb_ref, gb_ref, alog_ref, dskip_ref, ng_ref,
                out_ref, conv_ref, h_ref, ext_ref, y_ref, *, rows, t_valid):
    ck = pl.program_id(1)
    hist0 = HIST - (CONV_WIDTH - 1)

    @pl.when(ck == 0)
    def _():
        ext_ref[pl.ds(hist0, CONV_WIDTH - 1), :] = conv0_ref[0]
        h_ref[...] = h0_ref[...]

    ps = ps_ref[0]
    valid = _row_mask(rows, t_valid)
    z = ps[:, :SSD_W]
    xbc = _silu(_causal_conv(ext_ref, ps[:, SSD_W:], cw_ref, cb_ref, rows))
    conv_ref[0] = ext_ref[pl.ds(hist0 + t_valid, CONV_WIDTH - 1), :]
    ext_ref[pl.ds(hist0, CONV_WIDTH - 1), :] = ext_ref[pl.ds(hist0 + t_valid, CONV_WIDTH - 1), :]
    xs = xbc[:, :SSD_W]
    gs = SSD_GROUPS * SSD_STATE
    bm = xbc[:, SSD_W:SSD_W + gs]
    cm = xbc[:, SSD_W + gs:]

    dt = _softplus(dt_ref[0] + gb_ref[:, 256:384])
    if valid is not None:
        dt = jnp.where(valid, dt, 0.0)
    la = dt * (-jnp.exp(alog_ref[...]))
    tril = _tri(rows)
    eye = lax.broadcasted_iota(jnp.int32, (rows, rows), 0) == lax.broadcasted_iota(jnp.int32, (rows, rows), 1)
    acum = _dot(tril.astype(F32), la, HI)
    dskip = dskip_ref[...]

    for g in range(SSD_GROUPS):
        b_g = bm[:, g * SSD_STATE:(g + 1) * SSD_STATE]
        c_g = cm[:, g * SSD_STATE:(g + 1) * SSD_STATE]
        cb = _dot_nt(c_g, b_g)
        for j in range(SSD_HEADS_PER_GROUP):
            hd = g * SSD_HEADS_PER_GROUP + j
            sl = slice(hd * HEAD_DIM, (hd + 1) * HEAD_DIM)
            x = xs[:, sl]
            a_col = acum[:, hd:hd + 1]
            dt_col = dt[:, hd:hd + 1]
            seg = jnp.exp(jnp.where(tril, a_col - _to_row(a_col, eye), NEG_BIG))
            wmat = seg * cb * _to_row(dt_col, eye)
            h_h = h_ref[0, hd]
            y = _dot(wmat, x) + jnp.exp(a_col) * _dot_nt(c_g, h_h)
            a_last = a_col[rows - 1:rows, :]
            w_end = jnp.exp(a_last - a_col) * dt_col
            h_ref[0, hd] = jnp.exp(a_last) * h_h + _dot_tn(w_end * x, b_g)
            y_ref[:, sl] = y + dskip[:, hd:hd + 1] * x
    y = y_ref[...] * _silu(z)
    out_ref[0] = _rms(y, ng_ref[...])


def _ssd(ps, pg, conv0, h0, cw, cb, gb, alog, dskip, ng, *, rows, t_valid):
    b, t, _ = ps.shape
    nc = t // rows
    full = lambda *s: pl.BlockSpec(s, lambda i, c: (0,) * len(s))
    per_b = lambda *s: pl.BlockSpec((1,) + s, lambda i, c: (i,) + (0,) * len(s))
    return pl.pallas_call(
        functools.partial(_ssd_kernel, rows=rows, t_valid=t_valid),
        grid=(b, nc),
        in_specs=[pl.BlockSpec((1, rows, SSD_MAIN), lambda i, c: (i, c, 0)),
                  pl.BlockSpec((1, rows, 128), lambda i, c: (i, c, 2)),
                  per_b(CONV_WIDTH - 1, SSD_CONV_W), per_b(SSD_HEADS, HEAD_DIM, SSD_STATE),
                  full(CONV_WIDTH, SSD_CONV_W), full(1, SSD_CONV_W), full(1, GATE_W),
                  full(1, 128), full(1, 128), full(1, SSD_W)],
        out_specs=[pl.BlockSpec((1, rows, SSD_W), lambda i, c: (i, c, 0)),
                   per_b(CONV_WIDTH - 1, SSD_CONV_W), per_b(SSD_HEADS, HEAD_DIM, SSD_STATE)],
        out_shape=[jax.ShapeDtypeStruct((b, t, SSD_W), F32),
                   jax.ShapeDtypeStruct((b, CONV_WIDTH - 1, SSD_CONV_W), F32),
                   jax.ShapeDtypeStruct((b, SSD_HEADS, HEAD_DIM, SSD_STATE), F32)],
        scratch_shapes=[pltpu.VMEM((HIST + rows, SSD_CONV_W), F32), pltpu.VMEM((rows, SSD_W), F32)],
        compiler_params=_params(1, 1),
        name="ssd_mix",
    )(ps, pg, conv0, h0, cw, cb, gb, alog, dskip, ng)


def _rwkv_kernel(pr_ref, shift0_ref, s0_ref, mu_ref, w0_ref, wup_ref, a0_ref, aup_ref, gup_ref, kk_ref, ka_ref,
                 rk_ref, lng_ref, lnb_ref, bd_ref,
                 out_ref, shift_ref, s_ref, ext_ref, y_ref, *, rows, t_valid):
    ck = pl.program_id(1)

    @pl.when(ck == 0)
    def _():
        ext_ref[pl.ds(HIST - 1, 1), :] = shift0_ref[0]
        s_ref[...] = s0_ref[...]

    p = pr_ref[0]
    ext_ref[pl.ds(HIST, rows), :] = p
    prev = ext_ref[pl.ds(HIST - 1, rows), :]
    shift_ref[0] = ext_ref[pl.ds(HIST - 1 + t_valid, 1), :]
    ext_ref[pl.ds(HIST - 1, 1), :] = ext_ref[pl.ds(HIST - 1 + t_valid, 1), :]
    x = p + (prev - p) * mu_ref[...]
    w3 = 3 * RWKV_W
    r = x[:, :RWKV_W]
    k = x[:, RWKV_W:2 * RWKV_W]
    v = x[:, 2 * RWKV_W:w3]
    xw = x[:, w3:w3 + RWKV_DECAY_RANK]
    xa = x[:, w3 + RWKV_DECAY_RANK:w3 + RWKV_DECAY_RANK + RWKV_ICLR_RANK]
    xg = x[:, w3 + RWKV_DECAY_RANK + RWKV_ICLR_RANK:]
    w_log = -_softplus(-(w0_ref[...] + _dot(jnp.tanh(xw).astype(BF16), wup_ref[...]))) - 0.5
    logw = -jnp.exp(w_log)
    a = _sigmoid(a0_ref[...] + _dot(xa.astype(BF16), aup_ref[...]))
    g = _dot(_sigmoid(xg).astype(BF16), gup_ref[...])
    bd = bd_ref[...]
    kk = k * kk_ref[...]
    kk = kk / jnp.maximum(jnp.sqrt(_dot(kk * kk, bd, HI)), 1e-12)
    k = k * (1.0 + (a - 1.0) * ka_ref[...])
    valid = _row_mask(rows, t_valid)
    if valid is not None:
        logw = jnp.where(valid, logw, 0.0)
        kk = jnp.where(valid, kk, 0.0)
        k = jnp.where(valid, k, 0.0)

    tril = _tri(rows)
    stril = _tri(rows, strict=True)
    cum = _dot(tril.astype(F32), logw, HI)
    cum_end = cum[rows - 1:rows, :]
    p_in = jnp.exp(cum)
    p_inv = jnp.exp(-cum)
    a_t = -kk * jnp.exp(cum - logw)
    kka = kk * a
    b_t = kka * p_inv
    k_t = k * p_inv
    r_t = r * p_in
    to_end = jnp.exp(cum_end - cum)
    b_e = kka * to_end
    k_e = k * to_end
    p_end = jnp.exp(cum_end)

    for h in range(RWKV_HEADS):
        sl = slice(h * HEAD_DIM, (h + 1) * HEAD_DIM)
        s_h = s_ref[0, h]
        ah, bh, kh, rh, vh = a_t[:, sl], b_t[:, sl], k_t[:, sl], r_t[:, sl], v[:, sl]
        n_mat = jnp.where(stril, _dot_nt(ah, bh, HI), 0.0)
        u = _dot_nt(ah, s_h, HI) + _dot(jnp.where(stril, _dot_nt(ah, kh, HI), 0.0), vh, HI)
        span = 1
        while span < rows:
            u = u + _dot(n_mat, u, HI)
            span *= 2
            if span < rows:
                n_mat = _dot(n_mat, n_mat, HI)
        y = (_dot_nt(rh, s_h, HI)
             + _dot(jnp.where(tril, _dot_nt(rh, bh, HI), 0.0), u, HI)
             + _dot(jnp.where(tril, _dot_nt(rh, kh, HI), 0.0), vh, HI))
        s_ref[0, h] = p_end[:, sl] * s_h + _dot_tn(u, b_e[:, sl], HI) + _dot_tn(vh, k_e[:, sl], HI)
        y_ref[:, sl] = y

    y = y_ref[...]
    inv_n = 1.0 / HEAD_DIM
    yc = y - _dot(y, bd, HI) * inv_n
    yn = yc * lax.rsqrt(_dot(yc * yc, bd, HI) * inv_n + RWKV_LN_EPS)
    bonus = _dot(r * k * rk_ref[...], bd, HI) * v
    out_ref[0] = (yn * lng_ref[...] + lnb_ref[...] + bonus) * g


def _rwkv(pr, shift0, s0, mu, w0, wup, a0, aup, gup, kkp, kap, rk, lng, lnb, bd, *, rows, t_valid):
    b, t, _ = pr.shape
    nc = t // rows
    full = lambda *s: pl.BlockSpec(s, lambda i, c: (0,) * len(s))
    per_b = lambda *s: pl.BlockSpec((1,) + s, lambda i, c: (i,) + (0,) * len(s))
    return pl.pallas_call(
        functools.partial(_rwkv_kernel, rows=rows, t_valid=t_valid),
        grid=(b, nc),
        in_specs=[pl.BlockSpec((1, rows, RWKV_PROJ), lambda i, c: (i, c, 0)),
                  per_b(1, RWKV_PROJ), per_b(RWKV_HEADS, HEAD_DIM, HEAD_DIM),
                  full(1, RWKV_PROJ), full(1, RWKV_W), full(RWKV_DECAY_RANK, RWKV_W), full(1, RWKV_W),
                  full(RWKV_ICLR_RANK, RWKV_W), full(RWKV_GATE_RANK, RWKV_W), full(1, RWKV_W), full(1, RWKV_W),
                  full(1, RWKV_W), full(1, RWKV_W), full(1, RWKV_W), full(RWKV_W, RWKV_W)],
        out_specs=[pl.BlockSpec((1, rows, RWKV_W), lambda i, c: (i, c, 0)),
                   per_b(1, RWKV_PROJ), per_b(RWKV_HEADS, HEAD_DIM, HEAD_DIM)],
        out_shape=[jax.ShapeDtypeStruct((b, t, RWKV_W), F32),
                   jax.ShapeDtypeStruct((b, 1, RWKV_PROJ), F32),
                   jax.ShapeDtypeStruct((b, RWKV_HEADS, HEAD_DIM, HEAD_DIM), F32)],
        scratch_shapes=[pltpu.VMEM((HIST + rows, RWKV_PROJ), F32), pltpu.VMEM((rows, RWKV_W), F32)],
        compiler_params=_params(1, 1),
        name="rwkv_mix",
    )(pr, shift0, s0, mu, w0, wup, a0, aup, gup, kkp, kap, rk, lng, lnb, bd)


def _outproj_kernel(or_ref, om_ref, os_ref, x_ref, g1_ref, sh_ref, sc_ref, ng_ref, w_ref, xmid_ref, h2_ref,
                    *, per_token):
    pick = (lambda r: r[...]) if per_token else (lambda r: r[0])
    w = w_ref[...]
    mix = (_dot(or_ref[...].astype(BF16), w[:RWKV_W])
           + _dot(om_ref[...].astype(BF16), w[RWKV_W:RWKV_W + MLSTM_W])
           + _dot(os_ref[...].astype(BF16), w[RWKV_W + MLSTM_W:]))
    x = x_ref[...] + pick(g1_ref) * mix
    xmid_ref[...] = x
    h2_ref[...] = (_rms(x, ng_ref[...]) * (1.0 + pick(sc_ref)) + pick(sh_ref)).astype(BF16)


def _outproj(o_r, o_m, o_s, x, g1, sh, sc, ng, w, *, tm, rows_per_seq, per_token):
    n = x.shape[0]
    tok = lambda w_: pl.BlockSpec((tm, w_), lambda i: (i, 0))
    ms = _mod_spec(tm, rows_per_seq, per_token)
    return pl.pallas_call(
        functools.partial(_outproj_kernel, per_token=per_token),
        grid=(n // tm,),
        in_specs=[tok(RWKV_W), tok(MLSTM_W), tok(SSD_W), tok(D_MODEL), ms, ms, ms,
                  pl.BlockSpec((1, D_MODEL), lambda i: (0, 0)),
                  pl.BlockSpec((D_MODEL, D_MODEL), lambda i: (0, 0))],
        out_specs=[tok(D_MODEL), tok(D_MODEL)],
        out_shape=[jax.ShapeDtypeStruct((n, D_MODEL), F32), jax.ShapeDtypeStruct((n, D_MODEL), BF16)],
        compiler_params=_params(1),
        name="out_proj",
    )(o_r, o_m, o_s, x, g1, sh, sc, ng, w)


def _top1(x, iota):
    m = jnp.max(x, axis=0, keepdims=True)
    idx = jnp.min(jnp.where(x == m, iota, float(x.shape[0])), axis=0, keepdims=True)
    return m, idx


def _route_kernel(h2_ref, wq_ref, sk_ref, w3_ref, s_ref, cand_ref, v_ref, i_ref, top_ref, at_ref, bt_ref, gt_ref,
                  a_ref, b_ref, g_ref, *, tg):
    hd = pl.program_id(1)
    q = _dot(h2_ref[...], wq_ref[...]).astype(BF16)
    key_iota = lax.broadcasted_iota(jnp.int32, (PEER_KEYS, tg), 0).astype(F32)
    for c in range(2):
        s_ref[...] = _dot_nt(sk_ref[c], q[:, c * PEER_HALF:(c + 1) * PEER_HALF])

        def pick_key(p, carry, c=c):
            s = s_ref[...]
            m, idx = _top1(s, key_iota)
            s_ref[...] = jnp.where(key_iota == idx, -jnp.inf, s)
            v_ref[c, pl.ds(p, 1), :] = m
            i_ref[c, pl.ds(p, 1), :] = idx
            return carry

        lax.fori_loop(0, PEER_TOPK, pick_key, 0)

    v1, v2 = v_ref[0], v_ref[1]
    i1, i2 = i_ref[0], i_ref[1]
    for p in range(PEER_TOPK):
        cand_ref[pl.ds(p * PEER_TOPK, PEER_TOPK), :] = v1[p:p + 1, :] + v2
    ncand = PEER_TOPK * PEER_TOPK
    cand_iota = lax.broadcasted_iota(jnp.int32, (ncand, tg), 0).astype(F32)
    k_iota = lax.broadcasted_iota(jnp.int32, (PEER_TOPK, tg), 0).astype(F32)
    row0 = pl.multiple_of(hd * PEER_TOPK, PEER_TOPK)

    def pick_cand(k, carry):
        cand = cand_ref[...]
        m, pos = _top1(cand, cand_iota)
        cand_ref[...] = jnp.where(cand_iota == pos, -jnp.inf, cand)
        psel = jnp.floor(pos * (1.0 / PEER_TOPK))
        qsel = pos - psel * PEER_TOPK
        top_ref[pl.ds(k, 1), :] = m
        at_ref[pl.ds(row0 + k, 1), :] = jnp.sum(jnp.where(k_iota == psel, i1, 0.0), axis=0, keepdims=True)
        bt_ref[pl.ds(row0 + k, 1), :] = jnp.sum(jnp.where(k_iota == qsel, i2, 0.0), axis=0, keepdims=True)
        return carry

    lax.fori_loop(0, PEER_TOPK, pick_cand, 0)
    top = top_ref[...]
    e = jnp.exp(top - top[0:1, :])
    gt_ref[pl.ds(row0, PEER_TOPK), :] = e / jnp.sum(e, axis=0, keepdims=True)

    @pl.when(hd == PEER_HEADS - 1)
    def _():
        a_ref[...] = at_ref[...].T
        b_ref[...] = bt_ref[...].T
        g_ref[...] = gt_ref[...].T
        j_iota = lax.broadcasted_iota(jnp.int32, (PEER_KEYS, PEER_HEADS * PEER_TOPK), 0).astype(F32)

        def per_token(t, carry):
            sel1 = jnp.where(j_iota == a_ref[pl.ds(t, 1), :], 1.0, 0.0).astype(BF16)
            sel2 = jnp.where(j_iota == b_ref[pl.ds(t, 1), :], g_ref[pl.ds(t, 1), :], 0.0).astype(BF16)
            w3_ref[t] = _dot_nt(sel1, sel2).astype(BF16)
            return carry

        lax.fori_loop(0, tg, per_token, 0)


def _route(h2, wq, sk, *, tg):
    n = h2.shape[0]
    npick = PEER_HEADS * PEER_TOPK
    return pl.pallas_call(
        functools.partial(_route_kernel, tg=tg),
        grid=(n // tg, PEER_HEADS),
        in_specs=[pl.BlockSpec((tg, D_MODEL), lambda i, h: (i, 0)),
                  pl.BlockSpec((D_MODEL, PEER_QDIM), lambda i, h: (0, h)),
                  pl.BlockSpec((2, PEER_KEYS, PEER_HALF), lambda i, h: (0, 0, 0))],
        out_specs=pl.BlockSpec((tg, PEER_KEYS, PEER_KEYS), lambda i, h: (i, 0, 0)),
        out_shape=jax.ShapeDtypeStruct((n, PEER_KEYS, PEER_KEYS), BF16),
        scratch_shapes=[pltpu.VMEM((PEER_KEYS, tg), F32), pltpu.VMEM((PEER_TOPK * PEER_TOPK, tg), F32),
                        pltpu.VMEM((2, PEER_TOPK, tg), F32), pltpu.VMEM((2, PEER_TOPK, tg), F32),
                        pltpu.VMEM((PEER_TOPK, tg), F32),
                        pltpu.VMEM((npick, tg), F32), pltpu.VMEM((npick, tg), F32), pltpu.VMEM((npick, tg), F32),
                        pltpu.VMEM((tg, npick), F32), pltpu.VMEM((tg, npick), F32), pltpu.VMEM((tg, npick), F32)],
        compiler_params=_params(1, 1),
        name="peer_route",
    )(h2, wq, sk)


def _gelu_tanh(x):
    return x * (0.5 * (1.0 + jnp.tanh(math.sqrt(2.0 / math.pi) * (x + 0.044715 * (x * x * x)))))


def _experts_kernel(h2_ref, w_ref, u_ref, v_ref, xmid_ref, g2_ref, fg_ref, o_ref, acc_ref, *, per_token, final_norm):
    j = pl.program_id(1)

    @pl.when(j == 0)
    def _():
        acc_ref[...] = jnp.zeros_like(acc_ref)

    s = _dot_nt(h2_ref[...], u_ref[...])
    act = _gelu_tanh(s) * w_ref[...].astype(F32)
    acc_ref[...] += _dot(act.astype(BF16), v_ref[...])

    @pl.when(j == pl.num_programs(1) - 1)
    def _():
        g2 = g2_ref[...] if per_token else g2_ref[0]
        x = xmid_ref[...] + g2 * acc_ref[...]
        o_ref[...] = _rms(x, fg_ref[...]) if final_norm else x


def _experts(h2, w, u, v, xmid, g2, fg, *, tm, eb, rows_per_seq, per_token, final_norm):
    n = h2.shape[0]
    if per_token:
        ms = pl.BlockSpec((tm, D_MODEL), lambda i, j: (i, 0))
    else:
        ms = pl.BlockSpec((1, 1, D_MODEL), lambda i, j: (i * tm // rows_per_seq, 0, 0))
    return pl.pallas_call(
        functools.partial(_experts_kernel, per_token=per_token, final_norm=final_norm),
        grid=(n // tm, PEER_EXPERTS // eb),
        in_specs=[pl.BlockSpec((tm, D_MODEL), lambda i, j: (i, 0)),
                  pl.BlockSpec((tm, eb), lambda i, j: (i, j)),
                  pl.BlockSpec((eb, D_MODEL), lambda i, j: (j, 0)),
                  pl.BlockSpec((eb, D_MODEL), lambda i, j: (j, 0)),
                  pl.BlockSpec((tm, D_MODEL), lambda i, j: (i, 0)),
                  ms,
                  pl.BlockSpec((1, D_MODEL), lambda i, j: (0, 0))],
        out_specs=pl.BlockSpec((tm, D_MODEL), lambda i, j: (i, 0)),
        out_shape=jax.ShapeDtypeStruct((n, D_MODEL), F32),
        scratch_shapes=[pltpu.VMEM((tm, D_MODEL), F32)],
        compiler_params=_params(1, 1),
        name="peer_experts",
    )(h2, w, u, v, xmid, g2, fg)


TOKEN_TILE = 512
ROUTE_TILE = 256
EXPERT_BLOCK = 1024
SAMPLE_ROWS = 8


def _lane_pad(vec, offset=0, width=128):
    return jnp.zeros((1, width), F32).at[0, offset:offset + vec.shape[0]].set(vec)


def _layer_weights(l, w):
    w_in = w['w_in'][l]
    o_m = RWKV_PROJ
    o_mg = o_m + MLSTM_MAIN
    o_s = o_mg + 2 * MLSTM_HEADS
    o_sg = o_s + SSD_MAIN
    gate_cols = jnp.zeros((D_MODEL, GATE_W), F32)
    gate_cols = gate_cols.at[:, 0:MLSTM_HEADS].set(w_in[:, o_mg:o_mg + MLSTM_HEADS])
    gate_cols = gate_cols.at[:, 128:128 + MLSTM_HEADS].set(w_in[:, o_mg + MLSTM_HEADS:o_s])
    gate_cols = gate_cols.at[:, 256:256 + SSD_HEADS].set(w_in[:, o_sg:o_sg + SSD_HEADS])
    w_cat = jnp.concatenate([w_in[:, :o_m], w_in[:, o_m:o_mg], w_in[:, o_s:o_sg], gate_cols], axis=1).astype(BF16)
    gate_bias = jnp.concatenate([_lane_pad(w['mlstm_i_b'][l]), _lane_pad(w['mlstm_f_b'][l]),
                                 _lane_pad(w['ssd_dt_bias'][l])], axis=1)
    row = lambda name: w[name][l].reshape(1, -1)
    return dict(
        w_cat=w_cat, gate_bias=gate_bias, w_out=w['w_out'][l].astype(BF16),
        norm1_g=row('norm1_g'), norm2_g=row('norm2_g'),
        rwkv=(row('rwkv_mu'), row('rwkv_w0'), w['rwkv_w_up'][l].astype(BF16), row('rwkv_a0'),
              w['rwkv_a_up'][l].astype(BF16), w['rwkv_g_up'][l].astype(BF16), row('rwkv_k_k'), row('rwkv_k_a'),
              row('rwkv_r_k'), row('rwkv_ln_g'), row('rwkv_ln_b')),
        mlstm=(w['mlstm_conv_w'][l], row('mlstm_conv_b'), gate_bias, row('mlstm_norm_g')),
        ssd=(w['ssd_conv_w'][l], row('ssd_conv_b'), gate_bias, _lane_pad(w['ssd_A_log'][l]),
             _lane_pad(w['ssd_D'][l]), row('ssd_norm_g')),
        wq=w['peer_wq'][l].astype(BF16), sk=w['peer_subkeys'][l].astype(BF16),
        u=w['peer_u'][l].astype(BF16), v=w['peer_v'][l].astype(BF16),
    )


def _run_layer(x, mods, state, lw, head_sum, final_g, *, batch, seq, rows, per_token, final_norm):
    sh1, sc1, g1, sh2, sc2, g2 = mods
    shift, wkv, mconv, m_c, m_n, m_m, sconv, sst = state
    tm = min(TOKEN_TILE, batch * seq)
    tok = dict(tm=tm, rows_per_seq=seq, per_token=per_token)
    p_r, p_m, p_s, p_g = _inproj(x, sh1, sc1, lw['norm1_g'], lw['w_cat'], **tok)

    t_pad = -(-seq // rows) * rows
    t_valid = seq if seq < rows else rows

    def seqs(a):
        a = a.reshape(batch, seq, a.shape[-1])
        return a if t_pad == seq else jnp.pad(a, ((0, 0), (0, t_pad - seq), (0, 0)))

    def toks(a):
        return a[:, :seq].reshape(batch * seq, a.shape[-1])

    mix = dict(rows=rows, t_valid=t_valid)
    o_r, shift, wkv = _rwkv(seqs(p_r), shift[:, None, :], wkv, *lw['rwkv'], head_sum, **mix)
    m_m = jnp.pad(m_m, ((0, 0), (0, 128 - MLSTM_HEADS)))[:, None, :]
    o_m, mconv, m_c, m_n, m_m = _mlstm(seqs(p_m), seqs(p_g), mconv, m_c, m_n, m_m, *lw['mlstm'], **mix)
    o_s, sconv, sst = _ssd(seqs(p_s), seqs(p_g), sconv, sst, *lw['ssd'], **mix)
    new_state = (shift[:, 0, :], wkv, mconv, m_c, m_n, m_m[:, 0, :MLSTM_HEADS], sconv, sst)

    x_mid, h2 = _outproj(toks(o_r), toks(o_m), toks(o_s), x, g1, sh2, sc2, lw['norm2_g'], lw['w_out'], **tok)
    gate_mass = _route(h2, lw['wq'], lw['sk'], tg=min(ROUTE_TILE, batch * seq))
    gate_mass = gate_mass.reshape(batch * seq, PEER_EXPERTS)
    x_new = _experts(h2, gate_mass, lw['u'], lw['v'], x_mid, g2, final_g, eb=EXPERT_BLOCK,
                     final_norm=final_norm, **tok)
    return x_new, new_state


def kernel(x_prompt, x_sample, state_rwkv_shift, state_rwkv_wkv, state_mlstm_conv, state_mlstm_C, state_mlstm_n, state_mlstm_m, state_ssd_conv, state_ssd, c_prompt, c_sample, ada_w, ada_b, norm1_g, norm2_g, w_in, w_out, rwkv_mu, rwkv_w0, rwkv_w_up, rwkv_a0, rwkv_a_up, rwkv_g_up, rwkv_k_k, rwkv_k_a, rwkv_r_k, rwkv_ln_g, rwkv_ln_b, mlstm_conv_w, mlstm_conv_b, mlstm_i_b, mlstm_f_b, mlstm_norm_g, ssd_conv_w, ssd_conv_b, ssd_dt_bias, ssd_A_log, ssd_D, ssd_norm_g, peer_wq, peer_subkeys, peer_u, peer_v, final_g):
    weights = dict(norm1_g=norm1_g, norm2_g=norm2_g, w_in=w_in, w_out=w_out, rwkv_mu=rwkv_mu, rwkv_w0=rwkv_w0,
                   rwkv_w_up=rwkv_w_up, rwkv_a0=rwkv_a0, rwkv_a_up=rwkv_a_up, rwkv_g_up=rwkv_g_up,
                   rwkv_k_k=rwkv_k_k, rwkv_k_a=rwkv_k_a, rwkv_r_k=rwkv_r_k, rwkv_ln_g=rwkv_ln_g,
                   rwkv_ln_b=rwkv_ln_b, mlstm_conv_w=mlstm_conv_w, mlstm_conv_b=mlstm_conv_b,
                   mlstm_i_b=mlstm_i_b, mlstm_f_b=mlstm_f_b, mlstm_norm_g=mlstm_norm_g, ssd_conv_w=ssd_conv_w,
                   ssd_conv_b=ssd_conv_b, ssd_dt_bias=ssd_dt_bias, ssd_A_log=ssd_A_log, ssd_D=ssd_D,
                   ssd_norm_g=ssd_norm_g, peer_wq=peer_wq, peer_subkeys=peer_subkeys, peer_u=peer_u, peer_v=peer_v)
    nb, seq, d = x_prompt.shape
    nd, dseq, _ = x_sample.shape
    cache = (state_rwkv_shift, state_rwkv_wkv, state_mlstm_conv, state_mlstm_C, state_mlstm_n, state_mlstm_m,
             state_ssd_conv, state_ssd)

    mod = _ada_mod(jnp.concatenate([c_prompt, c_sample], axis=0), ada_w, ada_b)
    head_id = jnp.arange(RWKV_W) // HEAD_DIM
    head_sum = (head_id[:, None] == head_id[None, :]).astype(F32)
    fg = final_g.reshape(1, d)

    xp = x_prompt.reshape(nb * seq, d)
    xs = x_sample.reshape(nd * dseq, d)
    new_p, new_s = [], []
    for l in range(DEPTH):
        lw = _layer_weights(l, weights)
        last = l == DEPTH - 1
        mod_p = mod[l, :nb].reshape(nb, 6, 1, d)
        mods_p = tuple(mod_p[:, i] for i in range(6))
        mod_s = jnp.repeat(mod[l, nb:].reshape(nd, 6, d), dseq, axis=0)
        mods_s = tuple(mod_s[:, i] for i in range(6))
        zeros = tuple(jnp.zeros((nb,) + s.shape[2:], F32) for s in cache)
        xp, sp = _run_layer(xp, mods_p, zeros, lw, head_sum, fg, batch=nb, seq=seq, rows=CHUNK,
                            per_token=False, final_norm=last)
        xs, ss = _run_layer(xs, mods_s, tuple(s[l] for s in cache), lw, head_sum, fg, batch=nd, seq=dseq,
                            rows=SAMPLE_ROWS, per_token=True, final_norm=last)
        new_p.append(sp)
        new_s.append(ss)

    def stk(states, i):
        return jnp.stack([st[i] for st in states])

    return ((xp.reshape(nb, seq, d), xs.reshape(nd, dseq, d))
            + tuple(stk(new_p, i) for i in range(8)) + tuple(stk(new_s, i) for i in range(8)))
```

```python
import functools
import math

import jax
import jax.numpy as jnp
from jax import lax
from jax.experimental import pallas as pl
from jax.experimental.pallas import tpu as pltpu

F32 = jnp.float32
BF16 = jnp.bfloat16

D_MODEL = 1024
DEPTH = 4
HEAD_DIM = 64
RWKV_W = 256
RWKV_HEADS = 4
RWKV_DECAY_RANK = 64
RWKV_ICLR_RANK = 64
RWKV_GATE_RANK = 128
RWKV_LN_EPS = 1e-5 * HEAD_DIM
RWKV_PROJ = 3 * RWKV_W + RWKV_DECAY_RANK + RWKV_ICLR_RANK + RWKV_GATE_RANK
MLSTM_W = 256
MLSTM_HEADS = 4
MLSTM_CONV_W = 2 * MLSTM_W
MLSTM_MAIN = MLSTM_CONV_W + 2 * MLSTM_W
SSD_W = 512
SSD_HEADS = 8
SSD_STATE = 128
SSD_GROUPS = 2
SSD_HEADS_PER_GROUP = SSD_HEADS // SSD_GROUPS
SSD_CONV_W = SSD_W + 2 * SSD_GROUPS * SSD_STATE
SSD_MAIN = SSD_W + SSD_CONV_W
CONV_WIDTH = 4
CHUNK = 64
GATE_W = 3 * 128
PEER_KEYS = 128
PEER_EXPERTS = PEER_KEYS * PEER_KEYS
PEER_HEADS = 8
PEER_TOPK = 16
PEER_QDIM = 256
PEER_HALF = 128
NORM_EPS = 1e-6
NEG_BIG = -1e30
TOKEN_GROUP = 16
TOKEN_PITCH = PEER_KEYS + 8

VMEM_LIMIT_BYTES = 56 * 1024 * 1024
HIST = 8


def _dot(a, b, prec=None):
    return jnp.dot(a, b, preferred_element_type=F32, precision=prec)


def _dot_nt(a, b, prec=None):
    return lax.dot_general(a, b, (((1,), (1,)), ((), ())), preferred_element_type=F32, precision=prec)


def _dot_tn(a, b, prec=None):
    return lax.dot_general(a, b, (((0,), (0,)), ((), ())), preferred_element_type=F32, precision=prec)


def _hi_lo(x):
    hi = x.astype(BF16)
    return hi, (x - hi.astype(F32)).astype(BF16)


def _mm(a, b, form, b_exact=False):
    ah, al = _hi_lo(a)
    a_axis = 0 if form == 'tn' else 1
    b_axis = 1 if form == 'nt' else 0
    if b_exact:
        bb = b.astype(BF16)
        a3 = jnp.concatenate([ah, al], axis=a_axis)
        b3 = jnp.concatenate([bb, bb], axis=b_axis)
    else:
        bh, bl = _hi_lo(b)
        a3 = jnp.concatenate([ah, ah, al], axis=a_axis)
        b3 = jnp.concatenate([bh, bl, bh], axis=b_axis)
    return {'nn': _dot, 'nt': _dot_nt, 'tn': _dot_tn}[form](a3, b3)


def _cumsum_rows(x, tril_bf16):
    hi = x.astype(BF16)
    r1 = x - hi.astype(F32)
    mid = r1.astype(BF16)
    lo = (r1 - mid.astype(F32)).astype(BF16)
    return _dot(jnp.concatenate([tril_bf16] * 3, axis=1), jnp.concatenate([hi, mid, lo], axis=0))


def _sigmoid(x):
    return 1.0 / (1.0 + jnp.exp(-x))


def _silu(x):
    return x * _sigmoid(x)


def _softplus(x):
    return jnp.maximum(x, 0.0) + jnp.log(1.0 + jnp.exp(-jnp.abs(x)))


def _tri(n, strict=False):
    r = lax.broadcasted_iota(jnp.int32, (n, n), 0)
    c = lax.broadcasted_iota(jnp.int32, (n, n), 1)
    return (r > c) if strict else (r >= c)


def _to_row(col, eye):
    return jnp.sum(jnp.where(eye, col, 0.0), axis=0, keepdims=True)


def _params(n_parallel, n_arbitrary=0):
    sem = ("parallel",) * n_parallel + ("arbitrary",) * n_arbitrary
    return pltpu.CompilerParams(dimension_semantics=sem, vmem_limit_bytes=VMEM_LIMIT_BYTES)


def _ada_kernel(c_ref, w_ref, b_ref, o_ref):
    c = _silu(c_ref[...]).astype(BF16)
    o_ref[0] = _dot(c, w_ref[0].astype(BF16)) + b_ref[0]


def _ada_mod(c_all, ada_w, ada_b):
    nb = c_all.shape[0]
    tn = 1536
    return pl.pallas_call(
        _ada_kernel,
        grid=(DEPTH, 6 * D_MODEL // tn),
        in_specs=[pl.BlockSpec((nb, D_MODEL), lambda l, j: (0, 0)),
                  pl.BlockSpec((1, D_MODEL, tn), lambda l, j: (l, 0, j)),
                  pl.BlockSpec((1, 1, tn), lambda l, j: (l, 0, j))],
        out_specs=pl.BlockSpec((1, nb, tn), lambda l, j: (l, 0, j)),
        out_shape=jax.ShapeDtypeStruct((DEPTH, nb, 6 * D_MODEL), F32),
        compiler_params=_params(2),
        name="ada_mod",
    )(c_all, ada_w, ada_b.reshape(DEPTH, 1, 6 * D_MODEL))


def _rms(x, g):
    return x * lax.rsqrt(jnp.mean(x * x, axis=-1, keepdims=True) + NORM_EPS) * g


def _inproj_kernel(x_ref, sh_ref, sc_ref, g_ref, w_ref, pr_ref, pm_ref, ps_ref, pg_ref, *, per_token):
    sh = sh_ref[...] if per_token else sh_ref[0]
    sc = sc_ref[...] if per_token else sc_ref[0]
    h = _rms(x_ref[...], g_ref[...]) * (1.0 + sc) + sh
    p = _dot(h.astype(BF16), w_ref[...])
    o = 0
    for ref, w in ((pr_ref, RWKV_PROJ), (pm_ref, MLSTM_MAIN), (ps_ref, SSD_MAIN), (pg_ref, GATE_W)):
        ref[...] = p[:, o:o + w]
        o += w


def _mod_spec(tm, rows_per_seq, per_token):
    if per_token:
        return pl.BlockSpec((tm, D_MODEL), lambda i: (i, 0))
    return pl.BlockSpec((1, 1, D_MODEL), lambda i: (i * tm // rows_per_seq, 0, 0))


def _inproj(x, sh, sc, g, w, *, tm, rows_per_seq, per_token):
    n = x.shape[0]
    widths = (RWKV_PROJ, MLSTM_MAIN, SSD_MAIN, GATE_W)
    wtot = sum(widths)
    return pl.pallas_call(
        functools.partial(_inproj_kernel, per_token=per_token),
        grid=(n // tm,),
        in_specs=[pl.BlockSpec((tm, D_MODEL), lambda i: (i, 0)),
                  _mod_spec(tm, rows_per_seq, per_token), _mod_spec(tm, rows_per_seq, per_token),
                  pl.BlockSpec((1, D_MODEL), lambda i: (0, 0)),
                  pl.BlockSpec((D_MODEL, wtot), lambda i: (0, 0))],
        out_specs=[pl.BlockSpec((tm, w_), lambda i: (i, 0)) for w_ in widths],
        out_shape=[jax.ShapeDtypeStruct((n, w_), F32) for w_ in widths],
        compiler_params=_params(1),
        name="in_proj",
    )(x, sh, sc, g, w)


HIST0 = HIST - (CONV_WIDTH - 1)


def _causal_conv(ext_ref, s, u, w_ref, b_ref, rows, t_valid, new_buf_ref):
    ext_ref[s, pl.ds(HIST, rows), :] = u
    out = b_ref[...]
    for j in range(CONV_WIDTH):
        out = out + ext_ref[s, pl.ds(HIST0 + j, rows), :] * w_ref[pl.ds(j, 1), :]
    last = ext_ref[s, pl.ds(HIST0 + t_valid, CONV_WIDTH - 1), :]
    new_buf_ref[s] = last
    ext_ref[s, pl.ds(HIST0, CONV_WIDTH - 1), :] = last
    return out


def _mixer_specs(nseq, rows):
    chunk = lambda w, col=0: pl.BlockSpec((nseq, rows, w), lambda i, c: (i, c, col))
    full = lambda *s: pl.BlockSpec(s, lambda i, c: (0,) * len(s))
    per_seq = lambda *s: pl.BlockSpec((nseq,) + s, lambda i, c: (i,) + (0,) * len(s))
    return chunk, full, per_seq


def _row_mask(rows, t_valid):
    if t_valid == rows:
        return None
    return lax.broadcasted_iota(jnp.int32, (rows, 1), 0) < t_valid


def _mlstm_kernel(pm_ref, pg_ref, conv0_ref, c0_ref, n0_ref, m0_ref, cw_ref, cb_ref, gb_ref, ng_ref,
                  out_ref, conv_ref, c_ref, n_ref, m_ref, ext_ref, *, rows, t_valid, nseq):
    @pl.when(pl.program_id(1) == 0)
    def _():
        ext_ref[:, pl.ds(HIST0, CONV_WIDTH - 1), :] = conv0_ref[...]
        c_ref[...] = c0_ref[...]
        n_ref[...] = n0_ref[...]
        m_ref[...] = m0_ref[...]

    valid = _row_mask(rows, t_valid)
    tril = _tri(rows)
    tril_b = tril.astype(BF16)
    eye = lax.broadcasted_iota(jnp.int32, (rows, rows), 0) == lax.broadcasted_iota(jnp.int32, (rows, rows), 1)
    gb = gb_ref[...]

    seqs = []
    for s in range(nseq):
        pm = pm_ref[s]
        qk = _silu(_causal_conv(ext_ref, s, pm[:, :MLSTM_CONV_W], cw_ref, cb_ref, rows, t_valid, conv_ref))
        gates = pg_ref[s]
        ig = gates[:, :128] + gb[:, :128]
        fg = gates[:, 128:256] + gb[:, 128:256]
        logf = -_softplus(-fg)
        if valid is not None:
            ig = jnp.where(valid, ig, NEG_BIG)
            logf = jnp.where(valid, logf, 0.0)
        bcum = _cumsum_rows(logf, tril_b)
        m_prev = m_ref[s]
        b_end = bcum[rows - 1:rows, :]
        ws = b_end - bcum + ig
        m_new = jnp.maximum(b_end + m_prev, jnp.max(ws, axis=0, keepdims=True))
        m_ref[s] = m_new
        seqs.append(dict(q=qk[:, :MLSTM_W], k=qk[:, MLSTM_W:] * HEAD_DIM ** -0.5,
                         v=pm[:, MLSTM_CONV_W:MLSTM_CONV_W + MLSTM_W], o=pm[:, MLSTM_CONV_W + MLSTM_W:],
                         ig=ig, bcum=bcum, inter=bcum + m_prev, ws=jnp.exp(ws - m_new),
                         a_end=jnp.exp(b_end + m_prev - m_new)))

    chains = [(s, h) for s in range(nseq) for h in range(MLSTM_HEADS)]
    lanes = lambda c: slice(c[1] * HEAD_DIM, (c[1] + 1) * HEAD_DIM)
    col = lambda name, c: seqs[c[0]][name][:, c[1]:c[1] + 1]
    q = [seqs[c[0]]['q'][:, lanes(c)] for c in chains]
    k = [seqs[c[0]]['k'][:, lanes(c)] for c in chains]
    v = [seqs[c[0]]['v'][:, lanes(c)] for c in chains]
    c0 = [c_ref[c[0], c[1]] for c in chains]
    n0 = [n_ref[c[0], pl.ds(c[1], 1), :] for c in chains]
    qk_t = [_dot_nt(q_, k_) for q_, k_ in zip(q, k)]
    q_c = [_dot_nt(q_, c_) for q_, c_ in zip(q, c0)]
    c_add = [_dot_tn(col('ws', c) * v_, k_) for c, v_, k_ in zip(chains, v, k)]
    dmat = [jnp.where(tril, col('bcum', c) - _to_row(col('bcum', c), eye) + _to_row(col('ig', c), eye), NEG_BIG)
            for c in chains]
    m_t = [jnp.maximum(col('inter', c), jnp.max(d_, axis=1, keepdims=True)) for c, d_ in zip(chains, dmat)]
    sc = [g_ * jnp.exp(d_ - m_) for g_, d_, m_ in zip(qk_t, dmat, m_t)]
    s_v = [_dot(s_, v_) for s_, v_ in zip(sc, v)]
    for i, c in enumerate(chains):
        a_in = jnp.exp(col('inter', c) - m_t[i])
        num = s_v[i] + a_in * q_c[i]
        den = jnp.sum(sc[i], axis=1, keepdims=True) + a_in * jnp.sum(q[i] * n0[i], axis=1, keepdims=True)
        hh = num / jnp.maximum(jnp.abs(den), jnp.exp(-m_t[i]))
        ae = col('a_end', c)
        c_ref[c[0], c[1]] = ae * c0[i] + c_add[i]
        n_ref[c[0], pl.ds(c[1], 1), :] = ae * n0[i] + jnp.sum(col('ws', c) * k[i], axis=0, keepdims=True)
        hc = hh - jnp.mean(hh, axis=1, keepdims=True)
        hn = hc * lax.rsqrt(jnp.mean(hc * hc, axis=1, keepdims=True) + NORM_EPS)
        out_ref[c[0], :, lanes(c)] = hn * ng_ref[:, lanes(c)] * _sigmoid(seqs[c[0]]['o'][:, lanes(c)])


def _mlstm(pm, pg, conv0, c0, n0, m0, cw, cb, gb, ng, *, rows, t_valid, nseq):
    b, t, _ = pm.shape
    chunk, full, per_seq = _mixer_specs(nseq, rows)
    state_specs = [per_seq(CONV_WIDTH - 1, MLSTM_CONV_W), per_seq(MLSTM_HEADS, HEAD_DIM, HEAD_DIM),
                   per_seq(MLSTM_HEADS, HEAD_DIM), per_seq(1, 128)]
    return pl.pallas_call(
        functools.partial(_mlstm_kernel, rows=rows, t_valid=t_valid, nseq=nseq),
        grid=(b // nseq, t // rows),
        in_specs=[chunk(MLSTM_MAIN), chunk(256)] + state_specs
                 + [full(CONV_WIDTH, MLSTM_CONV_W), full(1, MLSTM_CONV_W), full(1, GATE_W), full(1, MLSTM_W)],
        out_specs=[chunk(MLSTM_W)] + state_specs,
        out_shape=[jax.ShapeDtypeStruct((b, t, MLSTM_W), F32),
                   jax.ShapeDtypeStruct((b, CONV_WIDTH - 1, MLSTM_CONV_W), F32),
                   jax.ShapeDtypeStruct((b, MLSTM_HEADS, HEAD_DIM, HEAD_DIM), F32),
                   jax.ShapeDtypeStruct((b, MLSTM_HEADS, HEAD_DIM), F32),
                   jax.ShapeDtypeStruct((b, 1, 128), F32)],
        scratch_shapes=[pltpu.VMEM((nseq, HIST + rows, MLSTM_CONV_W), F32)],
        compiler_params=_params(1, 1),
        name="mlstm_mix",
    )(pm, pg, conv0, c0, n0, m0, cw, cb, gb, ng)


def _ssd_kernel(ps_ref, dt_ref, conv0_ref, h0_ref, cw_ref, cb_ref, gb_ref, alog_ref, dskip_ref, ng_ref,
                out_ref, conv_ref, h_ref, ext_ref, y_ref, *, rows, t_valid, nseq):
    @pl.when(pl.program_id(1) == 0)
    def _():
        ext_ref[:, pl.ds(HIST0, CONV_WIDTH - 1), :] = conv0_ref[...]
        h_ref[...] = h0_ref[...]

    valid = _row_mask(rows, t_valid)
    tril = _tri(rows)
    tril_b = tril.astype(BF16)
    eye = lax.broadcasted_iota(jnp.int32, (rows, rows), 0) == lax.broadcasted_iota(jnp.int32, (rows, rows), 1)
    dskip = dskip_ref[...]
    neg_a = -jnp.exp(alog_ref[...])
    gs = SSD_GROUPS * SSD_STATE

    seqs = []
    for s in range(nseq):
        ps = ps_ref[s]
        xbc = _silu(_causal_conv(ext_ref, s, ps[:, SSD_W:], cw_ref, cb_ref, rows, t_valid, conv_ref))
        dt = _softplus(dt_ref[s] + gb_ref[:, 256:384])
        if valid is not None:
            dt = jnp.where(valid, dt, 0.0)
        seqs.append(dict(z=ps[:, :SSD_W], xs=xbc[:, :SSD_W], bm=xbc[:, SSD_W:SSD_W + gs], cm=xbc[:, SSD_W + gs:],
                         dt=dt, acum=_cumsum_rows(dt * neg_a, tril_b)))

    groups = [(s, g) for s in range(nseq) for g in range(SSD_GROUPS)]
    grp = lambda name, sg: seqs[sg[0]][name][:, sg[1] * SSD_STATE:(sg[1] + 1) * SSD_STATE]
    cb = {sg: _dot_nt(grp('cm', sg), grp('bm', sg)) for sg in groups}
    chains = [(s, hd) for s in range(nseq) for hd in range(SSD_HEADS)]
    lanes = lambda c: slice(c[1] * HEAD_DIM, (c[1] + 1) * HEAD_DIM)
    col = lambda name, c: seqs[c[0]][name][:, c[1]:c[1] + 1]
    group_of = lambda c: (c[0], c[1] // SSD_HEADS_PER_GROUP)
    x = [seqs[c[0]]['xs'][:, lanes(c)] for c in chains]
    h0 = [h_ref[c[0], c[1]] for c in chains]
    from_state = [_dot_nt(grp('cm', group_of(c)), h_) for c, h_ in zip(chains, h0)]
    h_add = [_dot_tn(jnp.exp(col('acum', c)[rows - 1:rows, :] - col('acum', c)) * col('dt', c) * x_,
                     grp('bm', group_of(c))) for c, x_ in zip(chains, x)]
    wmat = [jnp.exp(jnp.where(tril, col('acum', c) - _to_row(col('acum', c), eye), NEG_BIG))
            * cb[group_of(c)] * _to_row(col('dt', c), eye) for c in chains]
    y = [_dot(w_, x_) for w_, x_ in zip(wmat, x)]
    for i, c in enumerate(chains):
        a_col = col('acum', c)
        h_ref[c[0], c[1]] = jnp.exp(a_col[rows - 1:rows, :]) * h0[i] + h_add[i]
        y_ref[c[0], :, lanes(c)] = y[i] + jnp.exp(a_col) * from_state[i] + dskip[:, c[1]:c[1] + 1] * x[i]
    for s in range(nseq):
        out_ref[s] = _rms(y_ref[s] * _silu(seqs[s]['z']), ng_ref[...])


def _ssd(ps, pg, conv0, h0, cw, cb, gb, alog, dskip, ng, *, rows, t_valid, nseq):
    b, t, _ = ps.shape
    chunk, full, per_seq = _mixer_specs(nseq, rows)
    state_specs = [per_seq(CONV_WIDTH - 1, SSD_CONV_W), per_seq(SSD_HEADS, HEAD_DIM, SSD_STATE)]
    return pl.pallas_call(
        functools.partial(_ssd_kernel, rows=rows, t_valid=t_valid, nseq=nseq),
        grid=(b // nseq, t // rows),
        in_specs=[chunk(SSD_MAIN), chunk(128, col=2)] + state_specs
                 + [full(CONV_WIDTH, SSD_CONV_W), full(1, SSD_CONV_W), full(1, GATE_W),
                    full(1, 128), full(1, 128), full(1, SSD_W)],
        out_specs=[chunk(SSD_W)] + state_specs,
        out_shape=[jax.ShapeDtypeStruct((b, t, SSD_W), F32),
                   jax.ShapeDtypeStruct((b, CONV_WIDTH - 1, SSD_CONV_W), F32),
                   jax.ShapeDtypeStruct((b, SSD_HEADS, HEAD_DIM, SSD_STATE), F32)],
        scratch_shapes=[pltpu.VMEM((nseq, HIST + rows, SSD_CONV_W), F32), pltpu.VMEM((nseq, rows, SSD_W), F32)],
        compiler_params=_params(1, 1),
        name="ssd_mix",
    )(ps, pg, conv0, h0, cw, cb, gb, alog, dskip, ng)


def _rwkv_kernel(pr_ref, shift0_ref, s0_ref, mu_ref, w0_ref, wup_ref, a0_ref, aup_ref, gup_ref, kk_ref, ka_ref,
                 rk_ref, lng_ref, lnb_ref, bd_ref,
                 out_ref, shift_ref, s_ref, ext_ref, y_ref, *, rows, t_valid, nseq):
    @pl.when(pl.program_id(1) == 0)
    def _():
        ext_ref[:, pl.ds(HIST - 1, 1), :] = shift0_ref[...]
        s_ref[...] = s0_ref[...]

    tril = _tri(rows)
    stril = _tri(rows, strict=True)
    bd = bd_ref[...]
    head_sum = lambda t: _mm(t, bd, 'nn', b_exact=True)
    w3 = 3 * RWKV_W
    n = nseq * rows

    p3 = pr_ref[...]
    ext_ref[:, pl.ds(HIST, rows), :] = p3
    prev = ext_ref[:, pl.ds(HIST - 1, rows), :].reshape(n, RWKV_PROJ)
    last = ext_ref[:, pl.ds(HIST - 1 + t_valid, 1), :]
    shift_ref[...] = last
    ext_ref[:, pl.ds(HIST - 1, 1), :] = last
    p = p3.reshape(n, RWKV_PROJ)
    x = p + (prev - p) * mu_ref[...]
    r = x[:, :RWKV_W]
    k = x[:, RWKV_W:2 * RWKV_W]
    v = x[:, 2 * RWKV_W:w3]
    xw = x[:, w3:w3 + RWKV_DECAY_RANK]
    xa = x[:, w3 + RWKV_DECAY_RANK:w3 + RWKV_DECAY_RANK + RWKV_ICLR_RANK]
    xg = x[:, w3 + RWKV_DECAY_RANK + RWKV_ICLR_RANK:]
    w_log = -_softplus(-(w0_ref[...] + _dot(jnp.tanh(xw).astype(BF16), wup_ref[...]))) - 0.5
    logw = -jnp.exp(w_log)
    a = _sigmoid(a0_ref[...] + _dot(xa.astype(BF16), aup_ref[...]))
    g = _dot(_sigmoid(xg).astype(BF16), gup_ref[...])
    kk = k * kk_ref[...]
    kk = kk / jnp.maximum(jnp.sqrt(head_sum(kk * kk)), 1e-12)
    k = k * (1.0 + (a - 1.0) * ka_ref[...])
    ri = lax.broadcasted_iota(jnp.int32, (n, n), 0)
    ci = lax.broadcasted_iota(jnp.int32, (n, n), 1)
    same_seq = (ri // rows) == (ci // rows)
    if t_valid != rows:
        valid = (lax.broadcasted_iota(jnp.int32, (n, 1), 0) % rows) < t_valid
        logw = jnp.where(valid, logw, 0.0)
        kk = jnp.where(valid, kk, 0.0)
        k = jnp.where(valid, k, 0.0)

    cum = _cumsum_rows(logw, (same_seq & (ri >= ci)).astype(BF16))
    to_end = jnp.exp(_cumsum_rows(logw, (same_seq & (ri < ci)).astype(BF16)))
    p_inv = jnp.exp(-cum)
    a_t = -kk * jnp.exp(cum - logw)
    kka = kk * a
    b_t = kka * p_inv
    k_t = k * p_inv
    r_t = r * jnp.exp(cum)
    b_e = kka * to_end
    k_e = k * to_end
    p_end = jnp.exp(cum)

    chains = [(s, h) for s in range(nseq) for h in range(RWKV_HEADS)]
    blk = lambda t, c: t[c[0] * rows:(c[0] + 1) * rows, c[1] * HEAD_DIM:(c[1] + 1) * HEAD_DIM]
    s0 = [s_ref[c[0], c[1]] for c in chains]
    vh = [blk(v, c) for c in chains]
    ar = [jnp.concatenate([blk(a_t, c), blk(r_t, c)], axis=0) for c in chains]
    bk = [jnp.concatenate([blk(b_t, c), blk(k_t, c)], axis=0) for c in chains]
    gram = [_mm(x_, y_, 'nt') for x_, y_ in zip(ar, bk)]
    from_state = [_mm(x_, y_, 'nt') for x_, y_ in zip(ar, s0)]
    n_mat = [jnp.where(stril, g_[:rows, :rows], 0.0) for g_ in gram]
    u = [f_[:rows] + _mm(jnp.where(stril, g_[:rows, rows:], 0.0), v_, 'nn')
         for f_, g_, v_ in zip(from_state, gram, vh)]
    span = 1
    while span < rows:
        u = [u_ + _mm(n_, u_, 'nn') for u_, n_ in zip(u, n_mat)]
        span *= 2
        if span < rows:
            n_mat = [_mm(n_, n_, 'nn') for n_ in n_mat]
    y_u = [_mm(jnp.where(tril, g_[rows:, :rows], 0.0), u_, 'nn') for g_, u_ in zip(gram, u)]
    y_v = [_mm(jnp.where(tril, g_[rows:, rows:], 0.0), v_, 'nn') for g_, v_ in zip(gram, vh)]
    s_add = [_mm(jnp.concatenate([u_, v_], axis=0), jnp.concatenate([blk(b_e, c), blk(k_e, c)], axis=0), 'tn')
             for u_, v_, c in zip(u, vh, chains)]
    for i, c in enumerate(chains):
        lanes = slice(c[1] * HEAD_DIM, (c[1] + 1) * HEAD_DIM)
        y_ref[c[0], :, lanes] = from_state[i][rows:] + y_u[i] + y_v[i]
        end_row = (c[0] + 1) * rows - 1
        s_ref[c[0], c[1]] = p_end[end_row:end_row + 1, lanes] * s0[i] + s_add[i]

    y = y_ref[...].reshape(n, RWKV_W)
    inv_n = 1.0 / HEAD_DIM
    yc = y - head_sum(y) * inv_n
    yn = yc * lax.rsqrt(head_sum(yc * yc) * inv_n + RWKV_LN_EPS)
    bonus = head_sum(r * k * rk_ref[...]) * v
    out_ref[...] = ((yn * lng_ref[...] + lnb_ref[...] + bonus) * g).reshape(nseq, rows, RWKV_W)


def _rwkv(pr, shift0, s0, mu, w0, wup, a0, aup, gup, kkp, kap, rk, lng, lnb, bd, *, rows, t_valid, nseq):
    b, t, _ = pr.shape
    chunk, full, per_seq = _mixer_specs(nseq, rows)
    state_specs = [per_seq(1, RWKV_PROJ), per_seq(RWKV_HEADS, HEAD_DIM, HEAD_DIM)]
    return pl.pallas_call(
        functools.partial(_rwkv_kernel, rows=rows, t_valid=t_valid, nseq=nseq),
        grid=(b // nseq, t // rows),
        in_specs=[chunk(RWKV_PROJ)] + state_specs
                 + [full(1, RWKV_PROJ), full(1, RWKV_W), full(RWKV_DECAY_RANK, RWKV_W), full(1, RWKV_W),
                    full(RWKV_ICLR_RANK, RWKV_W), full(RWKV_GATE_RANK, RWKV_W), full(1, RWKV_W), full(1, RWKV_W),
                    full(1, RWKV_W), full(1, RWKV_W), full(1, RWKV_W), full(RWKV_W, RWKV_W)],
        out_specs=[chunk(RWKV_W)] + state_specs,
        out_shape=[jax.ShapeDtypeStruct((b, t, RWKV_W), F32),
                   jax.ShapeDtypeStruct((b, 1, RWKV_PROJ), F32),
                   jax.ShapeDtypeStruct((b, RWKV_HEADS, HEAD_DIM, HEAD_DIM), F32)],
        scratch_shapes=[pltpu.VMEM((nseq, HIST + rows, RWKV_PROJ), F32), pltpu.VMEM((nseq, rows, RWKV_W), F32)],
        compiler_params=_params(1, 1),
        name="rwkv_mix",
    )(pr, shift0, s0, mu, w0, wup, a0, aup, gup, kkp, kap, rk, lng, lnb, bd)


def _outproj_kernel(or_ref, om_ref, os_ref, x_ref, g1_ref, sh_ref, sc_ref, ng_ref, w_ref, xmid_ref, h2_ref,
                    *, per_token):
    pick = (lambda r: r[...]) if per_token else (lambda r: r[0])
    w = w_ref[...]
    mix = (_dot(or_ref[...].astype(BF16), w[:RWKV_W])
           + _dot(om_ref[...].astype(BF16), w[RWKV_W:RWKV_W + MLSTM_W])
           + _dot(os_ref[...].astype(BF16), w[RWKV_W + MLSTM_W:]))
    x = x_ref[...] + pick(g1_ref) * mix
    xmid_ref[...] = x
    h2_ref[...] = (_rms(x, ng_ref[...]) * (1.0 + pick(sc_ref)) + pick(sh_ref)).astype(BF16)


def _outproj(o_r, o_m, o_s, x, g1, sh, sc, ng, w, *, tm, rows_per_seq, per_token):
    n = x.shape[0]
    tok = lambda w_: pl.BlockSpec((tm, w_), lambda i: (i, 0))
    ms = _mod_spec(tm, rows_per_seq, per_token)
    return pl.pallas_call(
        functools.partial(_outproj_kernel, per_token=per_token),
        grid=(n // tm,),
        in_specs=[tok(RWKV_W), tok(MLSTM_W), tok(SSD_W), tok(D_MODEL), ms, ms, ms,
                  pl.BlockSpec((1, D_MODEL), lambda i: (0, 0)),
                  pl.BlockSpec((D_MODEL, D_MODEL), lambda i: (0, 0))],
        out_specs=[tok(D_MODEL), tok(D_MODEL)],
        out_shape=[jax.ShapeDtypeStruct((n, D_MODEL), F32), jax.ShapeDtypeStruct((n, D_MODEL), BF16)],
        compiler_params=_params(1),
        name="out_proj",
    )(o_r, o_m, o_s, x, g1, sh, sc, ng, w)


CAND_LEN = tuple(PEER_TOPK // (p + 1) for p in range(PEER_TOPK))
CAND_OFF = tuple(sum(CAND_LEN[:p]) for p in range(PEER_TOPK))
CAND_ROWS = -(-sum(CAND_LEN) // 8) * 8


def _top1(x, iota):
    m = jnp.max(x, axis=0, keepdims=True)
    idx = jnp.min(jnp.where(x == m, iota, float(x.shape[0])), axis=0, keepdims=True)
    return m, idx


def _route_kernel(h2_ref, wq_ref, sk_ref, w3_ref, s_ref, cand_ref, eid_ref, v_ref, i_ref, top_ref, et_ref, gt_ref,
                  a_ref, b_ref, g_ref, tmp_ref, *, tg):
    hd = pl.program_id(1)
    q = _dot(h2_ref[...], wq_ref[...]).astype(BF16)
    key_iota = lax.broadcasted_iota(jnp.int32, (PEER_KEYS, tg), 0).astype(F32)
    for c in range(2):
        s_ref[c] = _dot_nt(sk_ref[c], q[:, c * PEER_HALF:(c + 1) * PEER_HALF])

    def pick_key(p, carry):
        for c in range(2):
            s = s_ref[c]
            m, idx = _top1(s, key_iota)
            s_ref[c] = jnp.where(key_iota == idx, -jnp.inf, s)
            v_ref[c, pl.ds(p, 1), :] = m
            i_ref[c, pl.ds(p, 1), :] = idx
        return carry

    lax.fori_loop(0, PEER_TOPK, pick_key, 0)

    v1, v2 = v_ref[0], v_ref[1]
    i1, i2 = i_ref[0], i_ref[1]
    cand_ref[...] = jnp.full(cand_ref.shape, -jnp.inf, F32)
    eid_ref[...] = jnp.zeros(eid_ref.shape, F32)
    for p in range(PEER_TOPK):
        cand_ref[pl.ds(CAND_OFF[p], CAND_LEN[p]), :] = v1[p:p + 1, :] + v2[:CAND_LEN[p], :]
        eid_ref[pl.ds(CAND_OFF[p], CAND_LEN[p]), :] = i1[p:p + 1, :] * float(PEER_KEYS) + i2[:CAND_LEN[p], :]
    cand_iota = lax.broadcasted_iota(jnp.int32, (CAND_ROWS, tg), 0).astype(F32)
    row0 = pl.multiple_of(hd * PEER_TOPK, PEER_TOPK)

    def pick_cand(k, carry):
        cand = cand_ref[...]
        m, pos = _top1(cand, cand_iota)
        hit = cand_iota == pos
        cand_ref[...] = jnp.where(hit, -jnp.inf, cand)
        top_ref[pl.ds(k, 1), :] = m
        et_ref[pl.ds(row0 + k, 1), :] = jnp.sum(jnp.where(hit, eid_ref[...], 0.0), axis=0, keepdims=True)
        return carry

    lax.fori_loop(0, PEER_TOPK, pick_cand, 0)
    top = top_ref[...]
    e = jnp.exp(top - top[0:1, :])
    gt_ref[pl.ds(row0, PEER_TOPK), :] = e / jnp.sum(e, axis=0, keepdims=True)

    @pl.when(hd == PEER_HEADS - 1)
    def _():
        eid = et_ref[...]
        key1 = jnp.floor(eid * (1.0 / PEER_KEYS))
        a_ref[...] = key1.T
        b_ref[...] = (eid - key1 * float(PEER_KEYS)).T
        g_ref[...] = gt_ref[...].T
        j_iota = lax.broadcasted_iota(jnp.int32, (PEER_KEYS, PEER_HEADS * PEER_TOPK), 0).astype(F32)

        def token_groups(gi, carry):
            for half in range(2):
                t0 = pl.multiple_of((gi * 2 + half) * TOKEN_GROUP, TOKEN_GROUP)
                base = half * TOKEN_GROUP * TOKEN_PITCH
                for tt in range(TOKEN_GROUP):
                    t = t0 + tt
                    sel1 = jnp.where(j_iota == a_ref[pl.ds(t, 1), :], 1.0, 0.0).astype(BF16)
                    sel2 = jnp.where(j_iota == b_ref[pl.ds(t, 1), :], g_ref[pl.ds(t, 1), :], 0.0).astype(BF16)
                    tmp_ref[pl.ds(base + tt * TOKEN_PITCH, PEER_KEYS), :] = _dot_nt(sel1, sel2)
            for half in range(2):
                t0 = pl.multiple_of((gi * 2 + half) * TOKEN_GROUP, TOKEN_GROUP)
                base = half * TOKEN_GROUP * TOKEN_PITCH
                for j in range(PEER_KEYS):
                    rows = tmp_ref[pl.ds(base + j, TOKEN_GROUP, stride=TOKEN_PITCH), :]
                    w3_ref[j, pl.ds(t0, TOKEN_GROUP), :] = rows.astype(BF16)
            return carry

        lax.fori_loop(0, tg // (2 * TOKEN_GROUP), token_groups, 0)


def _route(h2, wq, sk, *, tg):
    n = h2.shape[0]
    npick = PEER_HEADS * PEER_TOPK
    return pl.pallas_call(
        functools.partial(_route_kernel, tg=tg),
        grid=(n // tg, PEER_HEADS),
        in_specs=[pl.BlockSpec((tg, D_MODEL), lambda i, h: (i, 0)),
                  pl.BlockSpec((D_MODEL, PEER_QDIM), lambda i, h: (0, h)),
                  pl.BlockSpec((2, PEER_KEYS, PEER_HALF), lambda i, h: (0, 0, 0))],
        out_specs=pl.BlockSpec((PEER_KEYS, tg, PEER_KEYS), lambda i, h: (0, i, 0)),
        out_shape=jax.ShapeDtypeStruct((PEER_KEYS, n, PEER_KEYS), BF16),
        scratch_shapes=[pltpu.VMEM((2, PEER_KEYS, tg), F32),
                        pltpu.VMEM((CAND_ROWS, tg), F32), pltpu.VMEM((CAND_ROWS, tg), F32),
                        pltpu.VMEM((2, PEER_TOPK, tg), F32), pltpu.VMEM((2, PEER_TOPK, tg), F32),
                        pltpu.VMEM((PEER_TOPK, tg), F32),
                        pltpu.VMEM((npick, tg), F32), pltpu.VMEM((npick, tg), F32),
                        pltpu.VMEM((tg, npick), F32), pltpu.VMEM((tg, npick), F32), pltpu.VMEM((tg, npick), F32),
                        pltpu.VMEM((2 * TOKEN_GROUP * TOKEN_PITCH, PEER_KEYS), F32)],
        compiler_params=_params(1, 1),
        name="peer_route",
    )(h2, wq, sk)


def _gelu_tanh(x):
    return x * (0.5 * (1.0 + jnp.tanh(math.sqrt(2.0 / math.pi) * (x + 0.044715 * (x * x * x)))))


def _experts_kernel(h2_ref, w_ref, u_ref, v_ref, xmid_ref, g2_ref, fg_ref, o_ref, acc_ref, *, per_token, final_norm):
    j = pl.program_id(1)

    @pl.when(j == 0)
    def _():
        acc_ref[...] = jnp.zeros_like(acc_ref)

    s = _dot_nt(h2_ref[...], u_ref[...])
    act = jnp.concatenate(
        [(_gelu_tanh(s[:, r * PEER_KEYS:(r + 1) * PEER_KEYS]) * w_ref[r].astype(F32)).astype(BF16)
         for r in range(w_ref.shape[0])], axis=1)
    acc_ref[...] += _dot(act, v_ref[...])

    @pl.when(j == pl.num_programs(1) - 1)
    def _():
        g2 = g2_ref[...] if per_token else g2_ref[0]
        x = xmid_ref[...] + g2 * acc_ref[...]
        o_ref[...] = _rms(x, fg_ref[...]) if final_norm else x


def _experts(h2, w, u, v, xmid, g2, fg, *, tm, eb, rows_per_seq, per_token, final_norm):
    n = h2.shape[0]
    if per_token:
        ms = pl.BlockSpec((tm, D_MODEL), lambda i, j: (i, 0))
    else:
        ms = pl.BlockSpec((1, 1, D_MODEL), lambda i, j: (i * tm // rows_per_seq, 0, 0))
    return pl.pallas_call(
        functools.partial(_experts_kernel, per_token=per_token, final_norm=final_norm),
        grid=(n // tm, PEER_EXPERTS // eb),
        in_specs=[pl.BlockSpec((tm, D_MODEL), lambda i, j: (i, 0)),
                  pl.BlockSpec((eb // PEER_KEYS, tm, PEER_KEYS), lambda i, j: (j, i, 0)),
                  pl.BlockSpec((eb, D_MODEL), lambda i, j: (j, 0)),
                  pl.BlockSpec((eb, D_MODEL), lambda i, j: (j, 0)),
                  pl.BlockSpec((tm, D_MODEL), lambda i, j: (i, 0)),
                  ms,
                  pl.BlockSpec((1, D_MODEL), lambda i, j: (0, 0))],
        out_specs=pl.BlockSpec((tm, D_MODEL), lambda i, j: (i, 0)),
        out_shape=jax.ShapeDtypeStruct((n, D_MODEL), F32),
        scratch_shapes=[pltpu.VMEM((tm, D_MODEL), F32)],
        compiler_params=_params(1, 1),
        name="peer_experts",
    )(h2, w, u, v, xmid, g2, fg)


TOKEN_TILE = 512
ROUTE_TILE = 256
EXPERT_BLOCK = 2048
SAMPLE_ROWS = 8
PROMPT_SEQS_PER_STEP = 2
SAMPLE_SEQS_PER_STEP = 8


def _lane_pad(vec, offset=0, width=128):
    return jnp.zeros((1, width), F32).at[0, offset:offset + vec.shape[0]].set(vec)


def _layer_weights(l, w):
    w_in = w['w_in'][l]
    o_m = RWKV_PROJ
    o_mg = o_m + MLSTM_MAIN
    o_s = o_mg + 2 * MLSTM_HEADS
    o_sg = o_s + SSD_MAIN
    gate_cols = jnp.zeros((D_MODEL, GATE_W), F32)
    gate_cols = gate_cols.at[:, 0:MLSTM_HEADS].set(w_in[:, o_mg:o_mg + MLSTM_HEADS])
    gate_cols = gate_cols.at[:, 128:128 + MLSTM_HEADS].set(w_in[:, o_mg + MLSTM_HEADS:o_s])
    gate_cols = gate_cols.at[:, 256:256 + SSD_HEADS].set(w_in[:, o_sg:o_sg + SSD_HEADS])
    w_cat = jnp.concatenate([w_in[:, :o_m], w_in[:, o_m:o_mg], w_in[:, o_s:o_sg], gate_cols], axis=1).astype(BF16)
    gate_bias = jnp.concatenate([_lane_pad(w['mlstm_i_b'][l]), _lane_pad(w['mlstm_f_b'][l]),
                                 _lane_pad(w['ssd_dt_bias'][l])], axis=1)
    row = lambda name: w[name][l].reshape(1, -1)
    return dict(
        w_cat=w_cat, gate_bias=gate_bias, w_out=w['w_out'][l].astype(BF16),
        norm1_g=row('norm1_g'), norm2_g=row('norm2_g'),
        rwkv=(row('rwkv_mu'), row('rwkv_w0'), w['rwkv_w_up'][l].astype(BF16), row('rwkv_a0'),
              w['rwkv_a_up'][l].astype(BF16), w['rwkv_g_up'][l].astype(BF16), row('rwkv_k_k'), row('rwkv_k_a'),
              row('rwkv_r_k'), row('rwkv_ln_g'), row('rwkv_ln_b')),
        mlstm=(w['mlstm_conv_w'][l], row('mlstm_conv_b'), gate_bias, row('mlstm_norm_g')),
        ssd=(w['ssd_conv_w'][l], row('ssd_conv_b'), gate_bias, _lane_pad(w['ssd_A_log'][l]),
             _lane_pad(w['ssd_D'][l]), row('ssd_norm_g')),
        wq=w['peer_wq'][l].astype(BF16), sk=w['peer_subkeys'][l].astype(BF16),
        u=w['peer_u'][l].astype(BF16), v=w['peer_v'][l].astype(BF16),
    )


def _run_layer(x, mods, state, lw, head_sum, final_g, *, batch, seq, rows, nseq, per_token, final_norm):
    sh1, sc1, g1, sh2, sc2, g2 = mods
    shift, wkv, mconv, m_c, m_n, m_m, sconv, sst = state
    tm = min(TOKEN_TILE, batch * seq)
    tok = dict(tm=tm, rows_per_seq=seq, per_token=per_token)
    p_r, p_m, p_s, p_g = _inproj(x, sh1, sc1, lw['norm1_g'], lw['w_cat'], **tok)

    t_pad = -(-seq // rows) * rows
    t_valid = seq if seq < rows else rows

    def seqs(a):
        a = a.reshape(batch, seq, a.shape[-1])
        return a if t_pad == seq else jnp.pad(a, ((0, 0), (0, t_pad - seq), (0, 0)))

    def toks(a):
        return a[:, :seq].reshape(batch * seq, a.shape[-1])

    mix = dict(rows=rows, t_valid=t_valid, nseq=nseq)
    o_r, shift, wkv = _rwkv(seqs(p_r), shift[:, None, :], wkv, *lw['rwkv'], head_sum, **mix)
    m_m = jnp.pad(m_m, ((0, 0), (0, 128 - MLSTM_HEADS)))[:, None, :]
    o_m, mconv, m_c, m_n, m_m = _mlstm(seqs(p_m), seqs(p_g), mconv, m_c, m_n, m_m, *lw['mlstm'], **mix)
    o_s, sconv, sst = _ssd(seqs(p_s), seqs(p_g), sconv, sst, *lw['ssd'], **mix)
    new_state = (shift[:, 0, :], wkv, mconv, m_c, m_n, m_m[:, 0, :MLSTM_HEADS], sconv, sst)

    x_mid, h2 = _outproj(toks(o_r), toks(o_m), toks(o_s), x, g1, sh2, sc2, lw['norm2_g'], lw['w_out'], **tok)
    gate_mass = _route(h2, lw['wq'], lw['sk'], tg=min(ROUTE_TILE, batch * seq))
    x_new = _experts(h2, gate_mass, lw['u'], lw['v'], x_mid, g2, final_g, eb=EXPERT_BLOCK,
                     final_norm=final_norm, **tok)
    return x_new, new_state


def kernel(x_prompt, x_sample, state_rwkv_shift, state_rwkv_wkv, state_mlstm_conv, state_mlstm_C, state_mlstm_n, state_mlstm_m, state_ssd_conv, state_ssd, c_prompt, c_sample, ada_w, ada_b, norm1_g, norm2_g, w_in, w_out, rwkv_mu, rwkv_w0, rwkv_w_up, rwkv_a0, rwkv_a_up, rwkv_g_up, rwkv_k_k, rwkv_k_a, rwkv_r_k, rwkv_ln_g, rwkv_ln_b, mlstm_conv_w, mlstm_conv_b, mlstm_i_b, mlstm_f_b, mlstm_norm_g, ssd_conv_w, ssd_conv_b, ssd_dt_bias, ssd_A_log, ssd_D, ssd_norm_g, peer_wq, peer_subkeys, peer_u, peer_v, final_g):
    weights = dict(norm1_g=norm1_g, norm2_g=norm2_g, w_in=w_in, w_out=w_out, rwkv_mu=rwkv_mu, rwkv_w0=rwkv_w0,
                   rwkv_w_up=rwkv_w_up, rwkv_a0=rwkv_a0, rwkv_a_up=rwkv_a_up, rwkv_g_up=rwkv_g_up,
                   rwkv_k_k=rwkv_k_k, rwkv_k_a=rwkv_k_a, rwkv_r_k=rwkv_r_k, rwkv_ln_g=rwkv_ln_g,
                   rwkv_ln_b=rwkv_ln_b, mlstm_conv_w=mlstm_conv_w, mlstm_conv_b=mlstm_conv_b,
                   mlstm_i_b=mlstm_i_b, mlstm_f_b=mlstm_f_b, mlstm_norm_g=mlstm_norm_g, ssd_conv_w=ssd_conv_w,
                   ssd_conv_b=ssd_conv_b, ssd_dt_bias=ssd_dt_bias, ssd_A_log=ssd_A_log, ssd_D=ssd_D,
                   ssd_norm_g=ssd_norm_g, peer_wq=peer_wq, peer_subkeys=peer_subkeys, peer_u=peer_u, peer_v=peer_v)
    nb, seq, d = x_prompt.shape
    nd, dseq, _ = x_sample.shape
    cache = (state_rwkv_shift, state_rwkv_wkv, state_mlstm_conv, state_mlstm_C, state_mlstm_n, state_mlstm_m,
             state_ssd_conv, state_ssd)

    mod = _ada_mod(jnp.concatenate([c_prompt, c_sample], axis=0), ada_w, ada_b)
    head_id = jnp.arange(RWKV_W) // HEAD_DIM
    head_sum = (head_id[:, None] == head_id[None, :]).astype(F32)
    fg = final_g.reshape(1, d)

    xp = x_prompt.reshape(nb * seq, d)
    xs = x_sample.reshape(nd * dseq, d)
    new_p, new_s = [], []
    for l in range(DEPTH):
        lw = _layer_weights(l, weights)
        last = l == DEPTH - 1
        mod_p = mod[l, :nb].reshape(nb, 6, 1, d)
        mods_p = tuple(mod_p[:, i] for i in range(6))
        mod_s = jnp.repeat(mod[l, nb:].reshape(nd, 6, d), dseq, axis=0)
        mods_s = tuple(mod_s[:, i] for i in range(6))
        zeros = tuple(jnp.zeros((nb,) + s.shape[2:], F32) for s in cache)
        xp, sp = _run_layer(xp, mods_p, zeros, lw, head_sum, fg, batch=nb, seq=seq, rows=CHUNK,
                            nseq=PROMPT_SEQS_PER_STEP, per_token=False, final_norm=last)
        xs, ss = _run_layer(xs, mods_s, tuple(s[l] for s in cache), lw, head_sum, fg, batch=nd, seq=dseq,
                            rows=SAMPLE_ROWS, nseq=SAMPLE_SEQS_PER_STEP, per_token=True, final_norm=last)
        new_p.append(sp)
        new_s.append(ss)

    def stk(states, i):
        return jnp.stack([st[i] for st in states])

    return ((xp.reshape(nb, seq, d), xs.reshape(nd, dseq, d))
            + tuple(stk(new_p, i) for i in range(8)) + tuple(stk(new_s, i) for i in range(8)))
```

```python
import functools
import math

import jax
import jax.numpy as jnp
from jax import lax
from jax.experimental import pallas as pl
from jax.experimental.pallas import tpu as pltpu

F32 = jnp.float32
BF16 = jnp.bfloat16

D_MODEL = 1024
DEPTH = 4
HEAD_DIM = 64
RWKV_W = 256
RWKV_HEADS = 4
RWKV_DECAY_RANK = 64
RWKV_ICLR_RANK = 64
RWKV_GATE_RANK = 128
RWKV_LN_EPS = 1e-5 * HEAD_DIM
RWKV_PROJ = 3 * RWKV_W + RWKV_DECAY_RANK + RWKV_ICLR_RANK + RWKV_GATE_RANK
MLSTM_W = 256
MLSTM_HEADS = 4
MLSTM_CONV_W = 2 * MLSTM_W
MLSTM_MAIN = MLSTM_CONV_W + 2 * MLSTM_W
SSD_W = 512
SSD_HEADS = 8
SSD_STATE = 128
SSD_GROUPS = 2
SSD_HEADS_PER_GROUP = SSD_HEADS // SSD_GROUPS
SSD_CONV_W = SSD_W + 2 * SSD_GROUPS * SSD_STATE
SSD_MAIN = SSD_W + SSD_CONV_W
CONV_WIDTH = 4
CHUNK = 64
GATE_W = 3 * 128
PEER_KEYS = 128
PEER_EXPERTS = PEER_KEYS * PEER_KEYS
PEER_HEADS = 8
PEER_TOPK = 16
PEER_QDIM = 256
PEER_HALF = 128
NORM_EPS = 1e-6
NEG_BIG = -1e30
TOKEN_GROUP = 16
TOKEN_PITCH = PEER_KEYS + 8

VMEM_LIMIT_BYTES = 56 * 1024 * 1024
HIST = 8


def _dot(a, b, prec=None):
    return jnp.dot(a, b, preferred_element_type=F32, precision=prec)


def _dot_nt(a, b, prec=None):
    return lax.dot_general(a, b, (((1,), (1,)), ((), ())), preferred_element_type=F32, precision=prec)


def _dot_tn(a, b, prec=None):
    return lax.dot_general(a, b, (((0,), (0,)), ((), ())), preferred_element_type=F32, precision=prec)


def _hi_lo(x):
    hi = x.astype(BF16)
    return hi, (x - hi.astype(F32)).astype(BF16)


def _mm(a, b, form, b_exact=False):
    ah, al = _hi_lo(a)
    a_axis = 0 if form == 'tn' else 1
    b_axis = 1 if form == 'nt' else 0
    if b_exact:
        bb = b.astype(BF16)
        a3 = jnp.concatenate([ah, al], axis=a_axis)
        b3 = jnp.concatenate([bb, bb], axis=b_axis)
    else:
        bh, bl = _hi_lo(b)
        a3 = jnp.concatenate([ah, ah, al], axis=a_axis)
        b3 = jnp.concatenate([bh, bl, bh], axis=b_axis)
    return {'nn': _dot, 'nt': _dot_nt, 'tn': _dot_tn}[form](a3, b3)


def _cumsum_rows(x, tril_bf16):
    hi = x.astype(BF16)
    r1 = x - hi.astype(F32)
    mid = r1.astype(BF16)
    lo = (r1 - mid.astype(F32)).astype(BF16)
    return _dot(jnp.concatenate([tril_bf16] * 3, axis=1), jnp.concatenate([hi, mid, lo], axis=0))


def _sigmoid(x):
    return 1.0 / (1.0 + jnp.exp(-x))


def _silu(x):
    return x * _sigmoid(x)


def _softplus(x):
    return jnp.maximum(x, 0.0) + jnp.log(1.0 + jnp.exp(-jnp.abs(x)))


def _tri(n, strict=False):
    r = lax.broadcasted_iota(jnp.int32, (n, n), 0)
    c = lax.broadcasted_iota(jnp.int32, (n, n), 1)
    return (r > c) if strict else (r >= c)


def _to_row(col, eye):
    return jnp.sum(jnp.where(eye, col, 0.0), axis=0, keepdims=True)


def _params(n_parallel, n_arbitrary=0):
    sem = ("parallel",) * n_parallel + ("arbitrary",) * n_arbitrary
    return pltpu.CompilerParams(dimension_semantics=sem, vmem_limit_bytes=VMEM_LIMIT_BYTES)


def _ada_kernel(c_ref, w_ref, b_ref, o_ref):
    c = _silu(c_ref[...]).astype(BF16)
    o_ref[0] = _dot(c, w_ref[0].astype(BF16)) + b_ref[0]


def _ada_mod(c_all, ada_w, ada_b):
    nb = c_all.shape[0]
    tn = 1536
    return pl.pallas_call(
        _ada_kernel,
        grid=(DEPTH, 6 * D_MODEL // tn),
        in_specs=[pl.BlockSpec((nb, D_MODEL), lambda l, j: (0, 0)),
                  pl.BlockSpec((1, D_MODEL, tn), lambda l, j: (l, 0, j)),
                  pl.BlockSpec((1, 1, tn), lambda l, j: (l, 0, j))],
        out_specs=pl.BlockSpec((1, nb, tn), lambda l, j: (l, 0, j)),
        out_shape=jax.ShapeDtypeStruct((DEPTH, nb, 6 * D_MODEL), F32),
        compiler_params=_params(2),
        name="ada_mod",
    )(c_all, ada_w, ada_b.reshape(DEPTH, 1, 6 * D_MODEL))


def _rms(x, g):
    return x * lax.rsqrt(jnp.mean(x * x, axis=-1, keepdims=True) + NORM_EPS) * g


def _inproj_kernel(x_ref, sh_ref, sc_ref, g_ref, w_ref, pr_ref, pm_ref, ps_ref, pg_ref, *, per_token):
    sh = sh_ref[...] if per_token else sh_ref[0]
    sc = sc_ref[...] if per_token else sc_ref[0]
    h = _rms(x_ref[...], g_ref[...]) * (1.0 + sc) + sh
    p = _dot(h.astype(BF16), w_ref[...])
    o = 0
    for ref, w in ((pr_ref, RWKV_PROJ), (pm_ref, MLSTM_MAIN), (ps_ref, SSD_MAIN), (pg_ref, GATE_W)):
        ref[...] = p[:, o:o + w]
        o += w


def _mod_spec(tm, rows_per_seq, per_token):
    if per_token:
        return pl.BlockSpec((tm, D_MODEL), lambda i: (i, 0))
    return pl.BlockSpec((1, 1, D_MODEL), lambda i: (i * tm // rows_per_seq, 0, 0))


def _inproj(x, sh, sc, g, w, *, tm, rows_per_seq, per_token):
    n = x.shape[0]
    widths = (RWKV_PROJ, MLSTM_MAIN, SSD_MAIN, GATE_W)
    wtot = sum(widths)
    return pl.pallas_call(
        functools.partial(_inproj_kernel, per_token=per_token),
        grid=(n // tm,),
        in_specs=[pl.BlockSpec((tm, D_MODEL), lambda i: (i, 0)),
                  _mod_spec(tm, rows_per_seq, per_token), _mod_spec(tm, rows_per_seq, per_token),
                  pl.BlockSpec((1, D_MODEL), lambda i: (0, 0)),
                  pl.BlockSpec((D_MODEL, wtot), lambda i: (0, 0))],
        out_specs=[pl.BlockSpec((tm, w_), lambda i: (i, 0)) for w_ in widths],
        out_shape=[jax.ShapeDtypeStruct((n, w_), F32) for w_ in widths],
        compiler_params=_params(1),
        name="in_proj",
    )(x, sh, sc, g, w)


HIST0 = HIST - (CONV_WIDTH - 1)


def _causal_conv(ext_ref, s, u, w_ref, b_ref, rows, t_valid, new_buf_ref):
    ext_ref[s, pl.ds(HIST, rows), :] = u
    out = b_ref[...]
    for j in range(CONV_WIDTH):
        out = out + ext_ref[s, pl.ds(HIST0 + j, rows), :] * w_ref[pl.ds(j, 1), :]
    last = ext_ref[s, pl.ds(HIST0 + t_valid, CONV_WIDTH - 1), :]
    new_buf_ref[s] = last
    ext_ref[s, pl.ds(HIST0, CONV_WIDTH - 1), :] = last
    return out


def _mixer_specs(nseq, rows):
    chunk = lambda w, col=0: pl.BlockSpec((nseq, rows, w), lambda i, c: (i, c, col))
    full = lambda *s: pl.BlockSpec(s, lambda i, c: (0,) * len(s))
    per_seq = lambda *s: pl.BlockSpec((nseq,) + s, lambda i, c: (i,) + (0,) * len(s))
    return chunk, full, per_seq


def _row_mask(rows, t_valid):
    if t_valid == rows:
        return None
    return lax.broadcasted_iota(jnp.int32, (rows, 1), 0) < t_valid


def _mlstm_kernel(pm_ref, pg_ref, conv0_ref, c0_ref, n0_ref, m0_ref, cw_ref, cb_ref, gb_ref, ng_ref,
                  out_ref, conv_ref, c_ref, n_ref, m_ref, ext_ref, *, rows, t_valid, nseq):
    @pl.when(pl.program_id(1) == 0)
    def _():
        ext_ref[:, pl.ds(HIST0, CONV_WIDTH - 1), :] = conv0_ref[...]
        c_ref[...] = c0_ref[...]
        n_ref[...] = n0_ref[...]
        m_ref[...] = m0_ref[...]

    valid = _row_mask(rows, t_valid)
    tril = _tri(rows)
    tril_b = tril.astype(BF16)
    eye = lax.broadcasted_iota(jnp.int32, (rows, rows), 0) == lax.broadcasted_iota(jnp.int32, (rows, rows), 1)
    gb = gb_ref[...]

    seqs = []
    for s in range(nseq):
        pm = pm_ref[s]
        qk = _silu(_causal_conv(ext_ref, s, pm[:, :MLSTM_CONV_W], cw_ref, cb_ref, rows, t_valid, conv_ref))
        gates = pg_ref[s]
        ig = gates[:, :128] + gb[:, :128]
        fg = gates[:, 128:256] + gb[:, 128:256]
        logf = -_softplus(-fg)
        if valid is not None:
            ig = jnp.where(valid, ig, NEG_BIG)
            logf = jnp.where(valid, logf, 0.0)
        bcum = _cumsum_rows(logf, tril_b)
        m_prev = m_ref[s]
        b_end = bcum[rows - 1:rows, :]
        ws = b_end - bcum + ig
        m_new = jnp.maximum(b_end + m_prev, jnp.max(ws, axis=0, keepdims=True))
        m_ref[s] = m_new
        seqs.append(dict(q=qk[:, :MLSTM_W], k=qk[:, MLSTM_W:] * HEAD_DIM ** -0.5,
                         v=pm[:, MLSTM_CONV_W:MLSTM_CONV_W + MLSTM_W], o=pm[:, MLSTM_CONV_W + MLSTM_W:],
                         ig=ig, bcum=bcum, inter=bcum + m_prev, ws=jnp.exp(ws - m_new),
                         a_end=jnp.exp(b_end + m_prev - m_new)))

    chains = [(s, h) for s in range(nseq) for h in range(MLSTM_HEADS)]
    lanes = lambda c: slice(c[1] * HEAD_DIM, (c[1] + 1) * HEAD_DIM)
    col = lambda name, c: seqs[c[0]][name][:, c[1]:c[1] + 1]
    q = [seqs[c[0]]['q'][:, lanes(c)] for c in chains]
    k = [seqs[c[0]]['k'][:, lanes(c)] for c in chains]
    v = [seqs[c[0]]['v'][:, lanes(c)] for c in chains]
    c0 = [c_ref[c[0], c[1]] for c in chains]
    n0 = [n_ref[c[0], pl.ds(c[1], 1), :] for c in chains]
    qk_t = [_dot_nt(q_, k_) for q_, k_ in zip(q, k)]
    q_c = [_dot_nt(q_, c_) for q_, c_ in zip(q, c0)]
    c_add = [_dot_tn(col('ws', c) * v_, k_) for c, v_, k_ in zip(chains, v, k)]
    dmat = [jnp.where(tril, col('bcum', c) - _to_row(col('bcum', c), eye) + _to_row(col('ig', c), eye), NEG_BIG)
            for c in chains]
    m_t = [jnp.maximum(col('inter', c), jnp.max(d_, axis=1, keepdims=True)) for c, d_ in zip(chains, dmat)]
    sc = [g_ * jnp.exp(d_ - m_) for g_, d_, m_ in zip(qk_t, dmat, m_t)]
    s_v = [_dot(s_, v_) for s_, v_ in zip(sc, v)]
    for i, c in enumerate(chains):
        a_in = jnp.exp(col('inter', c) - m_t[i])
        num = s_v[i] + a_in * q_c[i]
        den = jnp.sum(sc[i], axis=1, keepdims=True) + a_in * jnp.sum(q[i] * n0[i], axis=1, keepdims=True)
        hh = num / jnp.maximum(jnp.abs(den), jnp.exp(-m_t[i]))
        ae = col('a_end', c)
        c_ref[c[0], c[1]] = ae * c0[i] + c_add[i]
        n_ref[c[0], pl.ds(c[1], 1), :] = ae * n0[i] + jnp.sum(col('ws', c) * k[i], axis=0, keepdims=True)
        hc = hh - jnp.mean(hh, axis=1, keepdims=True)
        hn = hc * lax.rsqrt(jnp.mean(hc * hc, axis=1, keepdims=True) + NORM_EPS)
        out_ref[c[0], :, lanes(c)] = hn * ng_ref[:, lanes(c)] * _sigmoid(seqs[c[0]]['o'][:, lanes(c)])


def _mlstm(pm, pg, conv0, c0, n0, m0, cw, cb, gb, ng, *, rows, t_valid, nseq):
    b, t, _ = pm.shape
    chunk, full, per_seq = _mixer_specs(nseq, rows)
    state_specs = [per_seq(CONV_WIDTH - 1, MLSTM_CONV_W), per_seq(MLSTM_HEADS, HEAD_DIM, HEAD_DIM),
                   per_seq(MLSTM_HEADS, HEAD_DIM), per_seq(1, 128)]
    return pl.pallas_call(
        functools.partial(_mlstm_kernel, rows=rows, t_valid=t_valid, nseq=nseq),
        grid=(b // nseq, t // rows),
        in_specs=[chunk(MLSTM_MAIN), chunk(256)] + state_specs
                 + [full(CONV_WIDTH, MLSTM_CONV_W), full(1, MLSTM_CONV_W), full(1, GATE_W), full(1, MLSTM_W)],
        out_specs=[chunk(MLSTM_W)] + state_specs,
        out_shape=[jax.ShapeDtypeStruct((b, t, MLSTM_W), F32),
                   jax.ShapeDtypeStruct((b, CONV_WIDTH - 1, MLSTM_CONV_W), F32),
                   jax.ShapeDtypeStruct((b, MLSTM_HEADS, HEAD_DIM, HEAD_DIM), F32),
                   jax.ShapeDtypeStruct((b, MLSTM_HEADS, HEAD_DIM), F32),
                   jax.ShapeDtypeStruct((b, 1, 128), F32)],
        scratch_shapes=[pltpu.VMEM((nseq, HIST + rows, MLSTM_CONV_W), F32)],
        compiler_params=_params(1, 1),
        name="mlstm_mix",
    )(pm, pg, conv0, c0, n0, m0, cw, cb, gb, ng)


def _ssd_kernel(ps_ref, dt_ref, conv0_ref, h0_ref, cw_ref, cb_ref, gb_ref, alog_ref, dskip_ref, ng_ref,
                out_ref, conv_ref, h_ref, ext_ref, y_ref, *, rows, t_valid, nseq):
    @pl.when(pl.program_id(1) == 0)
    def _():
        ext_ref[:, pl.ds(HIST0, CONV_WIDTH - 1), :] = conv0_ref[...]
        h_ref[...] = h0_ref[...]

    valid = _row_mask(rows, t_valid)
    tril = _tri(rows)
    tril_b = tril.astype(BF16)
    eye = lax.broadcasted_iota(jnp.int32, (rows, rows), 0) == lax.broadcasted_iota(jnp.int32, (rows, rows), 1)
    dskip = dskip_ref[...]
    neg_a = -jnp.exp(alog_ref[...])
    gs = SSD_GROUPS * SSD_STATE

    seqs = []
    for s in range(nseq):
        ps = ps_ref[s]
        xbc = _silu(_causal_conv(ext_ref, s, ps[:, SSD_W:], cw_ref, cb_ref, rows, t_valid, conv_ref))
        dt = _softplus(dt_ref[s] + gb_ref[:, 256:384])
        if valid is not None:
            dt = jnp.where(valid, dt, 0.0)
        seqs.append(dict(z=ps[:, :SSD_W], xs=xbc[:, :SSD_W], bm=xbc[:, SSD_W:SSD_W + gs], cm=xbc[:, SSD_W + gs:],
                         dt=dt, acum=_cumsum_rows(dt * neg_a, tril_b)))

    groups = [(s, g) for s in range(nseq) for g in range(SSD_GROUPS)]
    grp = lambda name, sg: seqs[sg[0]][name][:, sg[1] * SSD_STATE:(sg[1] + 1) * SSD_STATE]
    cb = {sg: _dot_nt(grp('cm', sg), grp('bm', sg)) for sg in groups}
    chains = [(s, hd) for s in range(nseq) for hd in range(SSD_HEADS)]
    lanes = lambda c: slice(c[1] * HEAD_DIM, (c[1] + 1) * HEAD_DIM)
    col = lambda name, c: seqs[c[0]][name][:, c[1]:c[1] + 1]
    group_of = lambda c: (c[0], c[1] // SSD_HEADS_PER_GROUP)
    x = [seqs[c[0]]['xs'][:, lanes(c)] for c in chains]
    h0 = [h_ref[c[0], c[1]] for c in chains]
    from_state = [_dot_nt(grp('cm', group_of(c)), h_) for c, h_ in zip(chains, h0)]
    h_add = [_dot_tn(jnp.exp(col('acum', c)[rows - 1:rows, :] - col('acum', c)) * col('dt', c) * x_,
                     grp('bm', group_of(c))) for c, x_ in zip(chains, x)]
    wmat = [jnp.exp(jnp.where(tril, col('acum', c) - _to_row(col('acum', c), eye), NEG_BIG))
            * cb[group_of(c)] * _to_row(col('dt', c), eye) for c in chains]
    y = [_dot(w_, x_) for w_, x_ in zip(wmat, x)]
    for i, c in enumerate(chains):
        a_col = col('acum', c)
        h_ref[c[0], c[1]] = jnp.exp(a_col[rows - 1:rows, :]) * h0[i] + h_add[i]
        y_ref[c[0], :, lanes(c)] = y[i] + jnp.exp(a_col) * from_state[i] + dskip[:, c[1]:c[1] + 1] * x[i]
    for s in range(nseq):
        out_ref[s] = _rms(y_ref[s] * _silu(seqs[s]['z']), ng_ref[...])


def _ssd(ps, pg, conv0, h0, cw, cb, gb, alog, dskip, ng, *, rows, t_valid, nseq):
    b, t, _ = ps.shape
    chunk, full, per_seq = _mixer_specs(nseq, rows)
    state_specs = [per_seq(CONV_WIDTH - 1, SSD_CONV_W), per_seq(SSD_HEADS, HEAD_DIM, SSD_STATE)]
    return pl.pallas_call(
        functools.partial(_ssd_kernel, rows=rows, t_valid=t_valid, nseq=nseq),
        grid=(b // nseq, t // rows),
        in_specs=[chunk(SSD_MAIN), chunk(128, col=2)] + state_specs
                 + [full(CONV_WIDTH, SSD_CONV_W), full(1, SSD_CONV_W), full(1, GATE_W),
                    full(1, 128), full(1, 128), full(1, SSD_W)],
        out_specs=[chunk(SSD_W)] + state_specs,
        out_shape=[jax.ShapeDtypeStruct((b, t, SSD_W), F32),
                   jax.ShapeDtypeStruct((b, CONV_WIDTH - 1, SSD_CONV_W), F32),
                   jax.ShapeDtypeStruct((b, SSD_HEADS, HEAD_DIM, SSD_STATE), F32)],
        scratch_shapes=[pltpu.VMEM((nseq, HIST + rows, SSD_CONV_W), F32), pltpu.VMEM((nseq, rows, SSD_W), F32)],
        compiler_params=_params(1, 1),
        name="ssd_mix",
    )(ps, pg, conv0, h0, cw, cb, gb, alog, dskip, ng)


def _rwkv_kernel(pr_ref, shift0_ref, s0_ref, mu_ref, w0_ref, wup_ref, a0_ref, aup_ref, gup_ref, kk_ref, ka_ref,
                 rk_ref, lng_ref, lnb_ref, bd_ref,
                 out_ref, shift_ref, s_ref, ext_ref, y_ref, *, rows, t_valid, nseq):
    @pl.when(pl.program_id(1) == 0)
    def _():
        ext_ref[:, pl.ds(HIST - 1, 1), :] = shift0_ref[...]
        s_ref[...] = s0_ref[...]

    tril = _tri(rows)
    stril = _tri(rows, strict=True)
    bd = bd_ref[...]
    head_sum = lambda t: _mm(t, bd, 'nn', b_exact=True)
    w3 = 3 * RWKV_W
    n = nseq * rows

    p3 = pr_ref[...]
    ext_ref[:, pl.ds(HIST, rows), :] = p3
    prev = ext_ref[:, pl.ds(HIST - 1, rows), :].reshape(n, RWKV_PROJ)
    last = ext_ref[:, pl.ds(HIST - 1 + t_valid, 1), :]
    shift_ref[...] = last
    ext_ref[:, pl.ds(HIST - 1, 1), :] = last
    p = p3.reshape(n, RWKV_PROJ)
    x = p + (prev - p) * mu_ref[...]
    r = x[:, :RWKV_W]
    k = x[:, RWKV_W:2 * RWKV_W]
    v = x[:, 2 * RWKV_W:w3]
    xw = x[:, w3:w3 + RWKV_DECAY_RANK]
    xa = x[:, w3 + RWKV_DECAY_RANK:w3 + RWKV_DECAY_RANK + RWKV_ICLR_RANK]
    xg = x[:, w3 + RWKV_DECAY_RANK + RWKV_ICLR_RANK:]
    w_log = -_softplus(-(w0_ref[...] + _dot(jnp.tanh(xw).astype(BF16), wup_ref[...]))) - 0.5
    logw = -jnp.exp(w_log)
    a = _sigmoid(a0_ref[...] + _dot(xa.astype(BF16), aup_ref[...]))
    g = _dot(_sigmoid(xg).astype(BF16), gup_ref[...])
    kk = k * kk_ref[...]
    kk = kk / jnp.maximum(jnp.sqrt(head_sum(kk * kk)), 1e-12)
    k = k * (1.0 + (a - 1.0) * ka_ref[...])
    ri = lax.broadcasted_iota(jnp.int32, (n, n), 0)
    ci = lax.broadcasted_iota(jnp.int32, (n, n), 1)
    same_seq = (ri // rows) == (ci // rows)
    if t_valid != rows:
        valid = (lax.broadcasted_iota(jnp.int32, (n, 1), 0) % rows) < t_valid
        logw = jnp.where(valid, logw, 0.0)
        kk = jnp.where(valid, kk, 0.0)
        k = jnp.where(valid, k, 0.0)

    cum = _cumsum_rows(logw, (same_seq & (ri >= ci)).astype(BF16))
    to_end = jnp.exp(_cumsum_rows(logw, (same_seq & (ri < ci)).astype(BF16)))
    p_inv = jnp.exp(-cum)
    a_t = -kk * jnp.exp(cum - logw)
    kka = kk * a
    b_t = kka * p_inv
    k_t = k * p_inv
    r_t = r * jnp.exp(cum)
    b_e = kka * to_end
    k_e = k * to_end
    p_end = jnp.exp(cum)

    chains = [(s, h) for s in range(nseq) for h in range(RWKV_HEADS)]
    blk = lambda t, c: t[c[0] * rows:(c[0] + 1) * rows, c[1] * HEAD_DIM:(c[1] + 1) * HEAD_DIM]
    s0 = [s_ref[c[0], c[1]] for c in chains]
    vh = [blk(v, c) for c in chains]
    ar = [jnp.concatenate([blk(a_t, c), blk(r_t, c)], axis=0) for c in chains]
    bk = [jnp.concatenate([blk(b_t, c), blk(k_t, c)], axis=0) for c in chains]
    gram = [_mm(x_, y_, 'nt') for x_, y_ in zip(ar, bk)]
    from_state = [_mm(x_, y_, 'nt') for x_, y_ in zip(ar, s0)]
    n_mat = [jnp.where(stril, g_[:rows, :rows], 0.0) for g_ in gram]
    u = [f_[:rows] + _mm(jnp.where(stril, g_[:rows, rows:], 0.0), v_, 'nn')
         for f_, g_, v_ in zip(from_state, gram, vh)]
    span = 1
    while span < rows:
        u = [u_ + _mm(n_, u_, 'nn') for u_, n_ in zip(u, n_mat)]
        span *= 2
        if span < rows:
            n_mat = [_mm(n_, n_, 'nn') for n_ in n_mat]
    y_u = [_mm(jnp.where(tril, g_[rows:, :rows], 0.0), u_, 'nn') for g_, u_ in zip(gram, u)]
    y_v = [_mm(jnp.where(tril, g_[rows:, rows:], 0.0), v_, 'nn') for g_, v_ in zip(gram, vh)]
    s_add = [_mm(jnp.concatenate([u_, v_], axis=0), jnp.concatenate([blk(b_e, c), blk(k_e, c)], axis=0), 'tn')
             for u_, v_, c in zip(u, vh, chains)]
    for i, c in enumerate(chains):
        lanes = slice(c[1] * HEAD_DIM, (c[1] + 1) * HEAD_DIM)
        y_ref[c[0], :, lanes] = from_state[i][rows:] + y_u[i] + y_v[i]
        end_row = (c[0] + 1) * rows - 1
        s_ref[c[0], c[1]] = p_end[end_row:end_row + 1, lanes] * s0[i] + s_add[i]

    y = y_ref[...].reshape(n, RWKV_W)
    inv_n = 1.0 / HEAD_DIM
    yc = y - head_sum(y) * inv_n
    yn = yc * lax.rsqrt(head_sum(yc * yc) * inv_n + RWKV_LN_EPS)
    bonus = head_sum(r * k * rk_ref[...]) * v
    out_ref[...] = ((yn * lng_ref[...] + lnb_ref[...] + bonus) * g).reshape(nseq, rows, RWKV_W)


def _rwkv(pr, shift0, s0, mu, w0, wup, a0, aup, gup, kkp, kap, rk, lng, lnb, bd, *, rows, t_valid, nseq):
    b, t, _ = pr.shape
    chunk, full, per_seq = _mixer_specs(nseq, rows)
    state_specs = [per_seq(1, RWKV_PROJ), per_seq(RWKV_HEADS, HEAD_DIM, HEAD_DIM)]
    return pl.pallas_call(
        functools.partial(_rwkv_kernel, rows=rows, t_valid=t_valid, nseq=nseq),
        grid=(b // nseq, t // rows),
        in_specs=[chunk(RWKV_PROJ)] + state_specs
                 + [full(1, RWKV_PROJ), full(1, RWKV_W), full(RWKV_DECAY_RANK, RWKV_W), full(1, RWKV_W),
                    full(RWKV_ICLR_RANK, RWKV_W), full(RWKV_GATE_RANK, RWKV_W), full(1, RWKV_W), full(1, RWKV_W),
                    full(1, RWKV_W), full(1, RWKV_W), full(1, RWKV_W), full(RWKV_W, RWKV_W)],
        out_specs=[chunk(RWKV_W)] + state_specs,
        out_shape=[jax.ShapeDtypeStruct((b, t, RWKV_W), F32),
                   jax.ShapeDtypeStruct((b, 1, RWKV_PROJ), F32),
                   jax.ShapeDtypeStruct((b, RWKV_HEADS, HEAD_DIM, HEAD_DIM), F32)],
        scratch_shapes=[pltpu.VMEM((nseq, HIST + rows, RWKV_PROJ), F32), pltpu.VMEM((nseq, rows, RWKV_W), F32)],
        compiler_params=_params(1, 1),
        name="rwkv_mix",
    )(pr, shift0, s0, mu, w0, wup, a0, aup, gup, kkp, kap, rk, lng, lnb, bd)


def _outproj_kernel(or_ref, om_ref, os_ref, x_ref, g1_ref, sh_ref, sc_ref, ng_ref, w_ref, xmid_ref, h2_ref,
                    *, per_token):
    pick = (lambda r: r[...]) if per_token else (lambda r: r[0])
    w = w_ref[...]
    mix = (_dot(or_ref[...].astype(BF16), w[:RWKV_W])
           + _dot(om_ref[...].astype(BF16), w[RWKV_W:RWKV_W + MLSTM_W])
           + _dot(os_ref[...].astype(BF16), w[RWKV_W + MLSTM_W:]))
    x = x_ref[...] + pick(g1_ref) * mix
    xmid_ref[...] = x
    h2_ref[...] = (_rms(x, ng_ref[...]) * (1.0 + pick(sc_ref)) + pick(sh_ref)).astype(BF16)


def _outproj(o_r, o_m, o_s, x, g1, sh, sc, ng, w, *, tm, rows_per_seq, per_token):
    n = x.shape[0]
    tok = lambda w_: pl.BlockSpec((tm, w_), lambda i: (i, 0))
    ms = _mod_spec(tm, rows_per_seq, per_token)
    return pl.pallas_call(
        functools.partial(_outproj_kernel, per_token=per_token),
        grid=(n // tm,),
        in_specs=[tok(RWKV_W), tok(MLSTM_W), tok(SSD_W), tok(D_MODEL), ms, ms, ms,
                  pl.BlockSpec((1, D_MODEL), lambda i: (0, 0)),
                  pl.BlockSpec((D_MODEL, D_MODEL), lambda i: (0, 0))],
        out_specs=[tok(D_MODEL), tok(D_MODEL)],
        out_shape=[jax.ShapeDtypeStruct((n, D_MODEL), F32), jax.ShapeDtypeStruct((n, D_MODEL), BF16)],
        compiler_params=_params(1),
        name="out_proj",
    )(o_r, o_m, o_s, x, g1, sh, sc, ng, w)


CAND_LEN = tuple(PEER_TOPK // (p + 1) for p in range(PEER_TOPK))
CAND_OFF = tuple(sum(CAND_LEN[:p]) for p in range(PEER_TOPK))
CAND_ROWS = -(-sum(CAND_LEN) // 8) * 8


def _top1(x, iota):
    m = jnp.max(x, axis=0, keepdims=True)
    idx = jnp.min(jnp.where(x == m, iota, float(x.shape[0])), axis=0, keepdims=True)
    return m, idx


def _gelu_tanh(x):
    return x * (0.5 * (1.0 + jnp.tanh(math.sqrt(2.0 / math.pi) * (x + 0.044715 * (x * x * x)))))


def _peer_kernel(h2c_ref, h2n_ref, wq_ref, sk_ref, u_ref, v_ref, xmid_ref, g2_ref, fg_ref, o_ref,
                 acc_ref, act_ref, gm_ref, cand_ref, eid_ref, et_ref, gt_ref, a_ref, b_ref, g_ref, tmp_ref,
                 *, tm, per_token, final_norm):
    j = pl.program_id(1)

    @pl.when((pl.program_id(0) == 0) & (j == 0))
    def _():
        acc_ref[...] = jnp.zeros_like(acc_ref)
        act_ref[...] = jnp.zeros_like(act_ref)
        gm_ref[...] = jnp.zeros_like(gm_ref)

    q = _dot(h2n_ref[...], wq_ref[...]).astype(BF16)
    s12 = [_dot_nt(sk_ref[c], q[:, c * PEER_HALF:(c + 1) * PEER_HALF]) for c in range(2)]
    acc_ref[...] += _dot(act_ref[...], v_ref[...])
    s = _dot_nt(h2c_ref[...], u_ref[...])

    rows_per_block = u_ref.shape[0] // PEER_KEYS
    for r in range(rows_per_block):
        lanes = slice(r * PEER_KEYS, (r + 1) * PEER_KEYS)
        act_ref[:, lanes] = (_gelu_tanh(s[:, lanes]) * gm_ref[j * rows_per_block + r].astype(F32)).astype(BF16)

    key_iota = lax.broadcasted_iota(jnp.int32, (PEER_KEYS, tm), 0).astype(F32)
    vals = ([], [])
    idxs = ([], [])
    for _ in range(PEER_TOPK):
        for c in range(2):
            m, idx = _top1(s12[c], key_iota)
            s12[c] = jnp.where(key_iota == idx, -jnp.inf, s12[c])
            vals[c].append(m)
            idxs[c].append(idx)
    v1, v2 = (jnp.concatenate(v_, axis=0) for v_ in vals)
    i1, i2 = (jnp.concatenate(i_, axis=0) for i_ in idxs)

    cand_ref[...] = jnp.full(cand_ref.shape, -jnp.inf, F32)
    eid_ref[...] = jnp.zeros(eid_ref.shape, F32)
    for p in range(PEER_TOPK):
        cand_ref[pl.ds(CAND_OFF[p], CAND_LEN[p]), :] = v1[p:p + 1, :] + v2[:CAND_LEN[p], :]
        eid_ref[pl.ds(CAND_OFF[p], CAND_LEN[p]), :] = i1[p:p + 1, :] * float(PEER_KEYS) + i2[:CAND_LEN[p], :]
    cand_iota = lax.broadcasted_iota(jnp.int32, (CAND_ROWS, tm), 0).astype(F32)
    cand = cand_ref[...]
    eid = eid_ref[...]
    tops, picks = [], []
    for _ in range(PEER_TOPK):
        m, pos = _top1(cand, cand_iota)
        hit = cand_iota == pos
        cand = jnp.where(hit, -jnp.inf, cand)
        tops.append(m)
        picks.append(jnp.sum(jnp.where(hit, eid, 0.0), axis=0, keepdims=True))
    top = jnp.concatenate(tops, axis=0)
    e = jnp.exp(top - top[0:1, :])
    row0 = pl.multiple_of(j * PEER_TOPK, PEER_TOPK)
    et_ref[pl.ds(row0, PEER_TOPK), :] = jnp.concatenate(picks, axis=0)
    gt_ref[pl.ds(row0, PEER_TOPK), :] = e / jnp.sum(e, axis=0, keepdims=True)

    @pl.when(j == 0)
    def _():
        g2 = g2_ref[...] if per_token else g2_ref[0]
        x = xmid_ref[...] + g2 * acc_ref[...]
        o_ref[...] = _rms(x, fg_ref[...]) if final_norm else x
        acc_ref[...] = jnp.zeros_like(acc_ref)

    @pl.when(j == PEER_HEADS - 1)
    def _():
        picked = et_ref[...]
        key1 = jnp.floor(picked * (1.0 / PEER_KEYS))
        a_ref[...] = key1.T
        b_ref[...] = (picked - key1 * float(PEER_KEYS)).T
        g_ref[...] = gt_ref[...].T
        j_iota = lax.broadcasted_iota(jnp.int32, (PEER_KEYS, PEER_HEADS * PEER_TOPK), 0).astype(F32)

        def token_groups(gi, carry):
            for half in range(2):
                t0 = pl.multiple_of((gi * 2 + half) * TOKEN_GROUP, TOKEN_GROUP)
                base = half * TOKEN_GROUP * TOKEN_PITCH
                for tt in range(TOKEN_GROUP):
                    t = t0 + tt
                    sel1 = jnp.where(j_iota == a_ref[pl.ds(t, 1), :], 1.0, 0.0).astype(BF16)
                    sel2 = jnp.where(j_iota == b_ref[pl.ds(t, 1), :], g_ref[pl.ds(t, 1), :], 0.0).astype(BF16)
                    tmp_ref[pl.ds(base + tt * TOKEN_PITCH, PEER_KEYS), :] = _dot_nt(sel1, sel2)
            for half in range(2):
                t0 = pl.multiple_of((gi * 2 + half) * TOKEN_GROUP, TOKEN_GROUP)
                base = half * TOKEN_GROUP * TOKEN_PITCH
                for r in range(PEER_KEYS):
                    rows = tmp_ref[pl.ds(base + r, TOKEN_GROUP, stride=TOKEN_PITCH), :]
                    gm_ref[r, pl.ds(t0, TOKEN_GROUP), :] = rows.astype(BF16)
            return carry

        lax.fori_loop(0, tm // (2 * TOKEN_GROUP), token_groups, 0)


def _peer(h2, wq, sk, u, v, xmid, g2, fg, *, tm, rows_per_seq, per_token, final_norm):
    n = h2.shape[0]
    ntiles = n // tm
    eb = PEER_EXPERTS // PEER_HEADS
    npick = PEER_HEADS * PEER_TOPK
    clamp = lambda t: jnp.clip(t, 0, ntiles - 1)
    routed = lambda g, j: clamp(g)
    stage_a = lambda g, j: clamp(g - 1)
    done = lambda g, j: clamp(g - 2 + jnp.minimum(j, 1))
    if per_token:
        ms = pl.BlockSpec((tm, D_MODEL), lambda g, j: (done(g, j), 0))
    else:
        ms = pl.BlockSpec((1, 1, D_MODEL), lambda g, j: (done(g, j) * tm // rows_per_seq, 0, 0))
    return pl.pallas_call(
        functools.partial(_peer_kernel, tm=tm, per_token=per_token, final_norm=final_norm),
        grid=(ntiles + 2, PEER_HEADS),
        in_specs=[pl.BlockSpec((tm, D_MODEL), lambda g, j: (stage_a(g, j), 0)),
                  pl.BlockSpec((tm, D_MODEL), lambda g, j: (routed(g, j), 0)),
                  pl.BlockSpec((D_MODEL, PEER_QDIM), lambda g, j: (0, j)),
                  pl.BlockSpec((2, PEER_KEYS, PEER_HALF), lambda g, j: (0, 0, 0)),
                  pl.BlockSpec((eb, D_MODEL), lambda g, j: (j, 0)),
                  pl.BlockSpec((eb, D_MODEL), lambda g, j: ((j + PEER_HEADS - 1) % PEER_HEADS, 0)),
                  pl.BlockSpec((tm, D_MODEL), lambda g, j: (done(g, j), 0)),
                  ms,
                  pl.BlockSpec((1, D_MODEL), lambda g, j: (0, 0))],
        out_specs=pl.BlockSpec((tm, D_MODEL), lambda g, j: (done(g, j), 0)),
        out_shape=jax.ShapeDtypeStruct((n, D_MODEL), F32),
        scratch_shapes=[pltpu.VMEM((tm, D_MODEL), F32),
                        pltpu.VMEM((tm, eb), BF16),
                        pltpu.VMEM((PEER_KEYS, tm, PEER_KEYS), BF16),
                        pltpu.VMEM((CAND_ROWS, tm), F32), pltpu.VMEM((CAND_ROWS, tm), F32),
                        pltpu.VMEM((npick, tm), F32), pltpu.VMEM((npick, tm), F32),
                        pltpu.VMEM((tm, npick), F32), pltpu.VMEM((tm, npick), F32), pltpu.VMEM((tm, npick), F32),
                        pltpu.VMEM((2 * TOKEN_GROUP * TOKEN_PITCH, PEER_KEYS), F32)],
        compiler_params=_params(0, 2),
        name="peer",
    )(h2, h2, wq, sk, u, v, xmid, g2, fg)


TOKEN_TILE = 512
PEER_TILE = 256
SAMPLE_ROWS = 8
PROMPT_SEQS_PER_STEP = 2
SAMPLE_SEQS_PER_STEP = 8


def _lane_pad(vec, offset=0, width=128):
    return jnp.zeros((1, width), F32).at[0, offset:offset + vec.shape[0]].set(vec)


def _layer_weights(l, w):
    w_in = w['w_in'][l]
    o_m = RWKV_PROJ
    o_mg = o_m + MLSTM_MAIN
    o_s = o_mg + 2 * MLSTM_HEADS
    o_sg = o_s + SSD_MAIN
    gate_cols = jnp.zeros((D_MODEL, GATE_W), F32)
    gate_cols = gate_cols.at[:, 0:MLSTM_HEADS].set(w_in[:, o_mg:o_mg + MLSTM_HEADS])
    gate_cols = gate_cols.at[:, 128:128 + MLSTM_HEADS].set(w_in[:, o_mg + MLSTM_HEADS:o_s])
    gate_cols = gate_cols.at[:, 256:256 + SSD_HEADS].set(w_in[:, o_sg:o_sg + SSD_HEADS])
    w_cat = jnp.concatenate([w_in[:, :o_m], w_in[:, o_m:o_mg], w_in[:, o_s:o_sg], gate_cols], axis=1).astype(BF16)
    gate_bias = jnp.concatenate([_lane_pad(w['mlstm_i_b'][l]), _lane_pad(w['mlstm_f_b'][l]),
                                 _lane_pad(w['ssd_dt_bias'][l])], axis=1)
    row = lambda name: w[name][l].reshape(1, -1)
    return dict(
        w_cat=w_cat, gate_bias=gate_bias, w_out=w['w_out'][l].astype(BF16),
        norm1_g=row('norm1_g'), norm2_g=row('norm2_g'),
        rwkv=(row('rwkv_mu'), row('rwkv_w0'), w['rwkv_w_up'][l].astype(BF16), row('rwkv_a0'),
              w['rwkv_a_up'][l].astype(BF16), w['rwkv_g_up'][l].astype(BF16), row('rwkv_k_k'), row('rwkv_k_a'),
              row('rwkv_r_k'), row('rwkv_ln_g'), row('rwkv_ln_b')),
        mlstm=(w['mlstm_conv_w'][l], row('mlstm_conv_b'), gate_bias, row('mlstm_norm_g')),
        ssd=(w['ssd_conv_w'][l], row('ssd_conv_b'), gate_bias, _lane_pad(w['ssd_A_log'][l]),
             _lane_pad(w['ssd_D'][l]), row('ssd_norm_g')),
        wq=w['peer_wq'][l].astype(BF16), sk=w['peer_subkeys'][l].astype(BF16),
        u=w['peer_u'][l].astype(BF16), v=w['peer_v'][l].astype(BF16),
    )


def _run_layer(x, mods, state, lw, head_sum, final_g, *, batch, seq, rows, nseq, per_token, final_norm):
    sh1, sc1, g1, sh2, sc2, g2 = mods
    shift, wkv, mconv, m_c, m_n, m_m, sconv, sst = state
    tm = min(TOKEN_TILE, batch * seq)
    tok = dict(tm=tm, rows_per_seq=seq, per_token=per_token)
    p_r, p_m, p_s, p_g = _inproj(x, sh1, sc1, lw['norm1_g'], lw['w_cat'], **tok)

    t_pad = -(-seq // rows) * rows
    t_valid = seq if seq < rows else rows

    def seqs(a):
        a = a.reshape(batch, seq, a.shape[-1])
        return a if t_pad == seq else jnp.pad(a, ((0, 0), (0, t_pad - seq), (0, 0)))

    def toks(a):
        return a[:, :seq].reshape(batch * seq, a.shape[-1])

    mix = dict(rows=rows, t_valid=t_valid, nseq=nseq)
    o_r, shift, wkv = _rwkv(seqs(p_r), shift[:, None, :], wkv, *lw['rwkv'], head_sum, **mix)
    m_m = jnp.pad(m_m, ((0, 0), (0, 128 - MLSTM_HEADS)))[:, None, :]
    o_m, mconv, m_c, m_n, m_m = _mlstm(seqs(p_m), seqs(p_g), mconv, m_c, m_n, m_m, *lw['mlstm'], **mix)
    o_s, sconv, sst = _ssd(seqs(p_s), seqs(p_g), sconv, sst, *lw['ssd'], **mix)
    new_state = (shift[:, 0, :], wkv, mconv, m_c, m_n, m_m[:, 0, :MLSTM_HEADS], sconv, sst)

    x_mid, h2 = _outproj(toks(o_r), toks(o_m), toks(o_s), x, g1, sh2, sc2, lw['norm2_g'], lw['w_out'], **tok)
    x_new = _peer(h2, lw['wq'], lw['sk'], lw['u'], lw['v'], x_mid, g2, final_g, tm=min(PEER_TILE, batch * seq),
                  rows_per_seq=seq, per_token=per_token, final_norm=final_norm)
    return x_new, new_state


def kernel(x_prompt, x_sample, state_rwkv_shift, state_rwkv_wkv, state_mlstm_conv, state_mlstm_C, state_mlstm_n, state_mlstm_m, state_ssd_conv, state_ssd, c_prompt, c_sample, ada_w, ada_b, norm1_g, norm2_g, w_in, w_out, rwkv_mu, rwkv_w0, rwkv_w_up, rwkv_a0, rwkv_a_up, rwkv_g_up, rwkv_k_k, rwkv_k_a, rwkv_r_k, rwkv_ln_g, rwkv_ln_b, mlstm_conv_w, mlstm_conv_b, mlstm_i_b, mlstm_f_b, mlstm_norm_g, ssd_conv_w, ssd_conv_b, ssd_dt_bias, ssd_A_log, ssd_D, ssd_norm_g, peer_wq, peer_subkeys, peer_u, peer_v, final_g):
    weights = dict(norm1_g=norm1_g, norm2_g=norm2_g, w_in=w_in, w_out=w_out, rwkv_mu=rwkv_mu, rwkv_w0=rwkv_w0,
                   rwkv_w_up=rwkv_w_up, rwkv_a0=rwkv_a0, rwkv_a_up=rwkv_a_up, rwkv_g_up=rwkv_g_up,
                   rwkv_k_k=rwkv_k_k, rwkv_k_a=rwkv_k_a, rwkv_r_k=rwkv_r_k, rwkv_ln_g=rwkv_ln_g,
                   rwkv_ln_b=rwkv_ln_b, mlstm_conv_w=mlstm_conv_w, mlstm_conv_b=mlstm_conv_b,
                   mlstm_i_b=mlstm_i_b, mlstm_f_b=mlstm_f_b, mlstm_norm_g=mlstm_norm_g, ssd_conv_w=ssd_conv_w,
                   ssd_conv_b=ssd_conv_b, ssd_dt_bias=ssd_dt_bias, ssd_A_log=ssd_A_log, ssd_D=ssd_D,
                   ssd_norm_g=ssd_norm_g, peer_wq=peer_wq, peer_subkeys=peer_subkeys, peer_u=peer_u, peer_v=peer_v)
    nb, seq, d = x_prompt.shape
    nd, dseq, _ = x_sample.shape
    cache = (state_rwkv_shift, state_rwkv_wkv, state_mlstm_conv, state_mlstm_C, state_mlstm_n, state_mlstm_m,
             state_ssd_conv, state_ssd)

    mod = _ada_mod(jnp.concatenate([c_prompt, c_sample], axis=0), ada_w, ada_b)
    head_id = jnp.arange(RWKV_W) // HEAD_DIM
    head_sum = (head_id[:, None] == head_id[None, :]).astype(F32)
    fg = final_g.reshape(1, d)

    xp = x_prompt.reshape(nb * seq, d)
    xs = x_sample.reshape(nd * dseq, d)
    new_p, new_s = [], []
    for l in range(DEPTH):
        lw = _layer_weights(l, weights)
        last = l == DEPTH - 1
        mod_p = mod[l, :nb].reshape(nb, 6, 1, d)
        mods_p = tuple(mod_p[:, i] for i in range(6))
        mod_s = jnp.repeat(mod[l, nb:].reshape(nd, 6, d), dseq, axis=0)
        mods_s = tuple(mod_s[:, i] for i in range(6))
        zeros = tuple(jnp.zeros((nb,) + s.shape[2:], F32) for s in cache)
        xp, sp = _run_layer(xp, mods_p, zeros, lw, head_sum, fg, batch=nb, seq=seq, rows=CHUNK,
                            nseq=PROMPT_SEQS_PER_STEP, per_token=False, final_norm=last)
        xs, ss = _run_layer(xs, mods_s, tuple(s[l] for s in cache), lw, head_sum, fg, batch=nd, seq=dseq,
                            rows=SAMPLE_ROWS, nseq=SAMPLE_SEQS_PER_STEP, per_token=True, final_norm=last)
        new_p.append(sp)
        new_s.append(ss)

    def stk(states, i):
        return jnp.stack([st[i] for st in states])

    return ((xp.reshape(nb, seq, d), xs.reshape(nd, dseq, d))
            + tuple(stk(new_p, i) for i in range(8)) + tuple(stk(new_s, i) for i in range(8)))
```

```python
import functools
import math

import jax
import jax.numpy as jnp
from jax import lax
from jax.experimental import pallas as pl
from jax.experimental.pallas import tpu as pltpu

F32 = jnp.float32
BF16 = jnp.bfloat16

D_MODEL = 1024
DEPTH = 4
HEAD_DIM = 64
RWKV_W = 256
RWKV_HEADS = 4
RWKV_DECAY_RANK = 64
RWKV_ICLR_RANK = 64
RWKV_GATE_RANK = 128
RWKV_LN_EPS = 1e-5 * HEAD_DIM
RWKV_PROJ = 3 * RWKV_W + RWKV_DECAY_RANK + RWKV_ICLR_RANK + RWKV_GATE_RANK
MLSTM_W = 256
MLSTM_HEADS = 4
MLSTM_CONV_W = 2 * MLSTM_W
MLSTM_MAIN = MLSTM_CONV_W + 2 * MLSTM_W
SSD_W = 512
SSD_HEADS = 8
SSD_STATE = 128
SSD_GROUPS = 2
SSD_HEADS_PER_GROUP = SSD_HEADS // SSD_GROUPS
SSD_CONV_W = SSD_W + 2 * SSD_GROUPS * SSD_STATE
SSD_MAIN = SSD_W + SSD_CONV_W
CONV_WIDTH = 4
CHUNK = 64
GATE_W = 3 * 128
PEER_KEYS = 128
PEER_EXPERTS = PEER_KEYS * PEER_KEYS
PEER_HEADS = 8
PEER_TOPK = 16
PEER_QDIM = 256
PEER_HALF = 128
NORM_EPS = 1e-6
NEG_BIG = -1e30
TOKEN_GROUP = 16
TOKEN_PITCH = PEER_KEYS + 8

VMEM_LIMIT_BYTES = 56 * 1024 * 1024
HIST = 8


def _dot(a, b, prec=None):
    return jnp.dot(a, b, preferred_element_type=F32, precision=prec)


def _dot_nt(a, b, prec=None):
    return lax.dot_general(a, b, (((1,), (1,)), ((), ())), preferred_element_type=F32, precision=prec)


def _dot_tn(a, b, prec=None):
    return lax.dot_general(a, b, (((0,), (0,)), ((), ())), preferred_element_type=F32, precision=prec)


def _hi_lo(x):
    hi = x.astype(BF16)
    return hi, (x - hi.astype(F32)).astype(BF16)


def _mm(a, b, form, b_exact=False, single=False):
    dot = {'nn': _dot, 'nt': _dot_nt, 'tn': _dot_tn}[form]
    if single:
        return dot(a.astype(BF16), b.astype(BF16))
    ah, al = _hi_lo(a)
    a_axis = 0 if form == 'tn' else 1
    b_axis = 1 if form == 'nt' else 0
    if b_exact:
        bb = b.astype(BF16)
        a3 = jnp.concatenate([ah, al], axis=a_axis)
        b3 = jnp.concatenate([bb, bb], axis=b_axis)
    else:
        bh, bl = _hi_lo(b)
        a3 = jnp.concatenate([ah, ah, al], axis=a_axis)
        b3 = jnp.concatenate([bh, bl, bh], axis=b_axis)
    return dot(a3, b3)


def _cumsum_rows(x, tril_bf16):
    hi = x.astype(BF16)
    r1 = x - hi.astype(F32)
    mid = r1.astype(BF16)
    lo = (r1 - mid.astype(F32)).astype(BF16)
    return _dot(jnp.concatenate([tril_bf16] * 3, axis=1), jnp.concatenate([hi, mid, lo], axis=0))


def _sigmoid(x):
    return 1.0 / (1.0 + jnp.exp(-x))


def _silu(x):
    return x * _sigmoid(x)


def _softplus(x):
    return jnp.maximum(x, 0.0) + jnp.log(1.0 + jnp.exp(-jnp.abs(x)))


def _tri(n, strict=False):
    r = lax.broadcasted_iota(jnp.int32, (n, n), 0)
    c = lax.broadcasted_iota(jnp.int32, (n, n), 1)
    return (r > c) if strict else (r >= c)


def _to_row(col, eye):
    return jnp.sum(jnp.where(eye, col, 0.0), axis=0, keepdims=True)


def _params(n_parallel, n_arbitrary=0):
    sem = ("parallel",) * n_parallel + ("arbitrary",) * n_arbitrary
    return pltpu.CompilerParams(dimension_semantics=sem, vmem_limit_bytes=VMEM_LIMIT_BYTES)


def _ada_kernel(c_ref, w_ref, b_ref, o_ref):
    c = _silu(c_ref[...]).astype(BF16)
    o_ref[0] = _dot(c, w_ref[0].astype(BF16)) + b_ref[0]


def _ada_mod(c_all, ada_w, ada_b):
    nb = c_all.shape[0]
    tn = 1536
    return pl.pallas_call(
        _ada_kernel,
        grid=(DEPTH, 6 * D_MODEL // tn),
        in_specs=[pl.BlockSpec((nb, D_MODEL), lambda l, j: (0, 0)),
                  pl.BlockSpec((1, D_MODEL, tn), lambda l, j: (l, 0, j)),
                  pl.BlockSpec((1, 1, tn), lambda l, j: (l, 0, j))],
        out_specs=pl.BlockSpec((1, nb, tn), lambda l, j: (l, 0, j)),
        out_shape=jax.ShapeDtypeStruct((DEPTH, nb, 6 * D_MODEL), F32),
        compiler_params=_params(2),
        name="ada_mod",
    )(c_all, ada_w, ada_b.reshape(DEPTH, 1, 6 * D_MODEL))


def _rms(x, g):
    return x * lax.rsqrt(jnp.mean(x * x, axis=-1, keepdims=True) + NORM_EPS) * g


def _inproj_kernel(x_ref, sh_ref, sc_ref, g_ref, w_ref, pr_ref, pm_ref, ps_ref, pg_ref, *, per_token):
    sh = sh_ref[...] if per_token else sh_ref[0]
    sc = sc_ref[...] if per_token else sc_ref[0]
    h = _rms(x_ref[...], g_ref[...]) * (1.0 + sc) + sh
    p = _dot(h.astype(BF16), w_ref[...])
    o = 0
    for ref, w in ((pr_ref, RWKV_PROJ), (pm_ref, MLSTM_MAIN), (ps_ref, SSD_MAIN), (pg_ref, GATE_W)):
        ref[...] = p[:, o:o + w]
        o += w


def _mod_spec(tm, rows_per_seq, per_token):
    if per_token:
        return pl.BlockSpec((tm, D_MODEL), lambda i: (i, 0))
    return pl.BlockSpec((1, 1, D_MODEL), lambda i: (i * tm // rows_per_seq, 0, 0))


def _inproj(x, sh, sc, g, w, *, tm, rows_per_seq, per_token):
    n = x.shape[0]
    widths = (RWKV_PROJ, MLSTM_MAIN, SSD_MAIN, GATE_W)
    wtot = sum(widths)
    return pl.pallas_call(
        functools.partial(_inproj_kernel, per_token=per_token),
        grid=(n // tm,),
        in_specs=[pl.BlockSpec((tm, D_MODEL), lambda i: (i, 0)),
                  _mod_spec(tm, rows_per_seq, per_token), _mod_spec(tm, rows_per_seq, per_token),
                  pl.BlockSpec((1, D_MODEL), lambda i: (0, 0)),
                  pl.BlockSpec((D_MODEL, wtot), lambda i: (0, 0))],
        out_specs=[pl.BlockSpec((tm, w_), lambda i: (i, 0)) for w_ in widths],
        out_shape=[jax.ShapeDtypeStruct((n, w_), F32) for w_ in widths],
        compiler_params=_params(1),
        name="in_proj",
    )(x, sh, sc, g, w)


HIST0 = HIST - (CONV_WIDTH - 1)


def _causal_conv(ext_ref, s, u, w_ref, b_ref, rows, t_valid, new_buf_ref):
    ext_ref[s, pl.ds(HIST, rows), :] = u
    out = b_ref[...]
    for j in range(CONV_WIDTH):
        out = out + ext_ref[s, pl.ds(HIST0 + j, rows), :] * w_ref[pl.ds(j, 1), :]
    last = ext_ref[s, pl.ds(HIST0 + t_valid, CONV_WIDTH - 1), :]
    new_buf_ref[s] = last
    ext_ref[s, pl.ds(HIST0, CONV_WIDTH - 1), :] = last
    return out


def _mixer_specs(nseq, rows):
    chunk = lambda w, col=0: pl.BlockSpec((nseq, rows, w), lambda i, c: (i, c, col))
    full = lambda *s: pl.BlockSpec(s, lambda i, c: (0,) * len(s))
    per_seq = lambda *s: pl.BlockSpec((nseq,) + s, lambda i, c: (i,) + (0,) * len(s))
    return chunk, full, per_seq


def _row_mask(rows, t_valid):
    if t_valid == rows:
        return None
    return lax.broadcasted_iota(jnp.int32, (rows, 1), 0) < t_valid


def _mlstm_kernel(pm_ref, pg_ref, conv0_ref, c0_ref, n0_ref, m0_ref, cw_ref, cb_ref, gb_ref, ng_ref,
                  out_ref, conv_ref, c_ref, n_ref, m_ref, ext_ref, *, rows, t_valid, nseq):
    @pl.when(pl.program_id(1) == 0)
    def _():
        ext_ref[:, pl.ds(HIST0, CONV_WIDTH - 1), :] = conv0_ref[...]
        c_ref[...] = c0_ref[...]
        n_ref[...] = n0_ref[...]
        m_ref[...] = m0_ref[...]

    valid = _row_mask(rows, t_valid)
    tril = _tri(rows)
    tril_b = tril.astype(BF16)
    eye = lax.broadcasted_iota(jnp.int32, (rows, rows), 0) == lax.broadcasted_iota(jnp.int32, (rows, rows), 1)
    gb = gb_ref[...]

    seqs = []
    for s in range(nseq):
        pm = pm_ref[s]
        qk = _silu(_causal_conv(ext_ref, s, pm[:, :MLSTM_CONV_W], cw_ref, cb_ref, rows, t_valid, conv_ref))
        gates = pg_ref[s]
        ig = gates[:, :128] + gb[:, :128]
        fg = gates[:, 128:256] + gb[:, 128:256]
        logf = -_softplus(-fg)
        if valid is not None:
            ig = jnp.where(valid, ig, NEG_BIG)
            logf = jnp.where(valid, logf, 0.0)
        bcum = _cumsum_rows(logf, tril_b)
        m_prev = m_ref[s]
        b_end = bcum[rows - 1:rows, :]
        ws = b_end - bcum + ig
        m_new = jnp.maximum(b_end + m_prev, jnp.max(ws, axis=0, keepdims=True))
        m_ref[s] = m_new
        seqs.append(dict(q=qk[:, :MLSTM_W], k=qk[:, MLSTM_W:] * HEAD_DIM ** -0.5,
                         v=pm[:, MLSTM_CONV_W:MLSTM_CONV_W + MLSTM_W], o=pm[:, MLSTM_CONV_W + MLSTM_W:],
                         ig=ig, bcum=bcum, inter=bcum + m_prev, ws=jnp.exp(ws - m_new),
                         a_end=jnp.exp(b_end + m_prev - m_new)))

    chains = [(s, h) for s in range(nseq) for h in range(MLSTM_HEADS)]
    lanes = lambda c: slice(c[1] * HEAD_DIM, (c[1] + 1) * HEAD_DIM)
    col = lambda name, c: seqs[c[0]][name][:, c[1]:c[1] + 1]
    q = [seqs[c[0]]['q'][:, lanes(c)] for c in chains]
    k = [seqs[c[0]]['k'][:, lanes(c)] for c in chains]
    v = [seqs[c[0]]['v'][:, lanes(c)] for c in chains]
    c0 = [c_ref[c[0], c[1]] for c in chains]
    n0 = [n_ref[c[0], pl.ds(c[1], 1), :] for c in chains]
    qk_t = [_dot_nt(q_, k_) for q_, k_ in zip(q, k)]
    q_c = [_dot_nt(q_, c_) for q_, c_ in zip(q, c0)]
    c_add = [_dot_tn(col('ws', c) * v_, k_) for c, v_, k_ in zip(chains, v, k)]
    dmat = [jnp.where(tril, col('bcum', c) - _to_row(col('bcum', c), eye) + _to_row(col('ig', c), eye), NEG_BIG)
            for c in chains]
    m_t = [jnp.maximum(col('inter', c), jnp.max(d_, axis=1, keepdims=True)) for c, d_ in zip(chains, dmat)]
    sc = [g_ * jnp.exp(d_ - m_) for g_, d_, m_ in zip(qk_t, dmat, m_t)]
    s_v = [_dot(s_, v_) for s_, v_ in zip(sc, v)]
    ones = jnp.ones((rows, HEAD_DIM), BF16)
    ones_hh = jnp.ones((HEAD_DIM, HEAD_DIM), BF16)
    s_sum = [_mm(s_, ones, 'nn', b_exact=True) for s_ in sc]
    q_n = [_mm(q_, jnp.broadcast_to(n_, (HEAD_DIM, HEAD_DIM)), 'nt') for q_, n_ in zip(q, n0)]
    hh = []
    for i, c in enumerate(chains):
        a_in = jnp.exp(col('inter', c) - m_t[i])
        num = s_v[i] + a_in * q_c[i]
        den = s_sum[i] + a_in * q_n[i]
        hh.append(num / jnp.maximum(jnp.abs(den), jnp.exp(-m_t[i])))
        ae = col('a_end', c)
        c_ref[c[0], c[1]] = ae * c0[i] + c_add[i]
        n_ref[c[0], pl.ds(c[1], 1), :] = ae * n0[i] + jnp.sum(col('ws', c) * k[i], axis=0, keepdims=True)
    inv_n = 1.0 / HEAD_DIM
    hc = [h_ - _mm(h_, ones_hh, 'nn', b_exact=True) * inv_n for h_ in hh]
    var = [_mm(h_ * h_, ones_hh, 'nn', b_exact=True) * inv_n for h_ in hc]
    for i, c in enumerate(chains):
        hn = hc[i] * lax.rsqrt(var[i] + NORM_EPS)
        out_ref[c[0], :, lanes(c)] = hn * ng_ref[:, lanes(c)] * _sigmoid(seqs[c[0]]['o'][:, lanes(c)])


def _mlstm(pm, pg, conv0, c0, n0, m0, cw, cb, gb, ng, *, rows, t_valid, nseq):
    b, t, _ = pm.shape
    chunk, full, per_seq = _mixer_specs(nseq, rows)
    state_specs = [per_seq(CONV_WIDTH - 1, MLSTM_CONV_W), per_seq(MLSTM_HEADS, HEAD_DIM, HEAD_DIM),
                   per_seq(MLSTM_HEADS, HEAD_DIM), per_seq(1, 128)]
    return pl.pallas_call(
        functools.partial(_mlstm_kernel, rows=rows, t_valid=t_valid, nseq=nseq),
        grid=(b // nseq, t // rows),
        in_specs=[chunk(MLSTM_MAIN), chunk(256)] + state_specs
                 + [full(CONV_WIDTH, MLSTM_CONV_W), full(1, MLSTM_CONV_W), full(1, GATE_W), full(1, MLSTM_W)],
        out_specs=[chunk(MLSTM_W)] + state_specs,
        out_shape=[jax.ShapeDtypeStruct((b, t, MLSTM_W), F32),
                   jax.ShapeDtypeStruct((b, CONV_WIDTH - 1, MLSTM_CONV_W), F32),
                   jax.ShapeDtypeStruct((b, MLSTM_HEADS, HEAD_DIM, HEAD_DIM), F32),
                   jax.ShapeDtypeStruct((b, MLSTM_HEADS, HEAD_DIM), F32),
                   jax.ShapeDtypeStruct((b, 1, 128), F32)],
        scratch_shapes=[pltpu.VMEM((nseq, HIST + rows, MLSTM_CONV_W), F32)],
        compiler_params=_params(1, 1),
        name="mlstm_mix",
    )(pm, pg, conv0, c0, n0, m0, cw, cb, gb, ng)


def _ssd_kernel(ps_ref, dt_ref, conv0_ref, h0_ref, cw_ref, cb_ref, gb_ref, alog_ref, dskip_ref, ng_ref,
                out_ref, conv_ref, h_ref, ext_ref, y_ref, *, rows, t_valid, nseq):
    @pl.when(pl.program_id(1) == 0)
    def _():
        ext_ref[:, pl.ds(HIST0, CONV_WIDTH - 1), :] = conv0_ref[...]
        h_ref[...] = h0_ref[...]

    valid = _row_mask(rows, t_valid)
    tril = _tri(rows)
    tril_b = tril.astype(BF16)
    eye = lax.broadcasted_iota(jnp.int32, (rows, rows), 0) == lax.broadcasted_iota(jnp.int32, (rows, rows), 1)
    dskip = dskip_ref[...]
    neg_a = -jnp.exp(alog_ref[...])
    gs = SSD_GROUPS * SSD_STATE

    seqs = []
    for s in range(nseq):
        ps = ps_ref[s]
        xbc = _silu(_causal_conv(ext_ref, s, ps[:, SSD_W:], cw_ref, cb_ref, rows, t_valid, conv_ref))
        dt = _softplus(dt_ref[s] + gb_ref[:, 256:384])
        if valid is not None:
            dt = jnp.where(valid, dt, 0.0)
        seqs.append(dict(z=ps[:, :SSD_W], xs=xbc[:, :SSD_W], bm=xbc[:, SSD_W:SSD_W + gs], cm=xbc[:, SSD_W + gs:],
                         dt=dt, acum=_cumsum_rows(dt * neg_a, tril_b)))

    groups = [(s, g) for s in range(nseq) for g in range(SSD_GROUPS)]
    grp = lambda name, sg: seqs[sg[0]][name][:, sg[1] * SSD_STATE:(sg[1] + 1) * SSD_STATE]
    cb = {sg: _dot_nt(grp('cm', sg), grp('bm', sg)) for sg in groups}
    chains = [(s, hd) for s in range(nseq) for hd in range(SSD_HEADS)]
    lanes = lambda c: slice(c[1] * HEAD_DIM, (c[1] + 1) * HEAD_DIM)
    col = lambda name, c: seqs[c[0]][name][:, c[1]:c[1] + 1]
    group_of = lambda c: (c[0], c[1] // SSD_HEADS_PER_GROUP)
    x = [seqs[c[0]]['xs'][:, lanes(c)] for c in chains]
    h0 = [h_ref[c[0], c[1]] for c in chains]
    from_state = [_dot_nt(grp('cm', group_of(c)), h_) for c, h_ in zip(chains, h0)]
    h_add = [_dot_tn(jnp.exp(col('acum', c)[rows - 1:rows, :] - col('acum', c)) * col('dt', c) * x_,
                     grp('bm', group_of(c))) for c, x_ in zip(chains, x)]
    wmat = [jnp.exp(jnp.where(tril, col('acum', c) - _to_row(col('acum', c), eye), NEG_BIG))
            * cb[group_of(c)] * _to_row(col('dt', c), eye) for c in chains]
    y = [_dot(w_, x_) for w_, x_ in zip(wmat, x)]
    for i, c in enumerate(chains):
        a_col = col('acum', c)
        h_ref[c[0], c[1]] = jnp.exp(a_col[rows - 1:rows, :]) * h0[i] + h_add[i]
        y_ref[c[0], :, lanes(c)] = y[i] + jnp.exp(a_col) * from_state[i] + dskip[:, c[1]:c[1] + 1] * x[i]
    for s in range(nseq):
        out_ref[s] = _rms(y_ref[s] * _silu(seqs[s]['z']), ng_ref[...])


def _ssd(ps, pg, conv0, h0, cw, cb, gb, alog, dskip, ng, *, rows, t_valid, nseq):
    b, t, _ = ps.shape
    chunk, full, per_seq = _mixer_specs(nseq, rows)
    state_specs = [per_seq(CONV_WIDTH - 1, SSD_CONV_W), per_seq(SSD_HEADS, HEAD_DIM, SSD_STATE)]
    return pl.pallas_call(
        functools.partial(_ssd_kernel, rows=rows, t_valid=t_valid, nseq=nseq),
        grid=(b // nseq, t // rows),
        in_specs=[chunk(SSD_MAIN), chunk(128, col=2)] + state_specs
                 + [full(CONV_WIDTH, SSD_CONV_W), full(1, SSD_CONV_W), full(1, GATE_W),
                    full(1, 128), full(1, 128), full(1, SSD_W)],
        out_specs=[chunk(SSD_W)] + state_specs,
        out_shape=[jax.ShapeDtypeStruct((b, t, SSD_W), F32),
                   jax.ShapeDtypeStruct((b, CONV_WIDTH - 1, SSD_CONV_W), F32),
                   jax.ShapeDtypeStruct((b, SSD_HEADS, HEAD_DIM, SSD_STATE), F32)],
        scratch_shapes=[pltpu.VMEM((nseq, HIST + rows, SSD_CONV_W), F32), pltpu.VMEM((nseq, rows, SSD_W), F32)],
        compiler_params=_params(1, 1),
        name="ssd_mix",
    )(ps, pg, conv0, h0, cw, cb, gb, alog, dskip, ng)


def _rwkv_kernel(pr_ref, shift0_ref, s0_ref, mu_ref, w0_ref, wup_ref, a0_ref, aup_ref, gup_ref, kk_ref, ka_ref,
                 rk_ref, lng_ref, lnb_ref, bd_ref,
                 out_ref, shift_ref, s_ref, ext_ref, y_ref, *, rows, t_valid, nseq):
    @pl.when(pl.program_id(1) == 0)
    def _():
        ext_ref[:, pl.ds(HIST - 1, 1), :] = shift0_ref[...]
        s_ref[...] = s0_ref[...]

    tril = _tri(rows)
    stril = _tri(rows, strict=True)
    bd = bd_ref[...]
    head_sum = lambda t: _mm(t, bd, 'nn', b_exact=True)
    w3 = 3 * RWKV_W
    n = nseq * rows

    p3 = pr_ref[...]
    ext_ref[:, pl.ds(HIST, rows), :] = p3
    prev = ext_ref[:, pl.ds(HIST - 1, rows), :].reshape(n, RWKV_PROJ)
    last = ext_ref[:, pl.ds(HIST - 1 + t_valid, 1), :]
    shift_ref[...] = last
    ext_ref[:, pl.ds(HIST - 1, 1), :] = last
    p = p3.reshape(n, RWKV_PROJ)
    x = p + (prev - p) * mu_ref[...]
    r = x[:, :RWKV_W]
    k = x[:, RWKV_W:2 * RWKV_W]
    v = x[:, 2 * RWKV_W:w3]
    xw = x[:, w3:w3 + RWKV_DECAY_RANK]
    xa = x[:, w3 + RWKV_DECAY_RANK:w3 + RWKV_DECAY_RANK + RWKV_ICLR_RANK]
    xg = x[:, w3 + RWKV_DECAY_RANK + RWKV_ICLR_RANK:]
    w_log = -_softplus(-(w0_ref[...] + _dot(jnp.tanh(xw).astype(BF16), wup_ref[...]))) - 0.5
    logw = -jnp.exp(w_log)
    a = _sigmoid(a0_ref[...] + _dot(xa.astype(BF16), aup_ref[...]))
    g = _dot(_sigmoid(xg).astype(BF16), gup_ref[...])
    kk = k * kk_ref[...]
    kk = kk / jnp.maximum(jnp.sqrt(head_sum(kk * kk)), 1e-12)
    k = k * (1.0 + (a - 1.0) * ka_ref[...])
    ri = lax.broadcasted_iota(jnp.int32, (n, n), 0)
    ci = lax.broadcasted_iota(jnp.int32, (n, n), 1)
    same_seq = (ri // rows) == (ci // rows)
    if t_valid != rows:
        valid = (lax.broadcasted_iota(jnp.int32, (n, 1), 0) % rows) < t_valid
        logw = jnp.where(valid, logw, 0.0)
        kk = jnp.where(valid, kk, 0.0)
        k = jnp.where(valid, k, 0.0)

    cum = _cumsum_rows(logw, (same_seq & (ri >= ci)).astype(BF16))
    to_end = jnp.exp(_cumsum_rows(logw, (same_seq & (ri < ci)).astype(BF16)))
    p_inv = jnp.exp(-cum)
    a_t = -kk * jnp.exp(cum - logw)
    kka = kk * a
    b_t = kka * p_inv
    k_t = k * p_inv
    r_t = r * jnp.exp(cum)
    b_e = kka * to_end
    k_e = k * to_end
    p_end = jnp.exp(cum)

    chains = [(s, h) for s in range(nseq) for h in range(RWKV_HEADS)]
    blk = lambda t, c: t[c[0] * rows:(c[0] + 1) * rows, c[1] * HEAD_DIM:(c[1] + 1) * HEAD_DIM]
    s0 = [s_ref[c[0], c[1]] for c in chains]
    vh = [blk(v, c) for c in chains]
    ar = [jnp.concatenate([blk(a_t, c), blk(r_t, c)], axis=0) for c in chains]
    bk = [jnp.concatenate([blk(b_t, c), blk(k_t, c)], axis=0) for c in chains]
    mm = functools.partial(_mm, single=True)
    gram = [mm(x_, y_, 'nt') for x_, y_ in zip(ar, bk)]
    from_state = [mm(x_, y_, 'nt') for x_, y_ in zip(ar, s0)]
    n_mat = [jnp.where(stril, g_[:rows, :rows], 0.0) for g_ in gram]
    u = [f_[:rows] + mm(jnp.where(stril, g_[:rows, rows:], 0.0), v_, 'nn')
         for f_, g_, v_ in zip(from_state, gram, vh)]
    span = 1
    while span < rows:
        u = [u_ + mm(n_, u_, 'nn') for u_, n_ in zip(u, n_mat)]
        span *= 2
        if span < rows:
            n_mat = [mm(n_, n_, 'nn') for n_ in n_mat]
    y_u = [mm(jnp.where(tril, g_[rows:, :rows], 0.0), u_, 'nn') for g_, u_ in zip(gram, u)]
    y_v = [mm(jnp.where(tril, g_[rows:, rows:], 0.0), v_, 'nn') for g_, v_ in zip(gram, vh)]
    s_add = [mm(jnp.concatenate([u_, v_], axis=0), jnp.concatenate([blk(b_e, c), blk(k_e, c)], axis=0), 'tn')
             for u_, v_, c in zip(u, vh, chains)]
    for i, c in enumerate(chains):
        lanes = slice(c[1] * HEAD_DIM, (c[1] + 1) * HEAD_DIM)
        y_ref[c[0], :, lanes] = from_state[i][rows:] + y_u[i] + y_v[i]
        end_row = (c[0] + 1) * rows - 1
        s_ref[c[0], c[1]] = p_end[end_row:end_row + 1, lanes] * s0[i] + s_add[i]

    y = y_ref[...].reshape(n, RWKV_W)
    inv_n = 1.0 / HEAD_DIM
    yc = y - head_sum(y) * inv_n
    yn = yc * lax.rsqrt(head_sum(yc * yc) * inv_n + RWKV_LN_EPS)
    bonus = head_sum(r * k * rk_ref[...]) * v
    out_ref[...] = ((yn * lng_ref[...] + lnb_ref[...] + bonus) * g).reshape(nseq, rows, RWKV_W)


def _rwkv(pr, shift0, s0, mu, w0, wup, a0, aup, gup, kkp, kap, rk, lng, lnb, bd, *, rows, t_valid, nseq):
    b, t, _ = pr.shape
    chunk, full, per_seq = _mixer_specs(nseq, rows)
    state_specs = [per_seq(1, RWKV_PROJ), per_seq(RWKV_HEADS, HEAD_DIM, HEAD_DIM)]
    return pl.pallas_call(
        functools.partial(_rwkv_kernel, rows=rows, t_valid=t_valid, nseq=nseq),
        grid=(b // nseq, t // rows),
        in_specs=[chunk(RWKV_PROJ)] + state_specs
                 + [full(1, RWKV_PROJ), full(1, RWKV_W), full(RWKV_DECAY_RANK, RWKV_W), full(1, RWKV_W),
                    full(RWKV_ICLR_RANK, RWKV_W), full(RWKV_GATE_RANK, RWKV_W), full(1, RWKV_W), full(1, RWKV_W),
                    full(1, RWKV_W), full(1, RWKV_W), full(1, RWKV_W), full(RWKV_W, RWKV_W)],
        out_specs=[chunk(RWKV_W)] + state_specs,
        out_shape=[jax.ShapeDtypeStruct((b, t, RWKV_W), F32),
                   jax.ShapeDtypeStruct((b, 1, RWKV_PROJ), F32),
                   jax.ShapeDtypeStruct((b, RWKV_HEADS, HEAD_DIM, HEAD_DIM), F32)],
        scratch_shapes=[pltpu.VMEM((nseq, HIST + rows, RWKV_PROJ), F32), pltpu.VMEM((nseq, rows, RWKV_W), F32)],
        compiler_params=_params(1, 1),
        name="rwkv_mix",
    )(pr, shift0, s0, mu, w0, wup, a0, aup, gup, kkp, kap, rk, lng, lnb, bd)


def _outproj_kernel(or_ref, om_ref, os_ref, x_ref, g1_ref, sh_ref, sc_ref, ng_ref, w_ref, xmid_ref, h2_ref,
                    *, per_token):
    pick = (lambda r: r[...]) if per_token else (lambda r: r[0])
    w = w_ref[...]
    mix = (_dot(or_ref[...].astype(BF16), w[:RWKV_W])
           + _dot(om_ref[...].astype(BF16), w[RWKV_W:RWKV_W + MLSTM_W])
           + _dot(os_ref[...].astype(BF16), w[RWKV_W + MLSTM_W:]))
    x = x_ref[...] + pick(g1_ref) * mix
    xmid_ref[...] = x
    h2_ref[...] = (_rms(x, ng_ref[...]) * (1.0 + pick(sc_ref)) + pick(sh_ref)).astype(BF16)


def _outproj(o_r, o_m, o_s, x, g1, sh, sc, ng, w, *, tm, rows_per_seq, per_token):
    n = x.shape[0]
    tok = lambda w_: pl.BlockSpec((tm, w_), lambda i: (i, 0))
    ms = _mod_spec(tm, rows_per_seq, per_token)
    return pl.pallas_call(
        functools.partial(_outproj_kernel, per_token=per_token),
        grid=(n // tm,),
        in_specs=[tok(RWKV_W), tok(MLSTM_W), tok(SSD_W), tok(D_MODEL), ms, ms, ms,
                  pl.BlockSpec((1, D_MODEL), lambda i: (0, 0)),
                  pl.BlockSpec((D_MODEL, D_MODEL), lambda i: (0, 0))],
        out_specs=[tok(D_MODEL), tok(D_MODEL)],
        out_shape=[jax.ShapeDtypeStruct((n, D_MODEL), F32), jax.ShapeDtypeStruct((n, D_MODEL), BF16)],
        compiler_params=_params(1),
        name="out_proj",
    )(o_r, o_m, o_s, x, g1, sh, sc, ng, w)


CAND_LEN = tuple(PEER_TOPK // (p + 1) for p in range(PEER_TOPK))
CAND_OFF = tuple(sum(CAND_LEN[:p]) for p in range(PEER_TOPK))
CAND_ROWS = -(-sum(CAND_LEN) // 8) * 8


def _top1(x, iota):
    m = jnp.max(x, axis=0, keepdims=True)
    idx = jnp.min(jnp.where(x == m, iota, float(x.shape[0])), axis=0, keepdims=True)
    return m, idx


def _gelu_tanh(x):
    return x * (0.5 * (1.0 + jnp.tanh(math.sqrt(2.0 / math.pi) * (x + 0.044715 * (x * x * x)))))


def _peer_kernel(h2c_ref, h2n_ref, wq_ref, sk_ref, u_ref, v_ref, xmid_ref, g2_ref, fg_ref, o_ref,
                 acc_ref, act_ref, gm_ref, cand_ref, eid_ref, et_ref, gt_ref, a_ref, b_ref, g_ref, tmp_ref,
                 *, tm, per_token, final_norm):
    j = pl.program_id(1)

    @pl.when((pl.program_id(0) == 0) & (j == 0))
    def _():
        acc_ref[...] = jnp.zeros_like(acc_ref)
        act_ref[...] = jnp.zeros_like(act_ref)
        gm_ref[...] = jnp.zeros_like(gm_ref)

    q = _dot(h2n_ref[...], wq_ref[...]).astype(BF16)
    s12 = [_dot_nt(sk_ref[c], q[:, c * PEER_HALF:(c + 1) * PEER_HALF]) for c in range(2)]
    acc_ref[...] += _dot(act_ref[...], v_ref[...])
    s = _dot_nt(h2c_ref[...], u_ref[...])

    rows_per_block = u_ref.shape[0] // PEER_KEYS
    for r in range(rows_per_block):
        lanes = slice(r * PEER_KEYS, (r + 1) * PEER_KEYS)
        act_ref[:, lanes] = (_gelu_tanh(s[:, lanes]) * gm_ref[j * rows_per_block + r].astype(F32)).astype(BF16)

    key_iota = lax.broadcasted_iota(jnp.int32, (PEER_KEYS, tm), 0).astype(F32)
    vals = ([], [])
    idxs = ([], [])
    for _ in range(PEER_TOPK):
        for c in range(2):
            m, idx = _top1(s12[c], key_iota)
            s12[c] = jnp.where(key_iota == idx, -jnp.inf, s12[c])
            vals[c].append(m)
            idxs[c].append(idx)
    v1, v2 = (jnp.concatenate(v_, axis=0) for v_ in vals)
    i1, i2 = (jnp.concatenate(i_, axis=0) for i_ in idxs)

    cand_ref[...] = jnp.full(cand_ref.shape, -jnp.inf, F32)
    eid_ref[...] = jnp.zeros(eid_ref.shape, F32)
    for p in range(PEER_TOPK):
        cand_ref[pl.ds(CAND_OFF[p], CAND_LEN[p]), :] = v1[p:p + 1, :] + v2[:CAND_LEN[p], :]
        eid_ref[pl.ds(CAND_OFF[p], CAND_LEN[p]), :] = i1[p:p + 1, :] * float(PEER_KEYS) + i2[:CAND_LEN[p], :]
    cand_iota = lax.broadcasted_iota(jnp.int32, (CAND_ROWS, tm), 0).astype(F32)
    cand = cand_ref[...]
    eid = eid_ref[...]
    tops, picks = [], []
    for _ in range(PEER_TOPK):
        m, pos = _top1(cand, cand_iota)
        hit = cand_iota == pos
        cand = jnp.where(hit, -jnp.inf, cand)
        tops.append(m)
        picks.append(jnp.sum(jnp.where(hit, eid, 0.0), axis=0, keepdims=True))
    top = jnp.concatenate(tops, axis=0)
    e = jnp.exp(top - top[0:1, :])
    row0 = pl.multiple_of(j * PEER_TOPK, PEER_TOPK)
    et_ref[pl.ds(row0, PEER_TOPK), :] = jnp.concatenate(picks, axis=0)
    gt_ref[pl.ds(row0, PEER_TOPK), :] = e / jnp.sum(e, axis=0, keepdims=True)

    @pl.when(j == 0)
    def _():
        g2 = g2_ref[...] if per_token else g2_ref[0]
        x = xmid_ref[...] + g2 * acc_ref[...]
        o_ref[...] = _rms(x, fg_ref[...]) if final_norm else x
        acc_ref[...] = jnp.zeros_like(acc_ref)

    @pl.when(j == PEER_HEADS - 1)
    def _():
        picked = et_ref[...]
        key1 = jnp.floor(picked * (1.0 / PEER_KEYS))
        a_ref[...] = key1.T
        b_ref[...] = (picked - key1 * float(PEER_KEYS)).T
        g_ref[...] = gt_ref[...].T
        j_iota = lax.broadcasted_iota(jnp.int32, (PEER_KEYS, PEER_HEADS * PEER_TOPK), 0).astype(F32)

        def token_groups(gi, carry):
            for half in range(2):
                t0 = pl.multiple_of((gi * 2 + half) * TOKEN_GROUP, TOKEN_GROUP)
                base = half * TOKEN_GROUP * TOKEN_PITCH
                for tt in range(TOKEN_GROUP):
                    t = t0 + tt
                    sel1 = jnp.where(j_iota == a_ref[pl.ds(t, 1), :], 1.0, 0.0).astype(BF16)
                    sel2 = jnp.where(j_iota == b_ref[pl.ds(t, 1), :], g_ref[pl.ds(t, 1), :], 0.0).astype(BF16)
                    tmp_ref[pl.ds(base + tt * TOKEN_PITCH, PEER_KEYS), :] = _dot_nt(sel1, sel2)
            for half in range(2):
                t0 = pl.multiple_of((gi * 2 + half) * TOKEN_GROUP, TOKEN_GROUP)
                base = half * TOKEN_GROUP * TOKEN_PITCH
                for r in range(PEER_KEYS):
                    rows = tmp_ref[pl.ds(base + r, TOKEN_GROUP, stride=TOKEN_PITCH), :]
                    gm_ref[r, pl.ds(t0, TOKEN_GROUP), :] = rows.astype(BF16)
            return carry

        lax.fori_loop(0, tm // (2 * TOKEN_GROUP), token_groups, 0)


def _peer(h2, wq, sk, u, v, xmid, g2, fg, *, tm, rows_per_seq, per_token, final_norm):
    n = h2.shape[0]
    ntiles = n // tm
    eb = PEER_EXPERTS // PEER_HEADS
    npick = PEER_HEADS * PEER_TOPK
    clamp = lambda t: jnp.clip(t, 0, ntiles - 1)
    routed = lambda g, j: clamp(g)
    stage_a = lambda g, j: clamp(g - 1)
    done = lambda g, j: clamp(g - 2 + jnp.minimum(j, 1))
    if per_token:
        ms = pl.BlockSpec((tm, D_MODEL), lambda g, j: (done(g, j), 0))
    else:
        ms = pl.BlockSpec((1, 1, D_MODEL), lambda g, j: (done(g, j) * tm // rows_per_seq, 0, 0))
    return pl.pallas_call(
        functools.partial(_peer_kernel, tm=tm, per_token=per_token, final_norm=final_norm),
        grid=(ntiles + 2, PEER_HEADS),
        in_specs=[pl.BlockSpec((tm, D_MODEL), lambda g, j: (stage_a(g, j), 0)),
                  pl.BlockSpec((tm, D_MODEL), lambda g, j: (routed(g, j), 0)),
                  pl.BlockSpec((D_MODEL, PEER_QDIM), lambda g, j: (0, j)),
                  pl.BlockSpec((2, PEER_KEYS, PEER_HALF), lambda g, j: (0, 0, 0)),
                  pl.BlockSpec((eb, D_MODEL), lambda g, j: (j, 0)),
                  pl.BlockSpec((eb, D_MODEL), lambda g, j: ((j + PEER_HEADS - 1) % PEER_HEADS, 0)),
                  pl.BlockSpec((tm, D_MODEL), lambda g, j: (done(g, j), 0)),
                  ms,
                  pl.BlockSpec((1, D_MODEL), lambda g, j: (0, 0))],
        out_specs=pl.BlockSpec((tm, D_MODEL), lambda g, j: (done(g, j), 0)),
        out_shape=jax.ShapeDtypeStruct((n, D_MODEL), F32),
        scratch_shapes=[pltpu.VMEM((tm, D_MODEL), F32),
                        pltpu.VMEM((tm, eb), BF16),
                        pltpu.VMEM((PEER_KEYS, tm, PEER_KEYS), BF16),
                        pltpu.VMEM((CAND_ROWS, tm), F32), pltpu.VMEM((CAND_ROWS, tm), F32),
                        pltpu.VMEM((npick, tm), F32), pltpu.VMEM((npick, tm), F32),
                        pltpu.VMEM((tm, npick), F32), pltpu.VMEM((tm, npick), F32), pltpu.VMEM((tm, npick), F32),
                        pltpu.VMEM((2 * TOKEN_GROUP * TOKEN_PITCH, PEER_KEYS), F32)],
        compiler_params=_params(0, 2),
        name="peer",
    )(h2, h2, wq, sk, u, v, xmid, g2, fg)


TOKEN_TILE = 512
PEER_TILE = 256
SAMPLE_ROWS = 8
PROMPT_SEQS_PER_STEP = 4
SAMPLE_SEQS_PER_STEP = 8


def _lane_pad(vec, offset=0, width=128):
    return jnp.zeros((1, width), F32).at[0, offset:offset + vec.shape[0]].set(vec)


def _layer_weights(l, w):
    w_in = w['w_in'][l]
    o_m = RWKV_PROJ
    o_mg = o_m + MLSTM_MAIN
    o_s = o_mg + 2 * MLSTM_HEADS
    o_sg = o_s + SSD_MAIN
    gate_cols = jnp.zeros((D_MODEL, GATE_W), F32)
    gate_cols = gate_cols.at[:, 0:MLSTM_HEADS].set(w_in[:, o_mg:o_mg + MLSTM_HEADS])
    gate_cols = gate_cols.at[:, 128:128 + MLSTM_HEADS].set(w_in[:, o_mg + MLSTM_HEADS:o_s])
    gate_cols = gate_cols.at[:, 256:256 + SSD_HEADS].set(w_in[:, o_sg:o_sg + SSD_HEADS])
    w_cat = jnp.concatenate([w_in[:, :o_m], w_in[:, o_m:o_mg], w_in[:, o_s:o_sg], gate_cols], axis=1).astype(BF16)
    gate_bias = jnp.concatenate([_lane_pad(w['mlstm_i_b'][l]), _lane_pad(w['mlstm_f_b'][l]),
                                 _lane_pad(w['ssd_dt_bias'][l])], axis=1)
    row = lambda name: w[name][l].reshape(1, -1)
    return dict(
        w_cat=w_cat, gate_bias=gate_bias, w_out=w['w_out'][l].astype(BF16),
        norm1_g=row('norm1_g'), norm2_g=row('norm2_g'),
        rwkv=(row('rwkv_mu'), row('rwkv_w0'), w['rwkv_w_up'][l].astype(BF16), row('rwkv_a0'),
              w['rwkv_a_up'][l].astype(BF16), w['rwkv_g_up'][l].astype(BF16), row('rwkv_k_k'), row('rwkv_k_a'),
              row('rwkv_r_k'), row('rwkv_ln_g'), row('rwkv_ln_b')),
        mlstm=(w['mlstm_conv_w'][l], row('mlstm_conv_b'), gate_bias, row('mlstm_norm_g')),
        ssd=(w['ssd_conv_w'][l], row('ssd_conv_b'), gate_bias, _lane_pad(w['ssd_A_log'][l]),
             _lane_pad(w['ssd_D'][l]), row('ssd_norm_g')),
        wq=w['peer_wq'][l].astype(BF16), sk=w['peer_subkeys'][l].astype(BF16),
        u=w['peer_u'][l].astype(BF16), v=w['peer_v'][l].astype(BF16),
    )


def _run_layer(x, mods, state, lw, head_sum, final_g, *, batch, seq, rows, nseq, per_token, final_norm):
    sh1, sc1, g1, sh2, sc2, g2 = mods
    shift, wkv, mconv, m_c, m_n, m_m, sconv, sst = state
    tm = min(TOKEN_TILE, batch * seq)
    tok = dict(tm=tm, rows_per_seq=seq, per_token=per_token)
    p_r, p_m, p_s, p_g = _inproj(x, sh1, sc1, lw['norm1_g'], lw['w_cat'], **tok)

    t_pad = -(-seq // rows) * rows
    t_valid = seq if seq < rows else rows

    def seqs(a):
        a = a.reshape(batch, seq, a.shape[-1])
        return a if t_pad == seq else jnp.pad(a, ((0, 0), (0, t_pad - seq), (0, 0)))

    def toks(a):
        return a[:, :seq].reshape(batch * seq, a.shape[-1])

    mix = dict(rows=rows, t_valid=t_valid, nseq=nseq)
    o_r, shift, wkv = _rwkv(seqs(p_r), shift[:, None, :], wkv, *lw['rwkv'], head_sum, **mix)
    m_m = jnp.pad(m_m, ((0, 0), (0, 128 - MLSTM_HEADS)))[:, None, :]
    o_m, mconv, m_c, m_n, m_m = _mlstm(seqs(p_m), seqs(p_g), mconv, m_c, m_n, m_m, *lw['mlstm'], **mix)
    o_s, sconv, sst = _ssd(seqs(p_s), seqs(p_g), sconv, sst, *lw['ssd'], **mix)
    new_state = (shift[:, 0, :], wkv, mconv, m_c, m_n, m_m[:, 0, :MLSTM_HEADS], sconv, sst)

    x_mid, h2 = _outproj(toks(o_r), toks(o_m), toks(o_s), x, g1, sh2, sc2, lw['norm2_g'], lw['w_out'], **tok)
    x_new = _peer(h2, lw['wq'], lw['sk'], lw['u'], lw['v'], x_mid, g2, final_g, tm=min(PEER_TILE, batch * seq),
                  rows_per_seq=seq, per_token=per_token, final_norm=final_norm)
    return x_new, new_state


def kernel(x_prompt, x_sample, state_rwkv_shift, state_rwkv_wkv, state_mlstm_conv, state_mlstm_C, state_mlstm_n, state_mlstm_m, state_ssd_conv, state_ssd, c_prompt, c_sample, ada_w, ada_b, norm1_g, norm2_g, w_in, w_out, rwkv_mu, rwkv_w0, rwkv_w_up, rwkv_a0, rwkv_a_up, rwkv_g_up, rwkv_k_k, rwkv_k_a, rwkv_r_k, rwkv_ln_g, rwkv_ln_b, mlstm_conv_w, mlstm_conv_b, mlstm_i_b, mlstm_f_b, mlstm_norm_g, ssd_conv_w, ssd_conv_b, ssd_dt_bias, ssd_A_log, ssd_D, ssd_norm_g, peer_wq, peer_subkeys, peer_u, peer_v, final_g):
    weights = dict(norm1_g=norm1_g, norm2_g=norm2_g, w_in=w_in, w_out=w_out, rwkv_mu=rwkv_mu, rwkv_w0=rwkv_w0,
                   rwkv_w_up=rwkv_w_up, rwkv_a0=rwkv_a0, rwkv_a_up=rwkv_a_up, rwkv_g_up=rwkv_g_up,
                   rwkv_k_k=rwkv_k_k, rwkv_k_a=rwkv_k_a, rwkv_r_k=rwkv_r_k, rwkv_ln_g=rwkv_ln_g,
                   rwkv_ln_b=rwkv_ln_b, mlstm_conv_w=mlstm_conv_w, mlstm_conv_b=mlstm_conv_b,
                   mlstm_i_b=mlstm_i_b, mlstm_f_b=mlstm_f_b, mlstm_norm_g=mlstm_norm_g, ssd_conv_w=ssd_conv_w,
                   ssd_conv_b=ssd_conv_b, ssd_dt_bias=ssd_dt_bias, ssd_A_log=ssd_A_log, ssd_D=ssd_D,
                   ssd_norm_g=ssd_norm_g, peer_wq=peer_wq, peer_subkeys=peer_subkeys, peer_u=peer_u, peer_v=peer_v)
    nb, seq, d = x_prompt.shape
    nd, dseq, _ = x_sample.shape
    cache = (state_rwkv_shift, state_rwkv_wkv, state_mlstm_conv, state_mlstm_C, state_mlstm_n, state_mlstm_m,
             state_ssd_conv, state_ssd)

    mod = _ada_mod(jnp.concatenate([c_prompt, c_sample], axis=0), ada_w, ada_b)
    head_id = jnp.arange(RWKV_W) // HEAD_DIM
    head_sum = (head_id[:, None] == head_id[None, :]).astype(F32)
    fg = final_g.reshape(1, d)

    xp = x_prompt.reshape(nb * seq, d)
    xs = x_sample.reshape(nd * dseq, d)
    new_p, new_s = [], []
    for l in range(DEPTH):
        lw = _layer_weights(l, weights)
        last = l == DEPTH - 1
        mod_p = mod[l, :nb].reshape(nb, 6, 1, d)
        mods_p = tuple(mod_p[:, i] for i in range(6))
        mod_s = jnp.repeat(mod[l, nb:].reshape(nd, 6, d), dseq, axis=0)
        mods_s = tuple(mod_s[:, i] for i in range(6))
        zeros = tuple(jnp.zeros((nb,) + s.shape[2:], F32) for s in cache)
        xp, sp = _run_layer(xp, mods_p, zeros, lw, head_sum, fg, batch=nb, seq=seq, rows=CHUNK,
                            nseq=PROMPT_SEQS_PER_STEP, per_token=False, final_norm=last)
        xs, ss = _run_layer(xs, mods_s, tuple(s[l] for s in cache), lw, head_sum, fg, batch=nd, seq=dseq,
                            rows=SAMPLE_ROWS, nseq=SAMPLE_SEQS_PER_STEP, per_token=True, final_norm=last)
        new_p.append(sp)
        new_s.append(ss)

    def stk(states, i):
        return jnp.stack([st[i] for st in states])

    return ((xp.reshape(nb, seq, d), xs.reshape(nd, dseq, d))
            + tuple(stk(new_p, i) for i in range(8)) + tuple(stk(new_s, i) for i in range(8)))
```

```python
import functools
import math

import jax
import jax.numpy as jnp
from jax import lax
from jax.experimental import pallas as pl
from jax.experimental.pallas import tpu as pltpu

F32 = jnp.float32
BF16 = jnp.bfloat16

D_MODEL = 1024
DEPTH = 4
HEAD_DIM = 64
RWKV_W = 256
RWKV_HEADS = 4
RWKV_DECAY_RANK = 64
RWKV_ICLR_RANK = 64
RWKV_GATE_RANK = 128
RWKV_LN_EPS = 1e-5 * HEAD_DIM
RWKV_PROJ = 3 * RWKV_W + RWKV_DECAY_RANK + RWKV_ICLR_RANK + RWKV_GATE_RANK
MLSTM_W = 256
MLSTM_HEADS = 4
MLSTM_CONV_W = 2 * MLSTM_W
MLSTM_MAIN = MLSTM_CONV_W + 2 * MLSTM_W
SSD_W = 512
SSD_HEADS = 8
SSD_STATE = 128
SSD_GROUPS = 2
SSD_HEADS_PER_GROUP = SSD_HEADS // SSD_GROUPS
SSD_CONV_W = SSD_W + 2 * SSD_GROUPS * SSD_STATE
SSD_MAIN = SSD_W + SSD_CONV_W
CONV_WIDTH = 4
CHUNK = 64
GATE_W = 3 * 128
PEER_KEYS = 128
PEER_EXPERTS = PEER_KEYS * PEER_KEYS
PEER_HEADS = 8
PEER_TOPK = 16
PEER_QDIM = 256
PEER_HALF = 128
NORM_EPS = 1e-6
NEG_BIG = -1e30
TOKEN_GROUP = 16
TOKEN_PITCH = PEER_KEYS + 8

VMEM_LIMIT_BYTES = 56 * 1024 * 1024
HIST = 8


def _dot(a, b, prec=None):
    return jnp.dot(a, b, preferred_element_type=F32, precision=prec)


def _dot_nt(a, b, prec=None):
    return lax.dot_general(a, b, (((1,), (1,)), ((), ())), preferred_element_type=F32, precision=prec)


def _dot_tn(a, b, prec=None):
    return lax.dot_general(a, b, (((0,), (0,)), ((), ())), preferred_element_type=F32, precision=prec)


def _hi_lo(x):
    hi = x.astype(BF16)
    return hi, (x - hi.astype(F32)).astype(BF16)


def _mm(a, b, form, b_exact=False, single=False):
    dot = {'nn': _dot, 'nt': _dot_nt, 'tn': _dot_tn}[form]
    if single:
        return dot(a.astype(BF16), b.astype(BF16))
    ah, al = _hi_lo(a)
    a_axis = 0 if form == 'tn' else 1
    b_axis = 1 if form == 'nt' else 0
    if b_exact:
        bb = b.astype(BF16)
        a3 = jnp.concatenate([ah, al], axis=a_axis)
        b3 = jnp.concatenate([bb, bb], axis=b_axis)
    else:
        bh, bl = _hi_lo(b)
        a3 = jnp.concatenate([ah, ah, al], axis=a_axis)
        b3 = jnp.concatenate([bh, bl, bh], axis=b_axis)
    return dot(a3, b3)


def _cumsum_rows(x, tril_bf16):
    hi = x.astype(BF16)
    r1 = x - hi.astype(F32)
    mid = r1.astype(BF16)
    lo = (r1 - mid.astype(F32)).astype(BF16)
    return _dot(jnp.concatenate([tril_bf16] * 3, axis=1), jnp.concatenate([hi, mid, lo], axis=0))


def _sigmoid(x):
    return 1.0 / (1.0 + jnp.exp(-x))


def _silu(x):
    return x * _sigmoid(x)


def _softplus(x):
    return jnp.maximum(x, 0.0) + jnp.log(1.0 + jnp.exp(-jnp.abs(x)))


def _tri(n, strict=False):
    r = lax.broadcasted_iota(jnp.int32, (n, n), 0)
    c = lax.broadcasted_iota(jnp.int32, (n, n), 1)
    return (r > c) if strict else (r >= c)


def _to_row(col, eye):
    return jnp.sum(jnp.where(eye, col, 0.0), axis=0, keepdims=True)


def _params(n_parallel, n_arbitrary=0):
    sem = ("parallel",) * n_parallel + ("arbitrary",) * n_arbitrary
    return pltpu.CompilerParams(dimension_semantics=sem, vmem_limit_bytes=VMEM_LIMIT_BYTES)


def _ada_kernel(c_ref, w_ref, b_ref, o_ref):
    c = _silu(c_ref[...]).astype(BF16)
    o_ref[0] = _dot(c, w_ref[0].astype(BF16)) + b_ref[0]


def _ada_mod(c_all, ada_w, ada_b):
    nb = c_all.shape[0]
    tn = 1536
    return pl.pallas_call(
        _ada_kernel,
        grid=(DEPTH, 6 * D_MODEL // tn),
        in_specs=[pl.BlockSpec((nb, D_MODEL), lambda l, j: (0, 0)),
                  pl.BlockSpec((1, D_MODEL, tn), lambda l, j: (l, 0, j)),
                  pl.BlockSpec((1, 1, tn), lambda l, j: (l, 0, j))],
        out_specs=pl.BlockSpec((1, nb, tn), lambda l, j: (l, 0, j)),
        out_shape=jax.ShapeDtypeStruct((DEPTH, nb, 6 * D_MODEL), F32),
        compiler_params=_params(2),
        name="ada_mod",
    )(c_all, ada_w, ada_b.reshape(DEPTH, 1, 6 * D_MODEL))


def _rms(x, g):
    return x * lax.rsqrt(jnp.mean(x * x, axis=-1, keepdims=True) + NORM_EPS) * g


def _inproj_kernel(x_ref, sh_ref, sc_ref, g_ref, w_ref, pr_ref, pm_ref, ps_ref, pg_ref, *, per_token):
    sh = sh_ref[...] if per_token else sh_ref[0]
    sc = sc_ref[...] if per_token else sc_ref[0]
    h = _rms(x_ref[...], g_ref[...]) * (1.0 + sc) + sh
    p = _dot(h.astype(BF16), w_ref[...])
    o = 0
    for ref, w in ((pr_ref, RWKV_PROJ), (pm_ref, MLSTM_MAIN), (ps_ref, SSD_MAIN), (pg_ref, GATE_W)):
        ref[...] = p[:, o:o + w]
        o += w


def _mod_spec(tm, rows_per_seq, per_token):
    if per_token:
        return pl.BlockSpec((tm, D_MODEL), lambda i: (i, 0))
    return pl.BlockSpec((1, 1, D_MODEL), lambda i: (i * tm // rows_per_seq, 0, 0))


def _inproj(x, sh, sc, g, w, *, tm, rows_per_seq, per_token):
    n = x.shape[0]
    widths = (RWKV_PROJ, MLSTM_MAIN, SSD_MAIN, GATE_W)
    wtot = sum(widths)
    return pl.pallas_call(
        functools.partial(_inproj_kernel, per_token=per_token),
        grid=(n // tm,),
        in_specs=[pl.BlockSpec((tm, D_MODEL), lambda i: (i, 0)),
                  _mod_spec(tm, rows_per_seq, per_token), _mod_spec(tm, rows_per_seq, per_token),
                  pl.BlockSpec((1, D_MODEL), lambda i: (0, 0)),
                  pl.BlockSpec((D_MODEL, wtot), lambda i: (0, 0))],
        out_specs=[pl.BlockSpec((tm, w_), lambda i: (i, 0)) for w_ in widths],
        out_shape=[jax.ShapeDtypeStruct((n, w_), F32) for w_ in widths],
        compiler_params=_params(1),
        name="in_proj",
    )(x, sh, sc, g, w)


HIST0 = HIST - (CONV_WIDTH - 1)


def _causal_conv(ext_ref, s, u, w_ref, b_ref, rows, t_valid, new_buf_ref):
    ext_ref[s, pl.ds(HIST, rows), :] = u
    out = b_ref[...]
    for j in range(CONV_WIDTH):
        out = out + ext_ref[s, pl.ds(HIST0 + j, rows), :] * w_ref[pl.ds(j, 1), :]
    last = ext_ref[s, pl.ds(HIST0 + t_valid, CONV_WIDTH - 1), :]
    new_buf_ref[s] = last
    ext_ref[s, pl.ds(HIST0, CONV_WIDTH - 1), :] = last
    return out


def _mixer_specs(nseq, rows):
    chunk = lambda w, col=0: pl.BlockSpec((nseq, rows, w), lambda i, c: (i, c, col))
    full = lambda *s: pl.BlockSpec(s, lambda i, c: (0,) * len(s))
    per_seq = lambda *s: pl.BlockSpec((nseq,) + s, lambda i, c: (i,) + (0,) * len(s))
    return chunk, full, per_seq


def _row_mask(rows, t_valid):
    if t_valid == rows:
        return None
    return lax.broadcasted_iota(jnp.int32, (rows, 1), 0) < t_valid


def _mlstm_kernel(pm_ref, pg_ref, conv0_ref, c0_ref, n0_ref, m0_ref, cw_ref, cb_ref, gb_ref, ng_ref,
                  out_ref, conv_ref, c_ref, n_ref, m_ref, ext_ref, *, rows, t_valid, nseq):
    @pl.when(pl.program_id(1) == 0)
    def _():
        ext_ref[:, pl.ds(HIST0, CONV_WIDTH - 1), :] = conv0_ref[...]
        c_ref[...] = c0_ref[...]
        n_ref[...] = n0_ref[...]
        m_ref[...] = m0_ref[...]

    valid = _row_mask(rows, t_valid)
    tril = _tri(rows)
    tril_b = tril.astype(BF16)
    eye = lax.broadcasted_iota(jnp.int32, (rows, rows), 0) == lax.broadcasted_iota(jnp.int32, (rows, rows), 1)
    gb = gb_ref[...]

    seqs = []
    for s in range(nseq):
        pm = pm_ref[s]
        qk = _silu(_causal_conv(ext_ref, s, pm[:, :MLSTM_CONV_W], cw_ref, cb_ref, rows, t_valid, conv_ref))
        gates = pg_ref[s]
        ig = gates[:, :128] + gb[:, :128]
        fg = gates[:, 128:256] + gb[:, 128:256]
        logf = -_softplus(-fg)
        if valid is not None:
            ig = jnp.where(valid, ig, NEG_BIG)
            logf = jnp.where(valid, logf, 0.0)
        bcum = _cumsum_rows(logf, tril_b)
        m_prev = m_ref[s]
        b_end = bcum[rows - 1:rows, :]
        ws = b_end - bcum + ig
        m_new = jnp.maximum(b_end + m_prev, jnp.max(ws, axis=0, keepdims=True))
        m_ref[s] = m_new
        seqs.append(dict(q=qk[:, :MLSTM_W], k=qk[:, MLSTM_W:] * HEAD_DIM ** -0.5,
                         v=pm[:, MLSTM_CONV_W:MLSTM_CONV_W + MLSTM_W], o=pm[:, MLSTM_CONV_W + MLSTM_W:],
                         ig=ig, bcum=bcum, inter=bcum + m_prev, ws=jnp.exp(ws - m_new),
                         a_end=jnp.exp(b_end + m_prev - m_new)))

    chains = [(s, h) for s in range(nseq) for h in range(MLSTM_HEADS)]
    lanes = lambda c: slice(c[1] * HEAD_DIM, (c[1] + 1) * HEAD_DIM)
    col = lambda name, c: seqs[c[0]][name][:, c[1]:c[1] + 1]
    q = [seqs[c[0]]['q'][:, lanes(c)] for c in chains]
    k = [seqs[c[0]]['k'][:, lanes(c)] for c in chains]
    v = [seqs[c[0]]['v'][:, lanes(c)] for c in chains]
    c0 = [c_ref[c[0], c[1]] for c in chains]
    n0 = [n_ref[c[0], pl.ds(c[1], 1), :] for c in chains]
    qk_t = [_dot_nt(q_, k_) for q_, k_ in zip(q, k)]
    q_c = [_dot_nt(q_, c_) for q_, c_ in zip(q, c0)]
    c_add = [_dot_tn(col('ws', c) * v_, k_) for c, v_, k_ in zip(chains, v, k)]
    dmat = [jnp.where(tril, col('bcum', c) - _to_row(col('bcum', c), eye) + _to_row(col('ig', c), eye), NEG_BIG)
            for c in chains]
    m_t = [jnp.maximum(col('inter', c), jnp.max(d_, axis=1, keepdims=True)) for c, d_ in zip(chains, dmat)]
    sc = [g_ * jnp.exp(d_ - m_) for g_, d_, m_ in zip(qk_t, dmat, m_t)]
    s_v = [_dot(s_, v_) for s_, v_ in zip(sc, v)]
    ones = jnp.ones((rows, HEAD_DIM), BF16)
    ones_hh = jnp.ones((HEAD_DIM, HEAD_DIM), BF16)
    s_sum = [_mm(s_, ones, 'nn', b_exact=True) for s_ in sc]
    q_n = [_mm(q_, jnp.broadcast_to(n_, (HEAD_DIM, HEAD_DIM)), 'nt') for q_, n_ in zip(q, n0)]
    hh = []
    for i, c in enumerate(chains):
        a_in = jnp.exp(col('inter', c) - m_t[i])
        num = s_v[i] + a_in * q_c[i]
        den = s_sum[i] + a_in * q_n[i]
        hh.append(num / jnp.maximum(jnp.abs(den), jnp.exp(-m_t[i])))
        ae = col('a_end', c)
        c_ref[c[0], c[1]] = ae * c0[i] + c_add[i]
        n_ref[c[0], pl.ds(c[1], 1), :] = ae * n0[i] + jnp.sum(col('ws', c) * k[i], axis=0, keepdims=True)
    inv_n = 1.0 / HEAD_DIM
    hc = [h_ - _mm(h_, ones_hh, 'nn', b_exact=True) * inv_n for h_ in hh]
    var = [_mm(h_ * h_, ones_hh, 'nn', b_exact=True) * inv_n for h_ in hc]
    for i, c in enumerate(chains):
        hn = hc[i] * lax.rsqrt(var[i] + NORM_EPS)
        out_ref[c[0], :, lanes(c)] = hn * ng_ref[:, lanes(c)] * _sigmoid(seqs[c[0]]['o'][:, lanes(c)])


def _mlstm(pm, pg, conv0, c0, n0, m0, cw, cb, gb, ng, *, rows, t_valid, nseq):
    b, t, _ = pm.shape
    chunk, full, per_seq = _mixer_specs(nseq, rows)
    state_specs = [per_seq(CONV_WIDTH - 1, MLSTM_CONV_W), per_seq(MLSTM_HEADS, HEAD_DIM, HEAD_DIM),
                   per_seq(MLSTM_HEADS, HEAD_DIM), per_seq(1, 128)]
    return pl.pallas_call(
        functools.partial(_mlstm_kernel, rows=rows, t_valid=t_valid, nseq=nseq),
        grid=(b // nseq, t // rows),
        in_specs=[chunk(MLSTM_MAIN), chunk(256)] + state_specs
                 + [full(CONV_WIDTH, MLSTM_CONV_W), full(1, MLSTM_CONV_W), full(1, GATE_W), full(1, MLSTM_W)],
        out_specs=[chunk(MLSTM_W)] + state_specs,
        out_shape=[jax.ShapeDtypeStruct((b, t, MLSTM_W), F32),
                   jax.ShapeDtypeStruct((b, CONV_WIDTH - 1, MLSTM_CONV_W), F32),
                   jax.ShapeDtypeStruct((b, MLSTM_HEADS, HEAD_DIM, HEAD_DIM), F32),
                   jax.ShapeDtypeStruct((b, MLSTM_HEADS, HEAD_DIM), F32),
                   jax.ShapeDtypeStruct((b, 1, 128), F32)],
        scratch_shapes=[pltpu.VMEM((nseq, HIST + rows, MLSTM_CONV_W), F32)],
        compiler_params=_params(1, 1),
        name="mlstm_mix",
    )(pm, pg, conv0, c0, n0, m0, cw, cb, gb, ng)


def _ssd_kernel(ps_ref, dt_ref, conv0_ref, h0_ref, cw_ref, cb_ref, gb_ref, alog_ref, dskip_ref, ng_ref,
                out_ref, conv_ref, h_ref, ext_ref, y_ref, *, rows, t_valid, nseq):
    @pl.when(pl.program_id(1) == 0)
    def _():
        ext_ref[:, pl.ds(HIST0, CONV_WIDTH - 1), :] = conv0_ref[...]
        h_ref[...] = h0_ref[...]

    valid = _row_mask(rows, t_valid)
    tril = _tri(rows)
    tril_b = tril.astype(BF16)
    eye = lax.broadcasted_iota(jnp.int32, (rows, rows), 0) == lax.broadcasted_iota(jnp.int32, (rows, rows), 1)
    dskip = dskip_ref[...]
    neg_a = -jnp.exp(alog_ref[...])
    gs = SSD_GROUPS * SSD_STATE

    seqs = []
    for s in range(nseq):
        ps = ps_ref[s]
        xbc = _silu(_causal_conv(ext_ref, s, ps[:, SSD_W:], cw_ref, cb_ref, rows, t_valid, conv_ref))
        dt = _softplus(dt_ref[s] + gb_ref[:, 256:384])
        if valid is not None:
            dt = jnp.where(valid, dt, 0.0)
        seqs.append(dict(z=ps[:, :SSD_W], xs=xbc[:, :SSD_W], bm=xbc[:, SSD_W:SSD_W + gs], cm=xbc[:, SSD_W + gs:],
                         dt=dt, acum=_cumsum_rows(dt * neg_a, tril_b)))

    groups = [(s, g) for s in range(nseq) for g in range(SSD_GROUPS)]
    grp = lambda name, sg: seqs[sg[0]][name][:, sg[1] * SSD_STATE:(sg[1] + 1) * SSD_STATE]
    cb = {sg: _dot_nt(grp('cm', sg), grp('bm', sg)) for sg in groups}
    chains = [(s, hd) for s in range(nseq) for hd in range(SSD_HEADS)]
    lanes = lambda c: slice(c[1] * HEAD_DIM, (c[1] + 1) * HEAD_DIM)
    col = lambda name, c: seqs[c[0]][name][:, c[1]:c[1] + 1]
    group_of = lambda c: (c[0], c[1] // SSD_HEADS_PER_GROUP)
    x = [seqs[c[0]]['xs'][:, lanes(c)] for c in chains]
    h0 = [h_ref[c[0], c[1]] for c in chains]
    from_state = [_dot_nt(grp('cm', group_of(c)), h_) for c, h_ in zip(chains, h0)]
    h_add = [_dot_tn(jnp.exp(col('acum', c)[rows - 1:rows, :] - col('acum', c)) * col('dt', c) * x_,
                     grp('bm', group_of(c))) for c, x_ in zip(chains, x)]
    wmat = [jnp.exp(jnp.where(tril, col('acum', c) - _to_row(col('acum', c), eye), NEG_BIG))
            * cb[group_of(c)] * _to_row(col('dt', c), eye) for c in chains]
    y = [_dot(w_, x_) for w_, x_ in zip(wmat, x)]
    for i, c in enumerate(chains):
        a_col = col('acum', c)
        h_ref[c[0], c[1]] = jnp.exp(a_col[rows - 1:rows, :]) * h0[i] + h_add[i]
        y_ref[c[0], :, lanes(c)] = y[i] + jnp.exp(a_col) * from_state[i] + dskip[:, c[1]:c[1] + 1] * x[i]
    for s in range(nseq):
        out_ref[s] = _rms(y_ref[s] * _silu(seqs[s]['z']), ng_ref[...])


def _ssd(ps, pg, conv0, h0, cw, cb, gb, alog, dskip, ng, *, rows, t_valid, nseq):
    b, t, _ = ps.shape
    chunk, full, per_seq = _mixer_specs(nseq, rows)
    state_specs = [per_seq(CONV_WIDTH - 1, SSD_CONV_W), per_seq(SSD_HEADS, HEAD_DIM, SSD_STATE)]
    return pl.pallas_call(
        functools.partial(_ssd_kernel, rows=rows, t_valid=t_valid, nseq=nseq),
        grid=(b // nseq, t // rows),
        in_specs=[chunk(SSD_MAIN), chunk(128, col=2)] + state_specs
                 + [full(CONV_WIDTH, SSD_CONV_W), full(1, SSD_CONV_W), full(1, GATE_W),
                    full(1, 128), full(1, 128), full(1, SSD_W)],
        out_specs=[chunk(SSD_W)] + state_specs,
        out_shape=[jax.ShapeDtypeStruct((b, t, SSD_W), F32),
                   jax.ShapeDtypeStruct((b, CONV_WIDTH - 1, SSD_CONV_W), F32),
                   jax.ShapeDtypeStruct((b, SSD_HEADS, HEAD_DIM, SSD_STATE), F32)],
        scratch_shapes=[pltpu.VMEM((nseq, HIST + rows, SSD_CONV_W), F32), pltpu.VMEM((nseq, rows, SSD_W), F32)],
        compiler_params=_params(1, 1),
        name="ssd_mix",
    )(ps, pg, conv0, h0, cw, cb, gb, alog, dskip, ng)


def _rwkv_kernel(pr_ref, shift0_ref, s0_ref, mu_ref, w0_ref, wup_ref, a0_ref, aup_ref, gup_ref, kk_ref, ka_ref,
                 rk_ref, lng_ref, lnb_ref, bd_ref,
                 out_ref, shift_ref, s_ref, ext_ref, y_ref, *, rows, t_valid, nseq):
    @pl.when(pl.program_id(1) == 0)
    def _():
        ext_ref[:, pl.ds(HIST - 1, 1), :] = shift0_ref[...]
        s_ref[...] = s0_ref[...]

    tril = _tri(rows)
    stril = _tri(rows, strict=True)
    bd = bd_ref[...]
    head_sum = lambda t: _mm(t, bd, 'nn', b_exact=True)
    w3 = 3 * RWKV_W
    n = nseq * rows

    p3 = pr_ref[...]
    ext_ref[:, pl.ds(HIST, rows), :] = p3
    prev = ext_ref[:, pl.ds(HIST - 1, rows), :].reshape(n, RWKV_PROJ)
    last = ext_ref[:, pl.ds(HIST - 1 + t_valid, 1), :]
    shift_ref[...] = last
    ext_ref[:, pl.ds(HIST - 1, 1), :] = last
    p = p3.reshape(n, RWKV_PROJ)
    x = p + (prev - p) * mu_ref[...]
    r = x[:, :RWKV_W]
    k = x[:, RWKV_W:2 * RWKV_W]
    v = x[:, 2 * RWKV_W:w3]
    xw = x[:, w3:w3 + RWKV_DECAY_RANK]
    xa = x[:, w3 + RWKV_DECAY_RANK:w3 + RWKV_DECAY_RANK + RWKV_ICLR_RANK]
    xg = x[:, w3 + RWKV_DECAY_RANK + RWKV_ICLR_RANK:]
    w_log = -_softplus(-(w0_ref[...] + _dot(jnp.tanh(xw).astype(BF16), wup_ref[...]))) - 0.5
    logw = -jnp.exp(w_log)
    a = _sigmoid(a0_ref[...] + _dot(xa.astype(BF16), aup_ref[...]))
    g = _dot(_sigmoid(xg).astype(BF16), gup_ref[...])
    kk = k * kk_ref[...]
    kk = kk / jnp.maximum(jnp.sqrt(head_sum(kk * kk)), 1e-12)
    k = k * (1.0 + (a - 1.0) * ka_ref[...])
    ri = lax.broadcasted_iota(jnp.int32, (n, n), 0)
    ci = lax.broadcasted_iota(jnp.int32, (n, n), 1)
    same_seq = (ri // rows) == (ci // rows)
    if t_valid != rows:
        valid = (lax.broadcasted_iota(jnp.int32, (n, 1), 0) % rows) < t_valid
        logw = jnp.where(valid, logw, 0.0)
        kk = jnp.where(valid, kk, 0.0)
        k = jnp.where(valid, k, 0.0)

    cum = _cumsum_rows(logw, (same_seq & (ri >= ci)).astype(BF16))
    to_end = jnp.exp(_cumsum_rows(logw, (same_seq & (ri < ci)).astype(BF16)))
    p_inv = jnp.exp(-cum)
    a_t = -kk * jnp.exp(cum - logw)
    kka = kk * a
    b_t = kka * p_inv
    k_t = k * p_inv
    r_t = r * jnp.exp(cum)
    b_e = kka * to_end
    k_e = k * to_end
    p_end = jnp.exp(cum)

    chains = [(s, h) for s in range(nseq) for h in range(RWKV_HEADS)]
    blk = lambda t, c: t[c[0] * rows:(c[0] + 1) * rows, c[1] * HEAD_DIM:(c[1] + 1) * HEAD_DIM]
    s0 = [s_ref[c[0], c[1]] for c in chains]
    vh = [blk(v, c) for c in chains]
    ar = [jnp.concatenate([blk(a_t, c), blk(r_t, c)], axis=0) for c in chains]
    bk = [jnp.concatenate([blk(b_t, c), blk(k_t, c)], axis=0) for c in chains]
    mm = functools.partial(_mm, single=True)
    gram = [mm(x_, y_, 'nt') for x_, y_ in zip(ar, bk)]
    from_state = [mm(x_, y_, 'nt') for x_, y_ in zip(ar, s0)]
    n_mat = [jnp.where(stril, g_[:rows, :rows], 0.0) for g_ in gram]
    u = [f_[:rows] + mm(jnp.where(stril, g_[:rows, rows:], 0.0), v_, 'nn')
         for f_, g_, v_ in zip(from_state, gram, vh)]
    span = 1
    while span < rows:
        u = [u_ + mm(n_, u_, 'nn') for u_, n_ in zip(u, n_mat)]
        span *= 2
        if span < rows:
            n_mat = [mm(n_, n_, 'nn') for n_ in n_mat]
    y_u = [mm(jnp.where(tril, g_[rows:, :rows], 0.0), u_, 'nn') for g_, u_ in zip(gram, u)]
    y_v = [mm(jnp.where(tril, g_[rows:, rows:], 0.0), v_, 'nn') for g_, v_ in zip(gram, vh)]
    s_add = [mm(jnp.concatenate([u_, v_], axis=0), jnp.concatenate([blk(b_e, c), blk(k_e, c)], axis=0), 'tn')
             for u_, v_, c in zip(u, vh, chains)]
    for i, c in enumerate(chains):
        lanes = slice(c[1] * HEAD_DIM, (c[1] + 1) * HEAD_DIM)
        y_ref[c[0], :, lanes] = from_state[i][rows:] + y_u[i] + y_v[i]
        end_row = (c[0] + 1) * rows - 1
        s_ref[c[0], c[1]] = p_end[end_row:end_row + 1, lanes] * s0[i] + s_add[i]

    y = y_ref[...].reshape(n, RWKV_W)
    inv_n = 1.0 / HEAD_DIM
    yc = y - head_sum(y) * inv_n
    yn = yc * lax.rsqrt(head_sum(yc * yc) * inv_n + RWKV_LN_EPS)
    bonus = head_sum(r * k * rk_ref[...]) * v
    out_ref[...] = ((yn * lng_ref[...] + lnb_ref[...] + bonus) * g).reshape(nseq, rows, RWKV_W)


def _rwkv(pr, shift0, s0, mu, w0, wup, a0, aup, gup, kkp, kap, rk, lng, lnb, bd, *, rows, t_valid, nseq):
    b, t, _ = pr.shape
    chunk, full, per_seq = _mixer_specs(nseq, rows)
    state_specs = [per_seq(1, RWKV_PROJ), per_seq(RWKV_HEADS, HEAD_DIM, HEAD_DIM)]
    return pl.pallas_call(
        functools.partial(_rwkv_kernel, rows=rows, t_valid=t_valid, nseq=nseq),
        grid=(b // nseq, t // rows),
        in_specs=[chunk(RWKV_PROJ)] + state_specs
                 + [full(1, RWKV_PROJ), full(1, RWKV_W), full(RWKV_DECAY_RANK, RWKV_W), full(1, RWKV_W),
                    full(RWKV_ICLR_RANK, RWKV_W), full(RWKV_GATE_RANK, RWKV_W), full(1, RWKV_W), full(1, RWKV_W),
                    full(1, RWKV_W), full(1, RWKV_W), full(1, RWKV_W), full(RWKV_W, RWKV_W)],
        out_specs=[chunk(RWKV_W)] + state_specs,
        out_shape=[jax.ShapeDtypeStruct((b, t, RWKV_W), F32),
                   jax.ShapeDtypeStruct((b, 1, RWKV_PROJ), F32),
                   jax.ShapeDtypeStruct((b, RWKV_HEADS, HEAD_DIM, HEAD_DIM), F32)],
        scratch_shapes=[pltpu.VMEM((nseq, HIST + rows, RWKV_PROJ), F32), pltpu.VMEM((nseq, rows, RWKV_W), F32)],
        compiler_params=_params(1, 1),
        name="rwkv_mix",
    )(pr, shift0, s0, mu, w0, wup, a0, aup, gup, kkp, kap, rk, lng, lnb, bd)


def _outproj_kernel(or_ref, om_ref, os_ref, x_ref, g1_ref, sh_ref, sc_ref, ng_ref, w_ref, xmid_ref, h2_ref,
                    *, per_token):
    pick = (lambda r: r[...]) if per_token else (lambda r: r[0])
    w = w_ref[...]
    mix = (_dot(or_ref[...].astype(BF16), w[:RWKV_W])
           + _dot(om_ref[...].astype(BF16), w[RWKV_W:RWKV_W + MLSTM_W])
           + _dot(os_ref[...].astype(BF16), w[RWKV_W + MLSTM_W:]))
    x = x_ref[...] + pick(g1_ref) * mix
    xmid_ref[...] = x
    h2_ref[...] = (_rms(x, ng_ref[...]) * (1.0 + pick(sc_ref)) + pick(sh_ref)).astype(BF16)


def _outproj(o_r, o_m, o_s, x, g1, sh, sc, ng, w, *, tm, rows_per_seq, per_token):
    n = x.shape[0]
    tok = lambda w_: pl.BlockSpec((tm, w_), lambda i: (i, 0))
    ms = _mod_spec(tm, rows_per_seq, per_token)
    return pl.pallas_call(
        functools.partial(_outproj_kernel, per_token=per_token),
        grid=(n // tm,),
        in_specs=[tok(RWKV_W), tok(MLSTM_W), tok(SSD_W), tok(D_MODEL), ms, ms, ms,
                  pl.BlockSpec((1, D_MODEL), lambda i: (0, 0)),
                  pl.BlockSpec((D_MODEL, D_MODEL), lambda i: (0, 0))],
        out_specs=[tok(D_MODEL), tok(D_MODEL)],
        out_shape=[jax.ShapeDtypeStruct((n, D_MODEL), F32), jax.ShapeDtypeStruct((n, D_MODEL), BF16)],
        compiler_params=_params(1),
        name="out_proj",
    )(o_r, o_m, o_s, x, g1, sh, sc, ng, w)


CAND_LEN = tuple(PEER_TOPK // (p + 1) for p in range(PEER_TOPK))
CAND_OFF = tuple(sum(CAND_LEN[:p]) for p in range(PEER_TOPK))
CAND_ROWS = -(-sum(CAND_LEN) // 8) * 8


def _over_sublanes(r8, op):
    for shift in (4, 2, 1):
        r8 = op(r8, pltpu.roll(r8, shift, axis=0))
    return r8


def _row_iota(tiles, n):
    shape = (tiles, 8, n)
    return (lax.broadcasted_iota(jnp.int32, shape, 0) * 8 + lax.broadcasted_iota(jnp.int32, shape, 1)).astype(F32)


def _top1(x3, iota3):
    m = _over_sublanes(jnp.max(x3, axis=0), jnp.maximum)
    idx = _over_sublanes(jnp.min(jnp.where(x3 == m[None], iota3, float(8 * x3.shape[0])), axis=0), jnp.minimum)
    return m, idx


def _gelu_tanh(x):
    c1 = math.sqrt(2.0 / math.pi)
    half = 0.5 * x
    return half * jnp.tanh(x * (c1 + (c1 * 0.044715) * (x * x))) + half


def _peer_kernel(h2c_ref, h2n_ref, wq_ref, sk_ref, u_ref, v_ref, xmid_ref, g2_ref, fg_ref, o_ref,
                 acc_ref, act_ref, gm_ref, cand_ref, eid_ref, et_ref, gt_ref, a_ref, b_ref, g_ref, tmp_ref,
                 *, tm, per_token, final_norm):
    j = pl.program_id(1)

    @pl.when((pl.program_id(0) == 0) & (j == 0))
    def _():
        acc_ref[...] = jnp.zeros_like(acc_ref)
        act_ref[...] = jnp.zeros_like(act_ref)
        gm_ref[...] = jnp.zeros_like(gm_ref)

    q = _dot(h2n_ref[...], wq_ref[...]).astype(BF16)
    s12 = [_dot_nt(sk_ref[c], q[:, c * PEER_HALF:(c + 1) * PEER_HALF]) for c in range(2)]
    acc_ref[...] += _dot(act_ref[...], v_ref[...])
    s = _dot_nt(h2c_ref[...], u_ref[...])

    rows_per_block = u_ref.shape[0] // PEER_KEYS
    for r in range(rows_per_block):
        lanes = slice(r * PEER_KEYS, (r + 1) * PEER_KEYS)
        act_ref[:, lanes] = _gelu_tanh(s[:, lanes].astype(BF16)) * gm_ref[j * rows_per_block + r]

    key_iota = _row_iota(PEER_KEYS // 8, tm)
    s12 = [s_.reshape(PEER_KEYS // 8, 8, tm) for s_ in s12]
    vals = ([], [])
    idxs = ([], [])
    for _ in range(PEER_TOPK):
        for c in range(2):
            m, idx = _top1(s12[c], key_iota)
            s12[c] = jnp.where(key_iota == idx[None], -jnp.inf, s12[c])
            vals[c].append(m[0:1])
            idxs[c].append(idx[0:1])
    v1, v2 = (jnp.concatenate(v_, axis=0) for v_ in vals)
    i1, i2 = (jnp.concatenate(i_, axis=0) for i_ in idxs)

    cand_ref[...] = jnp.full(cand_ref.shape, -jnp.inf, F32)
    eid_ref[...] = jnp.zeros(eid_ref.shape, F32)
    for p in range(PEER_TOPK):
        cand_ref[pl.ds(CAND_OFF[p], CAND_LEN[p]), :] = v1[p:p + 1, :] + v2[:CAND_LEN[p], :]
        eid_ref[pl.ds(CAND_OFF[p], CAND_LEN[p]), :] = i1[p:p + 1, :] * float(PEER_KEYS) + i2[:CAND_LEN[p], :]
    cand_iota = _row_iota(CAND_ROWS // 8, tm)
    cand = cand_ref[...].reshape(CAND_ROWS // 8, 8, tm)
    eid = eid_ref[...].reshape(CAND_ROWS // 8, 8, tm)
    tops, picks = [], []
    for _ in range(PEER_TOPK):
        m, pos = _top1(cand, cand_iota)
        hit = cand_iota == pos[None]
        cand = jnp.where(hit, -jnp.inf, cand)
        tops.append(m[0:1])
        picks.append(_over_sublanes(jnp.sum(jnp.where(hit, eid, 0.0), axis=0), jnp.add)[0:1])
    top = jnp.concatenate(tops, axis=0)
    e = jnp.exp(top - top[0:1, :])
    row0 = pl.multiple_of(j * PEER_TOPK, PEER_TOPK)
    et_ref[pl.ds(row0, PEER_TOPK), :] = jnp.concatenate(picks, axis=0)
    gt_ref[pl.ds(row0, PEER_TOPK), :] = e / jnp.sum(e, axis=0, keepdims=True)

    @pl.when(j == 0)
    def _():
        g2 = g2_ref[...] if per_token else g2_ref[0]
        x = xmid_ref[...] + g2 * acc_ref[...]
        o_ref[...] = _rms(x, fg_ref[...]) if final_norm else x
        acc_ref[...] = jnp.zeros_like(acc_ref)

    @pl.when(j == PEER_HEADS - 1)
    def _():
        picked = et_ref[...]
        key1 = jnp.floor(picked * (1.0 / PEER_KEYS))
        a_ref[...] = key1.T
        b_ref[...] = (picked - key1 * float(PEER_KEYS)).T
        g_ref[...] = gt_ref[...].T
        j_iota = lax.broadcasted_iota(jnp.int32, (PEER_KEYS, PEER_HEADS * PEER_TOPK), 0).astype(F32)

        def token_groups(gi, carry):
            for half in range(2):
                t0 = pl.multiple_of((gi * 2 + half) * TOKEN_GROUP, TOKEN_GROUP)
                base = half * TOKEN_GROUP * TOKEN_PITCH
                for tt in range(TOKEN_GROUP):
                    t = t0 + tt
                    sel1 = jnp.where(j_iota == a_ref[pl.ds(t, 1), :], 1.0, 0.0).astype(BF16)
                    sel2 = jnp.where(j_iota == b_ref[pl.ds(t, 1), :], g_ref[pl.ds(t, 1), :], 0.0).astype(BF16)
                    tmp_ref[pl.ds(base + tt * TOKEN_PITCH, PEER_KEYS), :] = _dot_nt(sel1, sel2)
            for half in range(2):
                t0 = pl.multiple_of((gi * 2 + half) * TOKEN_GROUP, TOKEN_GROUP)
                base = half * TOKEN_GROUP * TOKEN_PITCH
                for r in range(PEER_KEYS):
                    rows = tmp_ref[pl.ds(base + r, TOKEN_GROUP, stride=TOKEN_PITCH), :]
                    gm_ref[r, pl.ds(t0, TOKEN_GROUP), :] = rows.astype(BF16)
            return carry

        lax.fori_loop(0, tm // (2 * TOKEN_GROUP), token_groups, 0)


def _peer(h2, wq, sk, u, v, xmid, g2, fg, *, tm, rows_per_seq, per_token, final_norm):
    n = h2.shape[0]
    ntiles = n // tm
    eb = PEER_EXPERTS // PEER_HEADS
    npick = PEER_HEADS * PEER_TOPK
    clamp = lambda t: jnp.clip(t, 0, ntiles - 1)
    routed = lambda g, j: clamp(g)
    stage_a = lambda g, j: clamp(g - 1)
    done = lambda g, j: clamp(g - 2 + jnp.minimum(j, 1))
    if per_token:
        ms = pl.BlockSpec((tm, D_MODEL), lambda g, j: (done(g, j), 0))
    else:
        ms = pl.BlockSpec((1, 1, D_MODEL), lambda g, j: (done(g, j) * tm // rows_per_seq, 0, 0))
    return pl.pallas_call(
        functools.partial(_peer_kernel, tm=tm, per_token=per_token, final_norm=final_norm),
        grid=(ntiles + 2, PEER_HEADS),
        in_specs=[pl.BlockSpec((tm, D_MODEL), lambda g, j: (stage_a(g, j), 0)),
                  pl.BlockSpec((tm, D_MODEL), lambda g, j: (routed(g, j), 0)),
                  pl.BlockSpec((D_MODEL, PEER_QDIM), lambda g, j: (0, j)),
                  pl.BlockSpec((2, PEER_KEYS, PEER_HALF), lambda g, j: (0, 0, 0)),
                  pl.BlockSpec((eb, D_MODEL), lambda g, j: (j, 0)),
                  pl.BlockSpec((eb, D_MODEL), lambda g, j: ((j + PEER_HEADS - 1) % PEER_HEADS, 0)),
                  pl.BlockSpec((tm, D_MODEL), lambda g, j: (done(g, j), 0)),
                  ms,
                  pl.BlockSpec((1, D_MODEL), lambda g, j: (0, 0))],
        out_specs=pl.BlockSpec((tm, D_MODEL), lambda g, j: (done(g, j), 0)),
        out_shape=jax.ShapeDtypeStruct((n, D_MODEL), F32),
        scratch_shapes=[pltpu.VMEM((tm, D_MODEL), F32),
                        pltpu.VMEM((tm, eb), BF16),
                        pltpu.VMEM((PEER_KEYS, tm, PEER_KEYS), BF16),
                        pltpu.VMEM((CAND_ROWS, tm), F32), pltpu.VMEM((CAND_ROWS, tm), F32),
                        pltpu.VMEM((npick, tm), F32), pltpu.VMEM((npick, tm), F32),
                        pltpu.VMEM((tm, npick), F32), pltpu.VMEM((tm, npick), F32), pltpu.VMEM((tm, npick), F32),
                        pltpu.VMEM((2 * TOKEN_GROUP * TOKEN_PITCH, PEER_KEYS), F32)],
        compiler_params=_params(0, 2),
        name="peer",
    )(h2, h2, wq, sk, u, v, xmid, g2, fg)


TOKEN_TILE = 512
PEER_TILE = 256
SAMPLE_ROWS = 8
PROMPT_SEQS_PER_STEP = 4
SAMPLE_SEQS_PER_STEP = 8


def _lane_pad(vec, offset=0, width=128):
    return jnp.zeros((1, width), F32).at[0, offset:offset + vec.shape[0]].set(vec)


def _layer_weights(l, w):
    w_in = w['w_in'][l]
    o_m = RWKV_PROJ
    o_mg = o_m + MLSTM_MAIN
    o_s = o_mg + 2 * MLSTM_HEADS
    o_sg = o_s + SSD_MAIN
    gate_cols = jnp.zeros((D_MODEL, GATE_W), F32)
    gate_cols = gate_cols.at[:, 0:MLSTM_HEADS].set(w_in[:, o_mg:o_mg + MLSTM_HEADS])
    gate_cols = gate_cols.at[:, 128:128 + MLSTM_HEADS].set(w_in[:, o_mg + MLSTM_HEADS:o_s])
    gate_cols = gate_cols.at[:, 256:256 + SSD_HEADS].set(w_in[:, o_sg:o_sg + SSD_HEADS])
    w_cat = jnp.concatenate([w_in[:, :o_m], w_in[:, o_m:o_mg], w_in[:, o_s:o_sg], gate_cols], axis=1).astype(BF16)
    gate_bias = jnp.concatenate([_lane_pad(w['mlstm_i_b'][l]), _lane_pad(w['mlstm_f_b'][l]),
                                 _lane_pad(w['ssd_dt_bias'][l])], axis=1)
    row = lambda name: w[name][l].reshape(1, -1)
    return dict(
        w_cat=w_cat, gate_bias=gate_bias, w_out=w['w_out'][l].astype(BF16),
        norm1_g=row('norm1_g'), norm2_g=row('norm2_g'),
        rwkv=(row('rwkv_mu'), row('rwkv_w0'), w['rwkv_w_up'][l].astype(BF16), row('rwkv_a0'),
              w['rwkv_a_up'][l].astype(BF16), w['rwkv_g_up'][l].astype(BF16), row('rwkv_k_k'), row('rwkv_k_a'),
              row('rwkv_r_k'), row('rwkv_ln_g'), row('rwkv_ln_b')),
        mlstm=(w['mlstm_conv_w'][l], row('mlstm_conv_b'), gate_bias, row('mlstm_norm_g')),
        ssd=(w['ssd_conv_w'][l], row('ssd_conv_b'), gate_bias, _lane_pad(w['ssd_A_log'][l]),
             _lane_pad(w['ssd_D'][l]), row('ssd_norm_g')),
        wq=w['peer_wq'][l].astype(BF16), sk=w['peer_subkeys'][l].astype(BF16),
        u=w['peer_u'][l].astype(BF16), v=w['peer_v'][l].astype(BF16),
    )


def _run_layer(x, mods, state, lw, head_sum, final_g, *, batch, seq, rows, nseq, per_token, final_norm):
    sh1, sc1, g1, sh2, sc2, g2 = mods
    shift, wkv, mconv, m_c, m_n, m_m, sconv, sst = state
    tm = min(TOKEN_TILE, batch * seq)
    tok = dict(tm=tm, rows_per_seq=seq, per_token=per_token)
    p_r, p_m, p_s, p_g = _inproj(x, sh1, sc1, lw['norm1_g'], lw['w_cat'], **tok)

    t_pad = -(-seq // rows) * rows
    t_valid = seq if seq < rows else rows

    def seqs(a):
        a = a.reshape(batch, seq, a.shape[-1])
        return a if t_pad == seq else jnp.pad(a, ((0, 0), (0, t_pad - seq), (0, 0)))

    def toks(a):
        return a[:, :seq].reshape(batch * seq, a.shape[-1])

    mix = dict(rows=rows, t_valid=t_valid, nseq=nseq)
    o_r, shift, wkv = _rwkv(seqs(p_r), shift[:, None, :], wkv, *lw['rwkv'], head_sum, **mix)
    m_m = jnp.pad(m_m, ((0, 0), (0, 128 - MLSTM_HEADS)))[:, None, :]
    o_m, mconv, m_c, m_n, m_m = _mlstm(seqs(p_m), seqs(p_g), mconv, m_c, m_n, m_m, *lw['mlstm'], **mix)
    o_s, sconv, sst = _ssd(seqs(p_s), seqs(p_g), sconv, sst, *lw['ssd'], **mix)
    new_state = (shift[:, 0, :], wkv, mconv, m_c, m_n, m_m[:, 0, :MLSTM_HEADS], sconv, sst)

    x_mid, h2 = _outproj(toks(o_r), toks(o_m), toks(o_s), x, g1, sh2, sc2, lw['norm2_g'], lw['w_out'], **tok)
    x_new = _peer(h2, lw['wq'], lw['sk'], lw['u'], lw['v'], x_mid, g2, final_g, tm=min(PEER_TILE, batch * seq),
                  rows_per_seq=seq, per_token=per_token, final_norm=final_norm)
    return x_new, new_state


def kernel(x_prompt, x_sample, state_rwkv_shift, state_rwkv_wkv, state_mlstm_conv, state_mlstm_C, state_mlstm_n, state_mlstm_m, state_ssd_conv, state_ssd, c_prompt, c_sample, ada_w, ada_b, norm1_g, norm2_g, w_in, w_out, rwkv_mu, rwkv_w0, rwkv_w_up, rwkv_a0, rwkv_a_up, rwkv_g_up, rwkv_k_k, rwkv_k_a, rwkv_r_k, rwkv_ln_g, rwkv_ln_b, mlstm_conv_w, mlstm_conv_b, mlstm_i_b, mlstm_f_b, mlstm_norm_g, ssd_conv_w, ssd_conv_b, ssd_dt_bias, ssd_A_log, ssd_D, ssd_norm_g, peer_wq, peer_subkeys, peer_u, peer_v, final_g):
    weights = dict(norm1_g=norm1_g, norm2_g=norm2_g, w_in=w_in, w_out=w_out, rwkv_mu=rwkv_mu, rwkv_w0=rwkv_w0,
                   rwkv_w_up=rwkv_w_up, rwkv_a0=rwkv_a0, rwkv_a_up=rwkv_a_up, rwkv_g_up=rwkv_g_up,
                   rwkv_k_k=rwkv_k_k, rwkv_k_a=rwkv_k_a, rwkv_r_k=rwkv_r_k, rwkv_ln_g=rwkv_ln_g,
                   rwkv_ln_b=rwkv_ln_b, mlstm_conv_w=mlstm_conv_w, mlstm_conv_b=mlstm_conv_b,
                   mlstm_i_b=mlstm_i_b, mlstm_f_b=mlstm_f_b, mlstm_norm_g=mlstm_norm_g, ssd_conv_w=ssd_conv_w,
                   ssd_conv_b=ssd_conv_b, ssd_dt_bias=ssd_dt_bias, ssd_A_log=ssd_A_log, ssd_D=ssd_D,
                   ssd_norm_g=ssd_norm_g, peer_wq=peer_wq, peer_subkeys=peer_subkeys, peer_u=peer_u, peer_v=peer_v)
    nb, seq, d = x_prompt.shape
    nd, dseq, _ = x_sample.shape
    cache = (state_rwkv_shift, state_rwkv_wkv, state_mlstm_conv, state_mlstm_C, state_mlstm_n, state_mlstm_m,
             state_ssd_conv, state_ssd)

    mod = _ada_mod(jnp.concatenate([c_prompt, c_sample], axis=0), ada_w, ada_b)
    head_id = jnp.arange(RWKV_W) // HEAD_DIM
    head_sum = (head_id[:, None] == head_id[None, :]).astype(F32)
    fg = final_g.reshape(1, d)

    xp = x_prompt.reshape(nb * seq, d)
    xs = x_sample.reshape(nd * dseq, d)
    new_p, new_s = [], []
    for l in range(DEPTH):
        lw = _layer_weights(l, weights)
        last = l == DEPTH - 1
        mod_p = mod[l, :nb].reshape(nb, 6, 1, d)
        mods_p = tuple(mod_p[:, i] for i in range(6))
        mod_s = jnp.repeat(mod[l, nb:].reshape(nd, 6, d), dseq, axis=0)
        mods_s = tuple(mod_s[:, i] for i in range(6))
        zeros = tuple(jnp.zeros((nb,) + s.shape[2:], F32) for s in cache)
        xp, sp = _run_layer(xp, mods_p, zeros, lw, head_sum, fg, batch=nb, seq=seq, rows=CHUNK,
                            nseq=PROMPT_SEQS_PER_STEP, per_token=False, final_norm=last)
        xs, ss = _run_layer(xs, mods_s, tuple(s[l] for s in cache), lw, head_sum, fg, batch=nd, seq=dseq,
                            rows=SAMPLE_ROWS, nseq=SAMPLE_SEQS_PER_STEP, per_token=True, final_norm=last)
        new_p.append(sp)
        new_s.append(ss)

    def stk(states, i):
        return jnp.stack([st[i] for st in states])

    return ((xp.reshape(nb, seq, d), xs.reshape(nd, dseq, d))
            + tuple(stk(new_p, i) for i in range(8)) + tuple(stk(new_s, i) for i in range(8)))
```

```python
import functools
import math

import jax
import jax.numpy as jnp
from jax import lax
from jax.experimental import pallas as pl
from jax.experimental.pallas import tpu as pltpu

F32 = jnp.float32
BF16 = jnp.bfloat16

D_MODEL = 1024
DEPTH = 4
HEAD_DIM = 64
RWKV_W = 256
RWKV_HEADS = 4
RWKV_DECAY_RANK = 64
RWKV_ICLR_RANK = 64
RWKV_GATE_RANK = 128
RWKV_LN_EPS = 1e-5 * HEAD_DIM
RWKV_PROJ = 3 * RWKV_W + RWKV_DECAY_RANK + RWKV_ICLR_RANK + RWKV_GATE_RANK
MLSTM_W = 256
MLSTM_HEADS = 4
MLSTM_CONV_W = 2 * MLSTM_W
MLSTM_MAIN = MLSTM_CONV_W + 2 * MLSTM_W
SSD_W = 512
SSD_HEADS = 8
SSD_STATE = 128
SSD_GROUPS = 2
SSD_HEADS_PER_GROUP = SSD_HEADS // SSD_GROUPS
SSD_CONV_W = SSD_W + 2 * SSD_GROUPS * SSD_STATE
SSD_MAIN = SSD_W + SSD_CONV_W
CONV_WIDTH = 4
CHUNK = 64
GATE_W = 3 * 128
PEER_KEYS = 128
PEER_EXPERTS = PEER_KEYS * PEER_KEYS
PEER_HEADS = 8
PEER_TOPK = 16
PEER_QDIM = 256
PEER_HALF = 128
NORM_EPS = 1e-6
NEG_BIG = -1e30
TOKEN_GROUP = 16
TOKEN_PITCH = PEER_KEYS + 8

VMEM_LIMIT_BYTES = 56 * 1024 * 1024
HIST = 8


def _dot(a, b, prec=None):
    return jnp.dot(a, b, preferred_element_type=F32, precision=prec)


def _dot_nt(a, b, prec=None):
    return lax.dot_general(a, b, (((1,), (1,)), ((), ())), preferred_element_type=F32, precision=prec)


def _dot_tn(a, b, prec=None):
    return lax.dot_general(a, b, (((0,), (0,)), ((), ())), preferred_element_type=F32, precision=prec)


def _hi_lo(x):
    hi = x.astype(BF16)
    return hi, (x - hi.astype(F32)).astype(BF16)


def _mm(a, b, form, b_exact=False, single=False):
    dot = {'nn': _dot, 'nt': _dot_nt, 'tn': _dot_tn}[form]
    if single:
        return dot(a.astype(BF16), b.astype(BF16))
    ah, al = _hi_lo(a)
    a_axis = 0 if form == 'tn' else 1
    b_axis = 1 if form == 'nt' else 0
    if b_exact:
        bb = b.astype(BF16)
        a3 = jnp.concatenate([ah, al], axis=a_axis)
        b3 = jnp.concatenate([bb, bb], axis=b_axis)
    else:
        bh, bl = _hi_lo(b)
        a3 = jnp.concatenate([ah, ah, al], axis=a_axis)
        b3 = jnp.concatenate([bh, bl, bh], axis=b_axis)
    return dot(a3, b3)


def _cumsum_rows(x, tril_bf16):
    hi = x.astype(BF16)
    r1 = x - hi.astype(F32)
    mid = r1.astype(BF16)
    lo = (r1 - mid.astype(F32)).astype(BF16)
    return _dot(jnp.concatenate([tril_bf16] * 3, axis=1), jnp.concatenate([hi, mid, lo], axis=0))


def _sigmoid(x):
    return 1.0 / (1.0 + jnp.exp(-x))


def _silu(x):
    return x * _sigmoid(x)


def _softplus(x):
    return jnp.maximum(x, 0.0) + jnp.log(1.0 + jnp.exp(-jnp.abs(x)))


def _tri(n, strict=False):
    r = lax.broadcasted_iota(jnp.int32, (n, n), 0)
    c = lax.broadcasted_iota(jnp.int32, (n, n), 1)
    return (r > c) if strict else (r >= c)


def _to_row(col, eye):
    return jnp.sum(jnp.where(eye, col, 0.0), axis=0, keepdims=True)


def _params(n_parallel, n_arbitrary=0):
    sem = ("parallel",) * n_parallel + ("arbitrary",) * n_arbitrary
    return pltpu.CompilerParams(dimension_semantics=sem, vmem_limit_bytes=VMEM_LIMIT_BYTES)


def _ada_kernel(c_ref, w_ref, b_ref, o_ref):
    c = _silu(c_ref[...]).astype(BF16)
    o_ref[0] = _dot(c, w_ref[0].astype(BF16)) + b_ref[0]


def _ada_mod(c_all, ada_w, ada_b):
    nb = c_all.shape[0]
    tn = 1536
    return pl.pallas_call(
        _ada_kernel,
        grid=(DEPTH, 6 * D_MODEL // tn),
        in_specs=[pl.BlockSpec((nb, D_MODEL), lambda l, j: (0, 0)),
                  pl.BlockSpec((1, D_MODEL, tn), lambda l, j: (l, 0, j)),
                  pl.BlockSpec((1, 1, tn), lambda l, j: (l, 0, j))],
        out_specs=pl.BlockSpec((1, nb, tn), lambda l, j: (l, 0, j)),
        out_shape=jax.ShapeDtypeStruct((DEPTH, nb, 6 * D_MODEL), F32),
        compiler_params=_params(2),
        name="ada_mod",
    )(c_all, ada_w, ada_b.reshape(DEPTH, 1, 6 * D_MODEL))


def _rms(x, g):
    return x * lax.rsqrt(jnp.mean(x * x, axis=-1, keepdims=True) + NORM_EPS) * g


def _inproj_kernel(x_ref, sh_ref, sc_ref, g_ref, w_ref, pr_ref, pm_ref, ps_ref, pg_ref, *, per_token):
    sh = sh_ref[...] if per_token else sh_ref[0]
    sc = sc_ref[...] if per_token else sc_ref[0]
    h = _rms(x_ref[...], g_ref[...]) * (1.0 + sc) + sh
    p = _dot(h.astype(BF16), w_ref[...])
    o = 0
    for ref, w in ((pr_ref, RWKV_PROJ), (pm_ref, MLSTM_MAIN), (ps_ref, SSD_MAIN), (pg_ref, GATE_W)):
        ref[...] = p[:, o:o + w]
        o += w


def _mod_spec(tm, rows_per_seq, per_token):
    if per_token:
        return pl.BlockSpec((tm, D_MODEL), lambda i: (i, 0))
    return pl.BlockSpec((1, 1, D_MODEL), lambda i: (i * tm // rows_per_seq, 0, 0))


def _inproj(x, sh, sc, g, w, *, tm, rows_per_seq, per_token):
    n = x.shape[0]
    widths = (RWKV_PROJ, MLSTM_MAIN, SSD_MAIN, GATE_W)
    wtot = sum(widths)
    return pl.pallas_call(
        functools.partial(_inproj_kernel, per_token=per_token),
        grid=(n // tm,),
        in_specs=[pl.BlockSpec((tm, D_MODEL), lambda i: (i, 0)),
                  _mod_spec(tm, rows_per_seq, per_token), _mod_spec(tm, rows_per_seq, per_token),
                  pl.BlockSpec((1, D_MODEL), lambda i: (0, 0)),
                  pl.BlockSpec((D_MODEL, wtot), lambda i: (0, 0))],
        out_specs=[pl.BlockSpec((tm, w_), lambda i: (i, 0)) for w_ in widths],
        out_shape=[jax.ShapeDtypeStruct((n, w_), F32) for w_ in widths],
        compiler_params=_params(1),
        name="in_proj",
    )(x, sh, sc, g, w)


HIST0 = HIST - (CONV_WIDTH - 1)


def _causal_conv(ext_ref, s, u, w_ref, b_ref, rows, t_valid, new_buf_ref):
    ext_ref[s, pl.ds(HIST, rows), :] = u
    out = b_ref[...]
    for j in range(CONV_WIDTH):
        out = out + ext_ref[s, pl.ds(HIST0 + j, rows), :] * w_ref[pl.ds(j, 1), :]
    last = ext_ref[s, pl.ds(HIST0 + t_valid, CONV_WIDTH - 1), :]
    new_buf_ref[s] = last
    ext_ref[s, pl.ds(HIST0, CONV_WIDTH - 1), :] = last
    return out


def _mixer_specs(nseq, rows):
    chunk = lambda w, col=0: pl.BlockSpec((nseq, rows, w), lambda i, c: (i, c, col))
    full = lambda *s: pl.BlockSpec(s, lambda i, c: (0,) * len(s))
    per_seq = lambda *s: pl.BlockSpec((nseq,) + s, lambda i, c: (i,) + (0,) * len(s))
    return chunk, full, per_seq


def _row_mask(rows, t_valid):
    if t_valid == rows:
        return None
    return lax.broadcasted_iota(jnp.int32, (rows, 1), 0) < t_valid


def _mlstm_kernel(pm_ref, pg_ref, conv0_ref, c0_ref, n0_ref, m0_ref, cw_ref, cb_ref, gb_ref, ng_ref,
                  out_ref, conv_ref, c_ref, n_ref, m_ref, ext_ref, *, rows, t_valid, nseq):
    @pl.when(pl.program_id(1) == 0)
    def _():
        ext_ref[:, pl.ds(HIST0, CONV_WIDTH - 1), :] = conv0_ref[...]
        c_ref[...] = c0_ref[...]
        n_ref[...] = n0_ref[...]
        m_ref[...] = m0_ref[...]

    valid = _row_mask(rows, t_valid)
    tril = _tri(rows)
    tril_b = tril.astype(BF16)
    eye = lax.broadcasted_iota(jnp.int32, (rows, rows), 0) == lax.broadcasted_iota(jnp.int32, (rows, rows), 1)
    gb = gb_ref[...]

    seqs = []
    for s in range(nseq):
        pm = pm_ref[s]
        qk = _silu(_causal_conv(ext_ref, s, pm[:, :MLSTM_CONV_W], cw_ref, cb_ref, rows, t_valid, conv_ref))
        gates = pg_ref[s]
        ig = gates[:, :128] + gb[:, :128]
        fg = gates[:, 128:256] + gb[:, 128:256]
        logf = -_softplus(-fg)
        if valid is not None:
            ig = jnp.where(valid, ig, NEG_BIG)
            logf = jnp.where(valid, logf, 0.0)
        bcum = _cumsum_rows(logf, tril_b)
        m_prev = m_ref[s]
        b_end = bcum[rows - 1:rows, :]
        ws = b_end - bcum + ig
        m_new = jnp.maximum(b_end + m_prev, jnp.max(ws, axis=0, keepdims=True))
        m_ref[s] = m_new
        seqs.append(dict(q=qk[:, :MLSTM_W], k=qk[:, MLSTM_W:] * HEAD_DIM ** -0.5,
                         v=pm[:, MLSTM_CONV_W:MLSTM_CONV_W + MLSTM_W], o=pm[:, MLSTM_CONV_W + MLSTM_W:],
                         ig=ig, bcum=bcum, inter=bcum + m_prev, ws=jnp.exp(ws - m_new),
                         a_end=jnp.exp(b_end + m_prev - m_new)))

    chains = [(s, h) for s in range(nseq) for h in range(MLSTM_HEADS)]
    lanes = lambda c: slice(c[1] * HEAD_DIM, (c[1] + 1) * HEAD_DIM)
    col = lambda name, c: seqs[c[0]][name][:, c[1]:c[1] + 1]
    q = [seqs[c[0]]['q'][:, lanes(c)] for c in chains]
    k = [seqs[c[0]]['k'][:, lanes(c)] for c in chains]
    v = [seqs[c[0]]['v'][:, lanes(c)] for c in chains]
    c0 = [c_ref[c[0], c[1]] for c in chains]
    n0 = [n_ref[c[0], pl.ds(c[1], 1), :] for c in chains]
    qk_t = [_dot_nt(q_, k_) for q_, k_ in zip(q, k)]
    q_c = [_dot_nt(q_, c_) for q_, c_ in zip(q, c0)]
    c_add = [_dot_tn(col('ws', c) * v_, k_) for c, v_, k_ in zip(chains, v, k)]
    dmat = [jnp.where(tril, col('bcum', c) - _to_row(col('bcum', c), eye) + _to_row(col('ig', c), eye), NEG_BIG)
            for c in chains]
    m_t = [jnp.maximum(col('inter', c), jnp.max(d_, axis=1, keepdims=True)) for c, d_ in zip(chains, dmat)]
    sc = [g_ * jnp.exp(d_ - m_) for g_, d_, m_ in zip(qk_t, dmat, m_t)]
    s_v = [_dot(s_, v_) for s_, v_ in zip(sc, v)]
    ones = jnp.ones((rows, HEAD_DIM), BF16)
    ones_hh = jnp.ones((HEAD_DIM, HEAD_DIM), BF16)
    s_sum = [_mm(s_, ones, 'nn', b_exact=True) for s_ in sc]
    q_n = [_mm(q_, jnp.broadcast_to(n_, (HEAD_DIM, HEAD_DIM)), 'nt') for q_, n_ in zip(q, n0)]
    hh = []
    for i, c in enumerate(chains):
        a_in = jnp.exp(col('inter', c) - m_t[i])
        num = s_v[i] + a_in * q_c[i]
        den = s_sum[i] + a_in * q_n[i]
        hh.append(num / jnp.maximum(jnp.abs(den), jnp.exp(-m_t[i])))
        ae = col('a_end', c)
        c_ref[c[0], c[1]] = ae * c0[i] + c_add[i]
        n_ref[c[0], pl.ds(c[1], 1), :] = ae * n0[i] + jnp.sum(col('ws', c) * k[i], axis=0, keepdims=True)
    inv_n = 1.0 / HEAD_DIM
    hc = [h_ - _mm(h_, ones_hh, 'nn', b_exact=True) * inv_n for h_ in hh]
    var = [_mm(h_ * h_, ones_hh, 'nn', b_exact=True) * inv_n for h_ in hc]
    for i, c in enumerate(chains):
        hn = hc[i] * lax.rsqrt(var[i] + NORM_EPS)
        out_ref[c[0], :, lanes(c)] = hn * ng_ref[:, lanes(c)] * _sigmoid(seqs[c[0]]['o'][:, lanes(c)])


def _mlstm(pm, pg, conv0, c0, n0, m0, cw, cb, gb, ng, *, rows, t_valid, nseq):
    b, t, _ = pm.shape
    chunk, full, per_seq = _mixer_specs(nseq, rows)
    state_specs = [per_seq(CONV_WIDTH - 1, MLSTM_CONV_W), per_seq(MLSTM_HEADS, HEAD_DIM, HEAD_DIM),
                   per_seq(MLSTM_HEADS, HEAD_DIM), per_seq(1, 128)]
    return pl.pallas_call(
        functools.partial(_mlstm_kernel, rows=rows, t_valid=t_valid, nseq=nseq),
        grid=(b // nseq, t // rows),
        in_specs=[chunk(MLSTM_MAIN), chunk(256)] + state_specs
                 + [full(CONV_WIDTH, MLSTM_CONV_W), full(1, MLSTM_CONV_W), full(1, GATE_W), full(1, MLSTM_W)],
        out_specs=[chunk(MLSTM_W)] + state_specs,
        out_shape=[jax.ShapeDtypeStruct((b, t, MLSTM_W), F32),
                   jax.ShapeDtypeStruct((b, CONV_WIDTH - 1, MLSTM_CONV_W), F32),
                   jax.ShapeDtypeStruct((b, MLSTM_HEADS, HEAD_DIM, HEAD_DIM), F32),
                   jax.ShapeDtypeStruct((b, MLSTM_HEADS, HEAD_DIM), F32),
                   jax.ShapeDtypeStruct((b, 1, 128), F32)],
        scratch_shapes=[pltpu.VMEM((nseq, HIST + rows, MLSTM_CONV_W), F32)],
        compiler_params=_params(1, 1),
        name="mlstm_mix",
    )(pm, pg, conv0, c0, n0, m0, cw, cb, gb, ng)


def _ssd_kernel(ps_ref, dt_ref, conv0_ref, h0_ref, cw_ref, cb_ref, gb_ref, alog_ref, dskip_ref, ng_ref,
                out_ref, conv_ref, h_ref, ext_ref, y_ref, *, rows, t_valid, nseq):
    @pl.when(pl.program_id(1) == 0)
    def _():
        ext_ref[:, pl.ds(HIST0, CONV_WIDTH - 1), :] = conv0_ref[...]
        h_ref[...] = h0_ref[...]

    valid = _row_mask(rows, t_valid)
    tril = _tri(rows)
    tril_b = tril.astype(BF16)
    eye = lax.broadcasted_iota(jnp.int32, (rows, rows), 0) == lax.broadcasted_iota(jnp.int32, (rows, rows), 1)
    dskip = dskip_ref[...]
    neg_a = -jnp.exp(alog_ref[...])
    gs = SSD_GROUPS * SSD_STATE

    seqs = []
    for s in range(nseq):
        ps = ps_ref[s]
        xbc = _silu(_causal_conv(ext_ref, s, ps[:, SSD_W:], cw_ref, cb_ref, rows, t_valid, conv_ref))
        dt = _softplus(dt_ref[s] + gb_ref[:, 256:384])
        if valid is not None:
            dt = jnp.where(valid, dt, 0.0)
        seqs.append(dict(z=ps[:, :SSD_W], xs=xbc[:, :SSD_W], bm=xbc[:, SSD_W:SSD_W + gs], cm=xbc[:, SSD_W + gs:],
                         dt=dt, acum=_cumsum_rows(dt * neg_a, tril_b)))

    groups = [(s, g) for s in range(nseq) for g in range(SSD_GROUPS)]
    grp = lambda name, sg: seqs[sg[0]][name][:, sg[1] * SSD_STATE:(sg[1] + 1) * SSD_STATE]
    cb = {sg: _dot_nt(grp('cm', sg), grp('bm', sg)) for sg in groups}
    chains = [(s, hd) for s in range(nseq) for hd in range(SSD_HEADS)]
    lanes = lambda c: slice(c[1] * HEAD_DIM, (c[1] + 1) * HEAD_DIM)
    col = lambda name, c: seqs[c[0]][name][:, c[1]:c[1] + 1]
    group_of = lambda c: (c[0], c[1] // SSD_HEADS_PER_GROUP)
    x = [seqs[c[0]]['xs'][:, lanes(c)] for c in chains]
    h0 = [h_ref[c[0], c[1]] for c in chains]
    from_state = [_dot_nt(grp('cm', group_of(c)), h_) for c, h_ in zip(chains, h0)]
    h_add = [_dot_tn(jnp.exp(col('acum', c)[rows - 1:rows, :] - col('acum', c)) * col('dt', c) * x_,
                     grp('bm', group_of(c))) for c, x_ in zip(chains, x)]
    wmat = [jnp.exp(jnp.where(tril, col('acum', c) - _to_row(col('acum', c), eye), NEG_BIG))
            * cb[group_of(c)] * _to_row(col('dt', c), eye) for c in chains]
    y = [_dot(w_, x_) for w_, x_ in zip(wmat, x)]
    for i, c in enumerate(chains):
        a_col = col('acum', c)
        h_ref[c[0], c[1]] = jnp.exp(a_col[rows - 1:rows, :]) * h0[i] + h_add[i]
        y_ref[c[0], :, lanes(c)] = y[i] + jnp.exp(a_col) * from_state[i] + dskip[:, c[1]:c[1] + 1] * x[i]
    for s in range(nseq):
        out_ref[s] = _rms(y_ref[s] * _silu(seqs[s]['z']), ng_ref[...])


def _ssd(ps, pg, conv0, h0, cw, cb, gb, alog, dskip, ng, *, rows, t_valid, nseq):
    b, t, _ = ps.shape
    chunk, full, per_seq = _mixer_specs(nseq, rows)
    state_specs = [per_seq(CONV_WIDTH - 1, SSD_CONV_W), per_seq(SSD_HEADS, HEAD_DIM, SSD_STATE)]
    return pl.pallas_call(
        functools.partial(_ssd_kernel, rows=rows, t_valid=t_valid, nseq=nseq),
        grid=(b // nseq, t // rows),
        in_specs=[chunk(SSD_MAIN), chunk(128, col=2)] + state_specs
                 + [full(CONV_WIDTH, SSD_CONV_W), full(1, SSD_CONV_W), full(1, GATE_W),
                    full(1, 128), full(1, 128), full(1, SSD_W)],
        out_specs=[chunk(SSD_W)] + state_specs,
        out_shape=[jax.ShapeDtypeStruct((b, t, SSD_W), F32),
                   jax.ShapeDtypeStruct((b, CONV_WIDTH - 1, SSD_CONV_W), F32),
                   jax.ShapeDtypeStruct((b, SSD_HEADS, HEAD_DIM, SSD_STATE), F32)],
        scratch_shapes=[pltpu.VMEM((nseq, HIST + rows, SSD_CONV_W), F32), pltpu.VMEM((nseq, rows, SSD_W), F32)],
        compiler_params=_params(1, 1),
        name="ssd_mix",
    )(ps, pg, conv0, h0, cw, cb, gb, alog, dskip, ng)


def _rwkv_kernel(pr_ref, shift0_ref, s0_ref, mu_ref, w0_ref, wup_ref, a0_ref, aup_ref, gup_ref, kk_ref, ka_ref,
                 rk_ref, lng_ref, lnb_ref, bd_ref,
                 out_ref, shift_ref, s_ref, ext_ref, y_ref, *, rows, t_valid, nseq):
    @pl.when(pl.program_id(1) == 0)
    def _():
        ext_ref[:, pl.ds(HIST - 1, 1), :] = shift0_ref[...]
        s_ref[...] = s0_ref[...]

    tril = _tri(rows)
    stril = _tri(rows, strict=True)
    bd = bd_ref[...]
    head_sum = lambda t: _mm(t, bd, 'nn', b_exact=True)
    w3 = 3 * RWKV_W
    n = nseq * rows

    p3 = pr_ref[...]
    ext_ref[:, pl.ds(HIST, rows), :] = p3
    prev = ext_ref[:, pl.ds(HIST - 1, rows), :].reshape(n, RWKV_PROJ)
    last = ext_ref[:, pl.ds(HIST - 1 + t_valid, 1), :]
    shift_ref[...] = last
    ext_ref[:, pl.ds(HIST - 1, 1), :] = last
    p = p3.reshape(n, RWKV_PROJ)
    x = p + (prev - p) * mu_ref[...]
    r = x[:, :RWKV_W]
    k = x[:, RWKV_W:2 * RWKV_W]
    v = x[:, 2 * RWKV_W:w3]
    xw = x[:, w3:w3 + RWKV_DECAY_RANK]
    xa = x[:, w3 + RWKV_DECAY_RANK:w3 + RWKV_DECAY_RANK + RWKV_ICLR_RANK]
    xg = x[:, w3 + RWKV_DECAY_RANK + RWKV_ICLR_RANK:]
    w_log = -_softplus(-(w0_ref[...] + _dot(jnp.tanh(xw).astype(BF16), wup_ref[...]))) - 0.5
    logw = -jnp.exp(w_log)
    a = _sigmoid(a0_ref[...] + _dot(xa.astype(BF16), aup_ref[...]))
    g = _dot(_sigmoid(xg).astype(BF16), gup_ref[...])
    kk = k * kk_ref[...]
    kk = kk / jnp.maximum(jnp.sqrt(head_sum(kk * kk)), 1e-12)
    k = k * (1.0 + (a - 1.0) * ka_ref[...])
    ri = lax.broadcasted_iota(jnp.int32, (n, n), 0)
    ci = lax.broadcasted_iota(jnp.int32, (n, n), 1)
    same_seq = (ri // rows) == (ci // rows)
    if t_valid != rows:
        valid = (lax.broadcasted_iota(jnp.int32, (n, 1), 0) % rows) < t_valid
        logw = jnp.where(valid, logw, 0.0)
        kk = jnp.where(valid, kk, 0.0)
        k = jnp.where(valid, k, 0.0)

    cum = _cumsum_rows(logw, (same_seq & (ri >= ci)).astype(BF16))
    to_end = jnp.exp(_cumsum_rows(logw, (same_seq & (ri < ci)).astype(BF16)))
    p_inv = jnp.exp(-cum)
    a_t = -kk * jnp.exp(cum - logw)
    kka = kk * a
    b_t = kka * p_inv
    k_t = k * p_inv
    r_t = r * jnp.exp(cum)
    b_e = kka * to_end
    k_e = k * to_end
    p_end = jnp.exp(cum)

    chains = [(s, h) for s in range(nseq) for h in range(RWKV_HEADS)]
    blk = lambda t, c: t[c[0] * rows:(c[0] + 1) * rows, c[1] * HEAD_DIM:(c[1] + 1) * HEAD_DIM]
    s0 = [s_ref[c[0], c[1]] for c in chains]
    vh = [blk(v, c) for c in chains]
    ar = [jnp.concatenate([blk(a_t, c), blk(r_t, c)], axis=0) for c in chains]
    bk = [jnp.concatenate([blk(b_t, c), blk(k_t, c)], axis=0) for c in chains]
    mm = functools.partial(_mm, single=True)
    gram = [mm(x_, y_, 'nt') for x_, y_ in zip(ar, bk)]
    from_state = [mm(x_, y_, 'nt') for x_, y_ in zip(ar, s0)]
    n_mat = [jnp.where(stril, g_[:rows, :rows], 0.0) for g_ in gram]
    u = [f_[:rows] + mm(jnp.where(stril, g_[:rows, rows:], 0.0), v_, 'nn')
         for f_, g_, v_ in zip(from_state, gram, vh)]
    span = 1
    while span < rows:
        u = [u_ + mm(n_, u_, 'nn') for u_, n_ in zip(u, n_mat)]
        span *= 2
        if span < rows:
            n_mat = [mm(n_, n_, 'nn') for n_ in n_mat]
    y_u = [mm(jnp.where(tril, g_[rows:, :rows], 0.0), u_, 'nn') for g_, u_ in zip(gram, u)]
    y_v = [mm(jnp.where(tril, g_[rows:, rows:], 0.0), v_, 'nn') for g_, v_ in zip(gram, vh)]
    s_add = [mm(jnp.concatenate([u_, v_], axis=0), jnp.concatenate([blk(b_e, c), blk(k_e, c)], axis=0), 'tn')
             for u_, v_, c in zip(u, vh, chains)]
    for i, c in enumerate(chains):
        lanes = slice(c[1] * HEAD_DIM, (c[1] + 1) * HEAD_DIM)
        y_ref[c[0], :, lanes] = from_state[i][rows:] + y_u[i] + y_v[i]
        end_row = (c[0] + 1) * rows - 1
        s_ref[c[0], c[1]] = p_end[end_row:end_row + 1, lanes] * s0[i] + s_add[i]

    y = y_ref[...].reshape(n, RWKV_W)
    inv_n = 1.0 / HEAD_DIM
    yc = y - head_sum(y) * inv_n
    yn = yc * lax.rsqrt(head_sum(yc * yc) * inv_n + RWKV_LN_EPS)
    bonus = head_sum(r * k * rk_ref[...]) * v
    out_ref[...] = ((yn * lng_ref[...] + lnb_ref[...] + bonus) * g).reshape(nseq, rows, RWKV_W)


def _rwkv(pr, shift0, s0, mu, w0, wup, a0, aup, gup, kkp, kap, rk, lng, lnb, bd, *, rows, t_valid, nseq):
    b, t, _ = pr.shape
    chunk, full, per_seq = _mixer_specs(nseq, rows)
    state_specs = [per_seq(1, RWKV_PROJ), per_seq(RWKV_HEADS, HEAD_DIM, HEAD_DIM)]
    return pl.pallas_call(
        functools.partial(_rwkv_kernel, rows=rows, t_valid=t_valid, nseq=nseq),
        grid=(b // nseq, t // rows),
        in_specs=[chunk(RWKV_PROJ)] + state_specs
                 + [full(1, RWKV_PROJ), full(1, RWKV_W), full(RWKV_DECAY_RANK, RWKV_W), full(1, RWKV_W),
                    full(RWKV_ICLR_RANK, RWKV_W), full(RWKV_GATE_RANK, RWKV_W), full(1, RWKV_W), full(1, RWKV_W),
                    full(1, RWKV_W), full(1, RWKV_W), full(1, RWKV_W), full(RWKV_W, RWKV_W)],
        out_specs=[chunk(RWKV_W)] + state_specs,
        out_shape=[jax.ShapeDtypeStruct((b, t, RWKV_W), F32),
                   jax.ShapeDtypeStruct((b, 1, RWKV_PROJ), F32),
                   jax.ShapeDtypeStruct((b, RWKV_HEADS, HEAD_DIM, HEAD_DIM), F32)],
        scratch_shapes=[pltpu.VMEM((nseq, HIST + rows, RWKV_PROJ), F32), pltpu.VMEM((nseq, rows, RWKV_W), F32)],
        compiler_params=_params(1, 1),
        name="rwkv_mix",
    )(pr, shift0, s0, mu, w0, wup, a0, aup, gup, kkp, kap, rk, lng, lnb, bd)


def _outproj_kernel(or_ref, om_ref, os_ref, x_ref, g1_ref, sh_ref, sc_ref, ng_ref, w_ref, xmid_ref, h2_ref,
                    *, per_token):
    pick = (lambda r: r[...]) if per_token else (lambda r: r[0])
    w = w_ref[...]
    mix = (_dot(or_ref[...].astype(BF16), w[:RWKV_W])
           + _dot(om_ref[...].astype(BF16), w[RWKV_W:RWKV_W + MLSTM_W])
           + _dot(os_ref[...].astype(BF16), w[RWKV_W + MLSTM_W:]))
    x = x_ref[...] + pick(g1_ref) * mix
    xmid_ref[...] = x
    h2_ref[...] = (_rms(x, ng_ref[...]) * (1.0 + pick(sc_ref)) + pick(sh_ref)).astype(BF16)


def _outproj(o_r, o_m, o_s, x, g1, sh, sc, ng, w, *, tm, rows_per_seq, per_token):
    n = x.shape[0]
    tok = lambda w_: pl.BlockSpec((tm, w_), lambda i: (i, 0))
    ms = _mod_spec(tm, rows_per_seq, per_token)
    return pl.pallas_call(
        functools.partial(_outproj_kernel, per_token=per_token),
        grid=(n // tm,),
        in_specs=[tok(RWKV_W), tok(MLSTM_W), tok(SSD_W), tok(D_MODEL), ms, ms, ms,
                  pl.BlockSpec((1, D_MODEL), lambda i: (0, 0)),
                  pl.BlockSpec((D_MODEL, D_MODEL), lambda i: (0, 0))],
        out_specs=[tok(D_MODEL), tok(D_MODEL)],
        out_shape=[jax.ShapeDtypeStruct((n, D_MODEL), F32), jax.ShapeDtypeStruct((n, D_MODEL), BF16)],
        compiler_params=_params(1),
        name="out_proj",
    )(o_r, o_m, o_s, x, g1, sh, sc, ng, w)


CAND_LEN = tuple(PEER_TOPK // (p + 1) for p in range(PEER_TOPK))
CAND_OFF = tuple(sum(CAND_LEN[:p]) for p in range(PEER_TOPK))
CAND_ROWS = -(-sum(CAND_LEN) // 8) * 8


def _over_sublanes(r8, op):
    for shift in (4, 2, 1):
        r8 = op(r8, pltpu.roll(r8, shift, axis=0))
    return r8


def _row_iota(tiles, n):
    shape = (tiles, 8, n)
    return (lax.broadcasted_iota(jnp.int32, shape, 0) * 8 + lax.broadcasted_iota(jnp.int32, shape, 1)).astype(F32)


def _top1(x3, iota3):
    m = _over_sublanes(jnp.max(x3, axis=0), jnp.maximum)
    idx = _over_sublanes(jnp.min(jnp.where(x3 == m[None], iota3, float(8 * x3.shape[0])), axis=0), jnp.minimum)
    return m, idx


def _gelu_tanh(x):
    c1 = math.sqrt(2.0 / math.pi)
    half = 0.5 * x
    return half * jnp.tanh(x * (c1 + (c1 * 0.044715) * (x * x))) + half


def _peer_kernel(h2c_ref, h2n_ref, wq_ref, sk_ref, u_ref, v_ref, xmid_ref, g2_ref, fg_ref, o_ref,
                 acc_ref, act_ref, gm_ref, cand_ref, eid_ref, et_ref, gt_ref, a_ref, b_ref, g_ref, tmp_ref,
                 *, tm, per_token, final_norm):
    j = pl.program_id(1)

    @pl.when((pl.program_id(0) == 0) & (j == 0))
    def _():
        acc_ref[...] = jnp.zeros_like(acc_ref)
        act_ref[...] = jnp.zeros_like(act_ref)
        gm_ref[...] = jnp.zeros_like(gm_ref)

    q = _dot(h2n_ref[...], wq_ref[...]).astype(BF16)
    s12 = [_dot_nt(sk_ref[c], q[:, c * PEER_HALF:(c + 1) * PEER_HALF]) for c in range(2)]
    acc_ref[...] += _dot(act_ref[...], v_ref[...])
    s = _dot_nt(h2c_ref[...], u_ref[...])

    rows_per_block = u_ref.shape[0] // PEER_KEYS
    for r in range(rows_per_block):
        lanes = slice(r * PEER_KEYS, (r + 1) * PEER_KEYS)
        act_ref[:, lanes] = _gelu_tanh(s[:, lanes].astype(BF16)) * gm_ref[j * rows_per_block + r]

    key_iota = _row_iota(PEER_KEYS // 8, tm)
    s12 = [s_.reshape(PEER_KEYS // 8, 8, tm) for s_ in s12]
    vals = ([], [])
    idxs = ([], [])
    for _ in range(PEER_TOPK):
        for c in range(2):
            m, idx = _top1(s12[c], key_iota)
            s12[c] = jnp.where(key_iota == idx[None], -jnp.inf, s12[c])
            vals[c].append(m[0:1])
            idxs[c].append(idx[0:1])
    v1, v2 = (jnp.concatenate(v_, axis=0) for v_ in vals)
    i1, i2 = (jnp.concatenate(i_, axis=0) for i_ in idxs)

    cand_ref[...] = jnp.full(cand_ref.shape, -jnp.inf, F32)
    eid_ref[...] = jnp.zeros(eid_ref.shape, F32)
    for p in range(PEER_TOPK):
        cand_ref[pl.ds(CAND_OFF[p], CAND_LEN[p]), :] = v1[p:p + 1, :] + v2[:CAND_LEN[p], :]
        eid_ref[pl.ds(CAND_OFF[p], CAND_LEN[p]), :] = i1[p:p + 1, :] * float(PEER_KEYS) + i2[:CAND_LEN[p], :]
    cand_iota = _row_iota(CAND_ROWS // 8, tm)
    cand = cand_ref[...].reshape(CAND_ROWS // 8, 8, tm)
    eid = eid_ref[...].reshape(CAND_ROWS // 8, 8, tm)
    tops, picks = [], []
    for _ in range(PEER_TOPK):
        m, pos = _top1(cand, cand_iota)
        hit = cand_iota == pos[None]
        cand = jnp.where(hit, -jnp.inf, cand)
        tops.append(m[0:1])
        picks.append(_over_sublanes(jnp.sum(jnp.where(hit, eid, 0.0), axis=0), jnp.add)[0:1])
    top = jnp.concatenate(tops, axis=0)
    e = jnp.exp(top - top[0:1, :])
    row0 = pl.multiple_of(j * PEER_TOPK, PEER_TOPK)
    et_ref[pl.ds(row0, PEER_TOPK), :] = jnp.concatenate(picks, axis=0)
    gt_ref[pl.ds(row0, PEER_TOPK), :] = e / jnp.sum(e, axis=0, keepdims=True)

    @pl.when(j == 0)
    def _():
        g2 = g2_ref[...] if per_token else g2_ref[0]
        x = xmid_ref[...] + g2 * acc_ref[...]
        o_ref[...] = _rms(x, fg_ref[...]) if final_norm else x
        acc_ref[...] = jnp.zeros_like(acc_ref)

    @pl.when(j == PEER_HEADS - 1)
    def _():
        picked = et_ref[...]
        key1 = jnp.floor(picked * (1.0 / PEER_KEYS))
        a_ref[...] = key1.T
        b_ref[...] = (picked - key1 * float(PEER_KEYS)).T
        g_ref[...] = gt_ref[...].T
        j_iota = lax.broadcasted_iota(jnp.int32, (PEER_KEYS, PEER_HEADS * PEER_TOPK), 0).astype(F32)

        def token_groups(gi, carry):
            for half in range(2):
                t0 = pl.multiple_of((gi * 2 + half) * TOKEN_GROUP, TOKEN_GROUP)
                base = half * TOKEN_GROUP * TOKEN_PITCH
                for tt in range(TOKEN_GROUP):
                    t = t0 + tt
                    sel1 = jnp.where(j_iota == a_ref[pl.ds(t, 1), :], 1.0, 0.0).astype(BF16)
                    sel2 = jnp.where(j_iota == b_ref[pl.ds(t, 1), :], g_ref[pl.ds(t, 1), :], 0.0).astype(BF16)
                    tmp_ref[pl.ds(base + tt * TOKEN_PITCH, PEER_KEYS), :] = _dot_nt(sel1, sel2)
            for half in range(2):
                t0 = pl.multiple_of((gi * 2 + half) * TOKEN_GROUP, TOKEN_GROUP)
                base = half * TOKEN_GROUP * TOKEN_PITCH
                for r in range(PEER_KEYS):
                    rows = tmp_ref[pl.ds(base + r, TOKEN_GROUP, stride=TOKEN_PITCH), :]
                    gm_ref[r, pl.ds(t0, TOKEN_GROUP), :] = rows.astype(BF16)
            return carry

        lax.fori_loop(0, tm // (2 * TOKEN_GROUP), token_groups, 0)


def _peer(h2, wq, sk, u, v, xmid, g2, fg, *, layer, tm, rows_per_seq, per_token, final_norm):
    n = h2.shape[0]
    ntiles = n // tm
    eb = PEER_EXPERTS // PEER_HEADS
    npick = PEER_HEADS * PEER_TOPK
    clamp = lambda t: jnp.clip(t, 0, ntiles - 1)
    routed = lambda g, j: clamp(g)
    stage_a = lambda g, j: clamp(g - 1)
    done = lambda g, j: clamp(g - 2 + jnp.minimum(j, 1))
    if per_token:
        ms = pl.BlockSpec((tm, D_MODEL), lambda g, j: (done(g, j), 0))
    else:
        ms = pl.BlockSpec((1, 1, D_MODEL), lambda g, j: (done(g, j) * tm // rows_per_seq, 0, 0))
    return pl.pallas_call(
        functools.partial(_peer_kernel, tm=tm, per_token=per_token, final_norm=final_norm),
        grid=(ntiles + 2, PEER_HEADS),
        in_specs=[pl.BlockSpec((tm, D_MODEL), lambda g, j: (stage_a(g, j), 0)),
                  pl.BlockSpec((tm, D_MODEL), lambda g, j: (routed(g, j), 0)),
                  pl.BlockSpec((D_MODEL, PEER_QDIM), lambda g, j: (0, j)),
                  pl.BlockSpec((2, PEER_KEYS, PEER_HALF), lambda g, j: (0, 0, 0)),
                  pl.BlockSpec((None, eb, D_MODEL), lambda g, j: (layer, j, 0)),
                  pl.BlockSpec((None, eb, D_MODEL), lambda g, j: (layer, (j + PEER_HEADS - 1) % PEER_HEADS, 0)),
                  pl.BlockSpec((tm, D_MODEL), lambda g, j: (done(g, j), 0)),
                  ms,
                  pl.BlockSpec((1, D_MODEL), lambda g, j: (0, 0))],
        out_specs=pl.BlockSpec((tm, D_MODEL), lambda g, j: (done(g, j), 0)),
        out_shape=jax.ShapeDtypeStruct((n, D_MODEL), F32),
        scratch_shapes=[pltpu.VMEM((tm, D_MODEL), F32),
                        pltpu.VMEM((tm, eb), BF16),
                        pltpu.VMEM((PEER_KEYS, tm, PEER_KEYS), BF16),
                        pltpu.VMEM((CAND_ROWS, tm), F32), pltpu.VMEM((CAND_ROWS, tm), F32),
                        pltpu.VMEM((npick, tm), F32), pltpu.VMEM((npick, tm), F32),
                        pltpu.VMEM((tm, npick), F32), pltpu.VMEM((tm, npick), F32), pltpu.VMEM((tm, npick), F32),
                        pltpu.VMEM((2 * TOKEN_GROUP * TOKEN_PITCH, PEER_KEYS), F32)],
        compiler_params=_params(0, 2),
        name="peer",
    )(h2, h2, wq, sk, u, v, xmid, g2, fg)


TOKEN_TILE = 512
PEER_TILE = 256
SAMPLE_ROWS = 8
PROMPT_SEQS_PER_STEP = 4
SAMPLE_SEQS_PER_STEP = 8


def _lane_pad(vec, offset=0, width=128):
    return jnp.zeros((1, width), F32).at[0, offset:offset + vec.shape[0]].set(vec)


def _layer_weights(l, w):
    w_in = w['w_in'][l]
    o_m = RWKV_PROJ
    o_mg = o_m + MLSTM_MAIN
    o_s = o_mg + 2 * MLSTM_HEADS
    o_sg = o_s + SSD_MAIN
    gate_cols = jnp.zeros((D_MODEL, GATE_W), F32)
    gate_cols = gate_cols.at[:, 0:MLSTM_HEADS].set(w_in[:, o_mg:o_mg + MLSTM_HEADS])
    gate_cols = gate_cols.at[:, 128:128 + MLSTM_HEADS].set(w_in[:, o_mg + MLSTM_HEADS:o_s])
    gate_cols = gate_cols.at[:, 256:256 + SSD_HEADS].set(w_in[:, o_sg:o_sg + SSD_HEADS])
    w_cat = jnp.concatenate([w_in[:, :o_m], w_in[:, o_m:o_mg], w_in[:, o_s:o_sg], gate_cols], axis=1).astype(BF16)
    gate_bias = jnp.concatenate([_lane_pad(w['mlstm_i_b'][l]), _lane_pad(w['mlstm_f_b'][l]),
                                 _lane_pad(w['ssd_dt_bias'][l])], axis=1)
    row = lambda name: w[name][l].reshape(1, -1)
    return dict(
        w_cat=w_cat, gate_bias=gate_bias, w_out=w['w_out'][l].astype(BF16),
        norm1_g=row('norm1_g'), norm2_g=row('norm2_g'),
        rwkv=(row('rwkv_mu'), row('rwkv_w0'), w['rwkv_w_up'][l].astype(BF16), row('rwkv_a0'),
              w['rwkv_a_up'][l].astype(BF16), w['rwkv_g_up'][l].astype(BF16), row('rwkv_k_k'), row('rwkv_k_a'),
              row('rwkv_r_k'), row('rwkv_ln_g'), row('rwkv_ln_b')),
        mlstm=(w['mlstm_conv_w'][l], row('mlstm_conv_b'), gate_bias, row('mlstm_norm_g')),
        ssd=(w['ssd_conv_w'][l], row('ssd_conv_b'), gate_bias, _lane_pad(w['ssd_A_log'][l]),
             _lane_pad(w['ssd_D'][l]), row('ssd_norm_g')),
        wq=w['peer_wq'][l].astype(BF16), sk=w['peer_subkeys'][l].astype(BF16),
        u=w['peer_u_bf16'], v=w['peer_v_bf16'], layer=l,
    )


def _run_layer(x, mods, state, lw, head_sum, final_g, *, batch, seq, rows, nseq, per_token, final_norm):
    sh1, sc1, g1, sh2, sc2, g2 = mods
    shift, wkv, mconv, m_c, m_n, m_m, sconv, sst = state
    tm = min(TOKEN_TILE, batch * seq)
    tok = dict(tm=tm, rows_per_seq=seq, per_token=per_token)
    p_r, p_m, p_s, p_g = _inproj(x, sh1, sc1, lw['norm1_g'], lw['w_cat'], **tok)

    t_pad = -(-seq // rows) * rows
    t_valid = seq if seq < rows else rows

    def seqs(a):
        a = a.reshape(batch, seq, a.shape[-1])
        return a if t_pad == seq else jnp.pad(a, ((0, 0), (0, t_pad - seq), (0, 0)))

    def toks(a):
        return a[:, :seq].reshape(batch * seq, a.shape[-1])

    mix = dict(rows=rows, t_valid=t_valid, nseq=nseq)
    o_r, shift, wkv = _rwkv(seqs(p_r), shift[:, None, :], wkv, *lw['rwkv'], head_sum, **mix)
    m_m = jnp.pad(m_m, ((0, 0), (0, 128 - MLSTM_HEADS)))[:, None, :]
    o_m, mconv, m_c, m_n, m_m = _mlstm(seqs(p_m), seqs(p_g), mconv, m_c, m_n, m_m, *lw['mlstm'], **mix)
    o_s, sconv, sst = _ssd(seqs(p_s), seqs(p_g), sconv, sst, *lw['ssd'], **mix)
    new_state = (shift[:, 0, :], wkv, mconv, m_c, m_n, m_m[:, 0, :MLSTM_HEADS], sconv, sst)

    x_mid, h2 = _outproj(toks(o_r), toks(o_m), toks(o_s), x, g1, sh2, sc2, lw['norm2_g'], lw['w_out'], **tok)
    x_new = _peer(h2, lw['wq'], lw['sk'], lw['u'], lw['v'], x_mid, g2, final_g, layer=lw['layer'],
                  tm=min(PEER_TILE, batch * seq), rows_per_seq=seq, per_token=per_token, final_norm=final_norm)
    return x_new, new_state


def kernel(x_prompt, x_sample, state_rwkv_shift, state_rwkv_wkv, state_mlstm_conv, state_mlstm_C, state_mlstm_n, state_mlstm_m, state_ssd_conv, state_ssd, c_prompt, c_sample, ada_w, ada_b, norm1_g, norm2_g, w_in, w_out, rwkv_mu, rwkv_w0, rwkv_w_up, rwkv_a0, rwkv_a_up, rwkv_g_up, rwkv_k_k, rwkv_k_a, rwkv_r_k, rwkv_ln_g, rwkv_ln_b, mlstm_conv_w, mlstm_conv_b, mlstm_i_b, mlstm_f_b, mlstm_norm_g, ssd_conv_w, ssd_conv_b, ssd_dt_bias, ssd_A_log, ssd_D, ssd_norm_g, peer_wq, peer_subkeys, peer_u, peer_v, final_g):
    weights = dict(norm1_g=norm1_g, norm2_g=norm2_g, w_in=w_in, w_out=w_out, rwkv_mu=rwkv_mu, rwkv_w0=rwkv_w0,
                   rwkv_w_up=rwkv_w_up, rwkv_a0=rwkv_a0, rwkv_a_up=rwkv_a_up, rwkv_g_up=rwkv_g_up,
                   rwkv_k_k=rwkv_k_k, rwkv_k_a=rwkv_k_a, rwkv_r_k=rwkv_r_k, rwkv_ln_g=rwkv_ln_g,
                   rwkv_ln_b=rwkv_ln_b, mlstm_conv_w=mlstm_conv_w, mlstm_conv_b=mlstm_conv_b,
                   mlstm_i_b=mlstm_i_b, mlstm_f_b=mlstm_f_b, mlstm_norm_g=mlstm_norm_g, ssd_conv_w=ssd_conv_w,
                   ssd_conv_b=ssd_conv_b, ssd_dt_bias=ssd_dt_bias, ssd_A_log=ssd_A_log, ssd_D=ssd_D,
                   ssd_norm_g=ssd_norm_g, peer_wq=peer_wq, peer_subkeys=peer_subkeys, peer_u=peer_u, peer_v=peer_v)
    weights['peer_u_bf16'] = peer_u.astype(BF16)
    weights['peer_v_bf16'] = peer_v.astype(BF16)
    nb, seq, d = x_prompt.shape
    nd, dseq, _ = x_sample.shape
    cache = (state_rwkv_shift, state_rwkv_wkv, state_mlstm_conv, state_mlstm_C, state_mlstm_n, state_mlstm_m,
             state_ssd_conv, state_ssd)

    mod = _ada_mod(jnp.concatenate([c_prompt, c_sample], axis=0), ada_w, ada_b)
    head_id = jnp.arange(RWKV_W) // HEAD_DIM
    head_sum = (head_id[:, None] == head_id[None, :]).astype(F32)
    fg = final_g.reshape(1, d)

    xp = x_prompt.reshape(nb * seq, d)
    xs = x_sample.reshape(nd * dseq, d)
    new_p, new_s = [], []
    for l in range(DEPTH):
        lw = _layer_weights(l, weights)
        last = l == DEPTH - 1
        mod_p = mod[l, :nb].reshape(nb, 6, 1, d)
        mods_p = tuple(mod_p[:, i] for i in range(6))
        mod_s = jnp.repeat(mod[l, nb:].reshape(nd, 6, d), dseq, axis=0)
        mods_s = tuple(mod_s[:, i] for i in range(6))
        zeros = tuple(jnp.zeros((nb,) + s.shape[2:], F32) for s in cache)
        xp, sp = _run_layer(xp, mods_p, zeros, lw, head_sum, fg, batch=nb, seq=seq, rows=CHUNK,
                            nseq=PROMPT_SEQS_PER_STEP, per_token=False, final_norm=last)
        xs, ss = _run_layer(xs, mods_s, tuple(s[l] for s in cache), lw, head_sum, fg, batch=nd, seq=dseq,
                            rows=SAMPLE_ROWS, nseq=SAMPLE_SEQS_PER_STEP, per_token=True, final_norm=last)
        new_p.append(sp)
        new_s.append(ss)

    def stk(states, i):
        return jnp.stack([st[i] for st in states])

    return ((xp.reshape(nb, seq, d), xs.reshape(nd, dseq, d))
            + tuple(stk(new_p, i) for i in range(8)) + tuple(stk(new_s, i) for i in range(8)))
```

```python
import functools
import math

import jax
import jax.numpy as jnp
from jax import lax
from jax.experimental import pallas as pl
from jax.experimental.pallas import tpu as pltpu

F32 = jnp.float32
BF16 = jnp.bfloat16

D_MODEL = 1024
DEPTH = 4
HEAD_DIM = 64
RWKV_W = 256
RWKV_HEADS = 4
RWKV_DECAY_RANK = 64
RWKV_ICLR_RANK = 64
RWKV_GATE_RANK = 128
RWKV_LN_EPS = 1e-5 * HEAD_DIM
RWKV_PROJ = 3 * RWKV_W + RWKV_DECAY_RANK + RWKV_ICLR_RANK + RWKV_GATE_RANK
MLSTM_W = 256
MLSTM_HEADS = 4
MLSTM_CONV_W = 2 * MLSTM_W
MLSTM_MAIN = MLSTM_CONV_W + 2 * MLSTM_W
SSD_W = 512
SSD_HEADS = 8
SSD_STATE = 128
SSD_GROUPS = 2
SSD_HEADS_PER_GROUP = SSD_HEADS // SSD_GROUPS
SSD_CONV_W = SSD_W + 2 * SSD_GROUPS * SSD_STATE
SSD_MAIN = SSD_W + SSD_CONV_W
CONV_WIDTH = 4
CHUNK = 64
LANES = 128
GATE_I, GATE_F, GATE_DT = 0, LANES, 2 * LANES
GATE_W = 3 * LANES
PEER_KEYS = 128
PEER_EXPERTS = PEER_KEYS * PEER_KEYS
PEER_HEADS = 8
PEER_TOPK = 16
PEER_QDIM = 256
PEER_HALF = 128
NORM_EPS = 1e-6
NEG_BIG = -1e30
TOKEN_GROUP = 16
TOKEN_PITCH = PEER_KEYS + 8

VMEM_LIMIT_BYTES = 56 * 1024 * 1024
HIST = 8


def _dot(a, b, prec=None):
    return jnp.dot(a, b, preferred_element_type=F32, precision=prec)


def _dot_nt(a, b, prec=None):
    return lax.dot_general(a, b, (((1,), (1,)), ((), ())), preferred_element_type=F32, precision=prec)


def _dot_tn(a, b, prec=None):
    return lax.dot_general(a, b, (((0,), (0,)), ((), ())), preferred_element_type=F32, precision=prec)


def _hi_lo(x):
    hi = x.astype(BF16)
    return hi, (x - hi.astype(F32)).astype(BF16)


def _mm(a, b, form, b_exact=False, single=False):
    dot = {'nn': _dot, 'nt': _dot_nt, 'tn': _dot_tn}[form]
    if single:
        return dot(a.astype(BF16), b.astype(BF16))
    ah, al = _hi_lo(a)
    a_axis = 0 if form == 'tn' else 1
    b_axis = 1 if form == 'nt' else 0
    if b_exact:
        bb = b.astype(BF16)
        a3 = jnp.concatenate([ah, al], axis=a_axis)
        b3 = jnp.concatenate([bb, bb], axis=b_axis)
    else:
        bh, bl = _hi_lo(b)
        a3 = jnp.concatenate([ah, ah, al], axis=a_axis)
        b3 = jnp.concatenate([bh, bl, bh], axis=b_axis)
    return dot(a3, b3)


def _cumsum_rows(x, tril_bf16):
    hi = x.astype(BF16)
    r1 = x - hi.astype(F32)
    mid = r1.astype(BF16)
    lo = (r1 - mid.astype(F32)).astype(BF16)
    return _dot(jnp.concatenate([tril_bf16] * 3, axis=1), jnp.concatenate([hi, mid, lo], axis=0))


def _sigmoid(x):
    return 1.0 / (1.0 + jnp.exp(-x))


def _silu(x):
    return x * _sigmoid(x)


def _softplus(x):
    return jnp.maximum(x, 0.0) + jnp.log(1.0 + jnp.exp(-jnp.abs(x)))


def _tri(n, strict=False):
    r = lax.broadcasted_iota(jnp.int32, (n, n), 0)
    c = lax.broadcasted_iota(jnp.int32, (n, n), 1)
    return (r > c) if strict else (r >= c)


def _to_row(col, eye):
    return jnp.sum(jnp.where(eye, col, 0.0), axis=0, keepdims=True)


def _params(n_parallel, n_arbitrary=0):
    sem = ("parallel",) * n_parallel + ("arbitrary",) * n_arbitrary
    return pltpu.CompilerParams(dimension_semantics=sem, vmem_limit_bytes=VMEM_LIMIT_BYTES)


def _ada_kernel(c_ref, w_ref, b_ref, o_ref):
    c = _silu(c_ref[...]).astype(BF16)
    o_ref[0] = _dot(c, w_ref[0].astype(BF16)) + b_ref[0]


def _ada_mod(c_all, ada_w, ada_b):
    nb = c_all.shape[0]
    tn = 1536
    return pl.pallas_call(
        _ada_kernel,
        grid=(DEPTH, 6 * D_MODEL // tn),
        in_specs=[pl.BlockSpec((nb, D_MODEL), lambda l, j: (0, 0)),
                  pl.BlockSpec((1, D_MODEL, tn), lambda l, j: (l, 0, j)),
                  pl.BlockSpec((1, 1, tn), lambda l, j: (l, 0, j))],
        out_specs=pl.BlockSpec((1, nb, tn), lambda l, j: (l, 0, j)),
        out_shape=jax.ShapeDtypeStruct((DEPTH, nb, 6 * D_MODEL), F32),
        compiler_params=_params(2),
        name="ada_mod",
    )(c_all, ada_w, ada_b.reshape(DEPTH, 1, 6 * D_MODEL))


def _rms(x, g):
    return x * lax.rsqrt(jnp.mean(x * x, axis=-1, keepdims=True) + NORM_EPS) * g


def _inproj_kernel(x_ref, sh_ref, sc_ref, g_ref, w_ref, pr_ref, pm_ref, ps_ref, pg_ref, *, per_token):
    sh = sh_ref[...] if per_token else sh_ref[0]
    sc = sc_ref[...] if per_token else sc_ref[0]
    h = _rms(x_ref[...], g_ref[...]) * (1.0 + sc) + sh
    p = _dot(h.astype(BF16), w_ref[...])
    o = 0
    for ref, w in ((pr_ref, RWKV_PROJ), (pm_ref, MLSTM_MAIN), (ps_ref, SSD_MAIN), (pg_ref, GATE_W)):
        ref[...] = p[:, o:o + w]
        o += w


def _mod_spec(tm, rows_per_seq, per_token):
    if per_token:
        return pl.BlockSpec((tm, D_MODEL), lambda i: (i, 0))
    return pl.BlockSpec((1, 1, D_MODEL), lambda i: (i * tm // rows_per_seq, 0, 0))


def _inproj(x, sh, sc, g, w, *, tm, rows_per_seq, per_token):
    n = x.shape[0]
    widths = (RWKV_PROJ, MLSTM_MAIN, SSD_MAIN, GATE_W)
    wtot = sum(widths)
    return pl.pallas_call(
        functools.partial(_inproj_kernel, per_token=per_token),
        grid=(n // tm,),
        in_specs=[pl.BlockSpec((tm, D_MODEL), lambda i: (i, 0)),
                  _mod_spec(tm, rows_per_seq, per_token), _mod_spec(tm, rows_per_seq, per_token),
                  pl.BlockSpec((1, D_MODEL), lambda i: (0, 0)),
                  pl.BlockSpec((D_MODEL, wtot), lambda i: (0, 0))],
        out_specs=[pl.BlockSpec((tm, w_), lambda i: (i, 0)) for w_ in widths],
        out_shape=[jax.ShapeDtypeStruct((n, w_), F32) for w_ in widths],
        compiler_params=_params(1),
        name="in_proj",
    )(x, sh, sc, g, w)


HIST0 = HIST - (CONV_WIDTH - 1)


def _causal_conv(ext_ref, s, u, w_ref, b_ref, rows, t_valid, new_buf_ref):
    ext_ref[s, pl.ds(HIST, rows), :] = u
    out = b_ref[...]
    for j in range(CONV_WIDTH):
        out = out + ext_ref[s, pl.ds(HIST0 + j, rows), :] * w_ref[pl.ds(j, 1), :]
    last = ext_ref[s, pl.ds(HIST0 + t_valid, CONV_WIDTH - 1), :]
    new_buf_ref[s] = last
    ext_ref[s, pl.ds(HIST0, CONV_WIDTH - 1), :] = last
    return out


def _mixer_specs(nseq, rows):
    chunk = lambda w, col=0: pl.BlockSpec((nseq, rows, w), lambda i, c: (i, c, col))
    full = lambda *s: pl.BlockSpec(s, lambda i, c: (0,) * len(s))
    per_seq = lambda *s: pl.BlockSpec((nseq,) + s, lambda i, c: (i,) + (0,) * len(s))
    return chunk, full, per_seq


def _row_mask(rows, t_valid):
    if t_valid == rows:
        return None
    return lax.broadcasted_iota(jnp.int32, (rows, 1), 0) < t_valid


def _mlstm_kernel(pm_ref, pg_ref, conv0_ref, c0_ref, n0_ref, m0_ref, cw_ref, cb_ref, gb_ref, ng_ref,
                  out_ref, conv_ref, c_ref, n_ref, m_ref, ext_ref, *, rows, t_valid, nseq):
    @pl.when(pl.program_id(1) == 0)
    def _():
        ext_ref[:, pl.ds(HIST0, CONV_WIDTH - 1), :] = conv0_ref[...]
        c_ref[...] = c0_ref[...]
        n_ref[...] = n0_ref[...]
        m_ref[...] = m0_ref[...]

    valid = _row_mask(rows, t_valid)
    tril = _tri(rows)
    tril_b = tril.astype(BF16)
    eye = lax.broadcasted_iota(jnp.int32, (rows, rows), 0) == lax.broadcasted_iota(jnp.int32, (rows, rows), 1)
    gb = gb_ref[...]

    seqs = []
    for s in range(nseq):
        pm = pm_ref[s]
        qk = _silu(_causal_conv(ext_ref, s, pm[:, :MLSTM_CONV_W], cw_ref, cb_ref, rows, t_valid, conv_ref))
        gates = pg_ref[s]
        ig = gates[:, GATE_I:GATE_I + LANES] + gb[:, GATE_I:GATE_I + LANES]
        fg = gates[:, GATE_F:GATE_F + LANES] + gb[:, GATE_F:GATE_F + LANES]
        logf = -_softplus(-fg)
        if valid is not None:
            ig = jnp.where(valid, ig, NEG_BIG)
            logf = jnp.where(valid, logf, 0.0)
        bcum = _cumsum_rows(logf, tril_b)
        m_prev = m_ref[s]
        b_end = bcum[rows - 1:rows, :]
        ws = b_end - bcum + ig
        m_new = jnp.maximum(b_end + m_prev, jnp.max(ws, axis=0, keepdims=True))
        m_ref[s] = m_new
        seqs.append(dict(q=qk[:, :MLSTM_W], k=qk[:, MLSTM_W:] * HEAD_DIM ** -0.5,
                         v=pm[:, MLSTM_CONV_W:MLSTM_CONV_W + MLSTM_W], o=pm[:, MLSTM_CONV_W + MLSTM_W:],
                         ig=ig, bcum=bcum, inter=bcum + m_prev, ws=jnp.exp(ws - m_new),
                         a_end=jnp.exp(b_end + m_prev - m_new)))

    chains = [(s, h) for s in range(nseq) for h in range(MLSTM_HEADS)]
    lanes = lambda c: slice(c[1] * HEAD_DIM, (c[1] + 1) * HEAD_DIM)
    col = lambda name, c: seqs[c[0]][name][:, c[1]:c[1] + 1]
    q = [seqs[c[0]]['q'][:, lanes(c)] for c in chains]
    k = [seqs[c[0]]['k'][:, lanes(c)] for c in chains]
    v = [seqs[c[0]]['v'][:, lanes(c)] for c in chains]
    c0 = [c_ref[c[0], c[1]] for c in chains]
    n0 = [n_ref[c[0], pl.ds(c[1], 1), :] for c in chains]
    qk_t = [_dot_nt(q_, k_) for q_, k_ in zip(q, k)]
    q_c = [_dot_nt(q_, c_) for q_, c_ in zip(q, c0)]
    c_add = [_dot_tn(col('ws', c) * v_, k_) for c, v_, k_ in zip(chains, v, k)]
    dmat = [jnp.where(tril, col('bcum', c) - _to_row(col('bcum', c), eye) + _to_row(col('ig', c), eye), NEG_BIG)
            for c in chains]
    m_t = [jnp.maximum(col('inter', c), jnp.max(d_, axis=1, keepdims=True)) for c, d_ in zip(chains, dmat)]
    sc = [g_ * jnp.exp(d_ - m_) for g_, d_, m_ in zip(qk_t, dmat, m_t)]
    s_v = [_dot(s_, v_) for s_, v_ in zip(sc, v)]
    ones = jnp.ones((rows, HEAD_DIM), BF16)
    ones_hh = jnp.ones((HEAD_DIM, HEAD_DIM), BF16)
    s_sum = [_mm(s_, ones, 'nn', b_exact=True) for s_ in sc]
    q_n = [_mm(q_, jnp.broadcast_to(n_, (HEAD_DIM, HEAD_DIM)), 'nt') for q_, n_ in zip(q, n0)]
    hh = []
    for i, c in enumerate(chains):
        a_in = jnp.exp(col('inter', c) - m_t[i])
        num = s_v[i] + a_in * q_c[i]
        den = s_sum[i] + a_in * q_n[i]
        hh.append(num / jnp.maximum(jnp.abs(den), jnp.exp(-m_t[i])))
        ae = col('a_end', c)
        c_ref[c[0], c[1]] = ae * c0[i] + c_add[i]
        n_ref[c[0], pl.ds(c[1], 1), :] = ae * n0[i] + jnp.sum(col('ws', c) * k[i], axis=0, keepdims=True)
    inv_n = 1.0 / HEAD_DIM
    hc = [h_ - _mm(h_, ones_hh, 'nn', b_exact=True) * inv_n for h_ in hh]
    var = [_mm(h_ * h_, ones_hh, 'nn', b_exact=True) * inv_n for h_ in hc]
    for i, c in enumerate(chains):
        hn = hc[i] * lax.rsqrt(var[i] + NORM_EPS)
        out_ref[c[0], :, lanes(c)] = hn * ng_ref[:, lanes(c)] * _sigmoid(seqs[c[0]]['o'][:, lanes(c)])


def _mlstm(pm, pg, conv0, c0, n0, m0, cw, cb, gb, ng, *, rows, t_valid, nseq):
    b, t, _ = pm.shape
    chunk, full, per_seq = _mixer_specs(nseq, rows)
    state_specs = [per_seq(CONV_WIDTH - 1, MLSTM_CONV_W), per_seq(MLSTM_HEADS, HEAD_DIM, HEAD_DIM),
                   per_seq(MLSTM_HEADS, HEAD_DIM), per_seq(1, LANES)]
    return pl.pallas_call(
        functools.partial(_mlstm_kernel, rows=rows, t_valid=t_valid, nseq=nseq),
        grid=(b // nseq, t // rows),
        in_specs=[chunk(MLSTM_MAIN), chunk(2 * LANES)] + state_specs
                 + [full(CONV_WIDTH, MLSTM_CONV_W), full(1, MLSTM_CONV_W), full(1, GATE_W), full(1, MLSTM_W)],
        out_specs=[chunk(MLSTM_W)] + state_specs,
        out_shape=[jax.ShapeDtypeStruct((b, t, MLSTM_W), F32),
                   jax.ShapeDtypeStruct((b, CONV_WIDTH - 1, MLSTM_CONV_W), F32),
                   jax.ShapeDtypeStruct((b, MLSTM_HEADS, HEAD_DIM, HEAD_DIM), F32),
                   jax.ShapeDtypeStruct((b, MLSTM_HEADS, HEAD_DIM), F32),
                   jax.ShapeDtypeStruct((b, 1, LANES), F32)],
        scratch_shapes=[pltpu.VMEM((nseq, HIST + rows, MLSTM_CONV_W), F32)],
        compiler_params=_params(1, 1),
        name="mlstm_mix",
    )(pm, pg, conv0, c0, n0, m0, cw, cb, gb, ng)


def _ssd_kernel(ps_ref, dt_ref, conv0_ref, h0_ref, cw_ref, cb_ref, gb_ref, alog_ref, dskip_ref, ng_ref,
                out_ref, conv_ref, h_ref, ext_ref, y_ref, *, rows, t_valid, nseq):
    @pl.when(pl.program_id(1) == 0)
    def _():
        ext_ref[:, pl.ds(HIST0, CONV_WIDTH - 1), :] = conv0_ref[...]
        h_ref[...] = h0_ref[...]

    valid = _row_mask(rows, t_valid)
    tril = _tri(rows)
    tril_b = tril.astype(BF16)
    eye = lax.broadcasted_iota(jnp.int32, (rows, rows), 0) == lax.broadcasted_iota(jnp.int32, (rows, rows), 1)
    dskip = dskip_ref[...]
    neg_a = -jnp.exp(alog_ref[...])
    gs = SSD_GROUPS * SSD_STATE

    seqs = []
    for s in range(nseq):
        ps = ps_ref[s]
        xbc = _silu(_causal_conv(ext_ref, s, ps[:, SSD_W:], cw_ref, cb_ref, rows, t_valid, conv_ref))
        dt = _softplus(dt_ref[s] + gb_ref[:, GATE_DT:GATE_DT + LANES])
        if valid is not None:
            dt = jnp.where(valid, dt, 0.0)
        seqs.append(dict(z=ps[:, :SSD_W], xs=xbc[:, :SSD_W], bm=xbc[:, SSD_W:SSD_W + gs], cm=xbc[:, SSD_W + gs:],
                         dt=dt, acum=_cumsum_rows(dt * neg_a, tril_b)))

    groups = [(s, g) for s in range(nseq) for g in range(SSD_GROUPS)]
    grp = lambda name, sg: seqs[sg[0]][name][:, sg[1] * SSD_STATE:(sg[1] + 1) * SSD_STATE]
    cb = {sg: _dot_nt(grp('cm', sg), grp('bm', sg)) for sg in groups}
    chains = [(s, hd) for s in range(nseq) for hd in range(SSD_HEADS)]
    lanes = lambda c: slice(c[1] * HEAD_DIM, (c[1] + 1) * HEAD_DIM)
    col = lambda name, c: seqs[c[0]][name][:, c[1]:c[1] + 1]
    group_of = lambda c: (c[0], c[1] // SSD_HEADS_PER_GROUP)
    x = [seqs[c[0]]['xs'][:, lanes(c)] for c in chains]
    h0 = [h_ref[c[0], c[1]] for c in chains]
    from_state = [_dot_nt(grp('cm', group_of(c)), h_) for c, h_ in zip(chains, h0)]
    h_add = [_dot_tn(jnp.exp(col('acum', c)[rows - 1:rows, :] - col('acum', c)) * col('dt', c) * x_,
                     grp('bm', group_of(c))) for c, x_ in zip(chains, x)]
    wmat = [jnp.exp(jnp.where(tril, col('acum', c) - _to_row(col('acum', c), eye), NEG_BIG))
            * cb[group_of(c)] * _to_row(col('dt', c), eye) for c in chains]
    y = [_dot(w_, x_) for w_, x_ in zip(wmat, x)]
    for i, c in enumerate(chains):
        a_col = col('acum', c)
        h_ref[c[0], c[1]] = jnp.exp(a_col[rows - 1:rows, :]) * h0[i] + h_add[i]
        y_ref[c[0], :, lanes(c)] = y[i] + jnp.exp(a_col) * from_state[i] + dskip[:, c[1]:c[1] + 1] * x[i]
    for s in range(nseq):
        out_ref[s] = _rms(y_ref[s] * _silu(seqs[s]['z']), ng_ref[...])


def _ssd(ps, pg, conv0, h0, cw, cb, gb, alog, dskip, ng, *, rows, t_valid, nseq):
    b, t, _ = ps.shape
    chunk, full, per_seq = _mixer_specs(nseq, rows)
    state_specs = [per_seq(CONV_WIDTH - 1, SSD_CONV_W), per_seq(SSD_HEADS, HEAD_DIM, SSD_STATE)]
    return pl.pallas_call(
        functools.partial(_ssd_kernel, rows=rows, t_valid=t_valid, nseq=nseq),
        grid=(b // nseq, t // rows),
        in_specs=[chunk(SSD_MAIN), chunk(LANES, col=GATE_DT // LANES)] + state_specs
                 + [full(CONV_WIDTH, SSD_CONV_W), full(1, SSD_CONV_W), full(1, GATE_W),
                    full(1, LANES), full(1, LANES), full(1, SSD_W)],
        out_specs=[chunk(SSD_W)] + state_specs,
        out_shape=[jax.ShapeDtypeStruct((b, t, SSD_W), F32),
                   jax.ShapeDtypeStruct((b, CONV_WIDTH - 1, SSD_CONV_W), F32),
                   jax.ShapeDtypeStruct((b, SSD_HEADS, HEAD_DIM, SSD_STATE), F32)],
        scratch_shapes=[pltpu.VMEM((nseq, HIST + rows, SSD_CONV_W), F32), pltpu.VMEM((nseq, rows, SSD_W), F32)],
        compiler_params=_params(1, 1),
        name="ssd_mix",
    )(ps, pg, conv0, h0, cw, cb, gb, alog, dskip, ng)


def _rwkv_kernel(pr_ref, shift0_ref, s0_ref, mu_ref, w0_ref, wup_ref, a0_ref, aup_ref, gup_ref, kk_ref, ka_ref,
                 rk_ref, lng_ref, lnb_ref, bd_ref,
                 out_ref, shift_ref, s_ref, ext_ref, y_ref, *, rows, t_valid, nseq):
    @pl.when(pl.program_id(1) == 0)
    def _():
        ext_ref[:, pl.ds(HIST - 1, 1), :] = shift0_ref[...]
        s_ref[...] = s0_ref[...]

    tril = _tri(rows)
    stril = _tri(rows, strict=True)
    bd = bd_ref[...]
    head_sum = lambda t: _mm(t, bd, 'nn', b_exact=True)
    w3 = 3 * RWKV_W
    n = nseq * rows

    p3 = pr_ref[...]
    ext_ref[:, pl.ds(HIST, rows), :] = p3
    prev = ext_ref[:, pl.ds(HIST - 1, rows), :].reshape(n, RWKV_PROJ)
    last = ext_ref[:, pl.ds(HIST - 1 + t_valid, 1), :]
    shift_ref[...] = last
    ext_ref[:, pl.ds(HIST - 1, 1), :] = last
    p = p3.reshape(n, RWKV_PROJ)
    x = p + (prev - p) * mu_ref[...]
    r = x[:, :RWKV_W]
    k = x[:, RWKV_W:2 * RWKV_W]
    v = x[:, 2 * RWKV_W:w3]
    xw = x[:, w3:w3 + RWKV_DECAY_RANK]
    xa = x[:, w3 + RWKV_DECAY_RANK:w3 + RWKV_DECAY_RANK + RWKV_ICLR_RANK]
    xg = x[:, w3 + RWKV_DECAY_RANK + RWKV_ICLR_RANK:]
    w_log = -_softplus(-(w0_ref[...] + _dot(jnp.tanh(xw).astype(BF16), wup_ref[...]))) - 0.5
    logw = -jnp.exp(w_log)
    a = _sigmoid(a0_ref[...] + _dot(xa.astype(BF16), aup_ref[...]))
    g = _dot(_sigmoid(xg).astype(BF16), gup_ref[...])
    kk = k * kk_ref[...]
    kk = kk / jnp.maximum(jnp.sqrt(head_sum(kk * kk)), 1e-12)
    k = k * (1.0 + (a - 1.0) * ka_ref[...])
    ri = lax.broadcasted_iota(jnp.int32, (n, n), 0)
    ci = lax.broadcasted_iota(jnp.int32, (n, n), 1)
    same_seq = (ri // rows) == (ci // rows)
    if t_valid != rows:
        valid = (lax.broadcasted_iota(jnp.int32, (n, 1), 0) % rows) < t_valid
        logw = jnp.where(valid, logw, 0.0)
        kk = jnp.where(valid, kk, 0.0)
        k = jnp.where(valid, k, 0.0)

    cum = _cumsum_rows(logw, (same_seq & (ri >= ci)).astype(BF16))
    to_end = jnp.exp(_cumsum_rows(logw, (same_seq & (ri < ci)).astype(BF16)))
    p_inv = jnp.exp(-cum)
    a_t = -kk * jnp.exp(cum - logw)
    kka = kk * a
    b_t = kka * p_inv
    k_t = k * p_inv
    r_t = r * jnp.exp(cum)
    b_e = kka * to_end
    k_e = k * to_end
    p_end = jnp.exp(cum)

    chains = [(s, h) for s in range(nseq) for h in range(RWKV_HEADS)]
    blk = lambda t, c: t[c[0] * rows:(c[0] + 1) * rows, c[1] * HEAD_DIM:(c[1] + 1) * HEAD_DIM]
    s0 = [s_ref[c[0], c[1]] for c in chains]
    vh = [blk(v, c) for c in chains]
    ar = [jnp.concatenate([blk(a_t, c), blk(r_t, c)], axis=0) for c in chains]
    bk = [jnp.concatenate([blk(b_t, c), blk(k_t, c)], axis=0) for c in chains]
    mm = functools.partial(_mm, single=True)
    gram = [mm(x_, y_, 'nt') for x_, y_ in zip(ar, bk)]
    from_state = [mm(x_, y_, 'nt') for x_, y_ in zip(ar, s0)]
    n_mat = [jnp.where(stril, g_[:rows, :rows], 0.0) for g_ in gram]
    u = [f_[:rows] + mm(jnp.where(stril, g_[:rows, rows:], 0.0), v_, 'nn')
         for f_, g_, v_ in zip(from_state, gram, vh)]
    span = 1
    while span < rows:
        u = [u_ + mm(n_, u_, 'nn') for u_, n_ in zip(u, n_mat)]
        span *= 2
        if span < rows:
            n_mat = [mm(n_, n_, 'nn') for n_ in n_mat]
    y_u = [mm(jnp.where(tril, g_[rows:, :rows], 0.0), u_, 'nn') for g_, u_ in zip(gram, u)]
    y_v = [mm(jnp.where(tril, g_[rows:, rows:], 0.0), v_, 'nn') for g_, v_ in zip(gram, vh)]
    s_add = [mm(jnp.concatenate([u_, v_], axis=0), jnp.concatenate([blk(b_e, c), blk(k_e, c)], axis=0), 'tn')
             for u_, v_, c in zip(u, vh, chains)]
    for i, c in enumerate(chains):
        lanes = slice(c[1] * HEAD_DIM, (c[1] + 1) * HEAD_DIM)
        y_ref[c[0], :, lanes] = from_state[i][rows:] + y_u[i] + y_v[i]
        end_row = (c[0] + 1) * rows - 1
        s_ref[c[0], c[1]] = p_end[end_row:end_row + 1, lanes] * s0[i] + s_add[i]

    y = y_ref[...].reshape(n, RWKV_W)
    inv_n = 1.0 / HEAD_DIM
    yc = y - head_sum(y) * inv_n
    yn = yc * lax.rsqrt(head_sum(yc * yc) * inv_n + RWKV_LN_EPS)
    bonus = head_sum(r * k * rk_ref[...]) * v
    out_ref[...] = ((yn * lng_ref[...] + lnb_ref[...] + bonus) * g).reshape(nseq, rows, RWKV_W)


def _rwkv(pr, shift0, s0, mu, w0, wup, a0, aup, gup, kkp, kap, rk, lng, lnb, bd, *, rows, t_valid, nseq):
    b, t, _ = pr.shape
    chunk, full, per_seq = _mixer_specs(nseq, rows)
    state_specs = [per_seq(1, RWKV_PROJ), per_seq(RWKV_HEADS, HEAD_DIM, HEAD_DIM)]
    return pl.pallas_call(
        functools.partial(_rwkv_kernel, rows=rows, t_valid=t_valid, nseq=nseq),
        grid=(b // nseq, t // rows),
        in_specs=[chunk(RWKV_PROJ)] + state_specs
                 + [full(1, RWKV_PROJ), full(1, RWKV_W), full(RWKV_DECAY_RANK, RWKV_W), full(1, RWKV_W),
                    full(RWKV_ICLR_RANK, RWKV_W), full(RWKV_GATE_RANK, RWKV_W), full(1, RWKV_W), full(1, RWKV_W),
                    full(1, RWKV_W), full(1, RWKV_W), full(1, RWKV_W), full(RWKV_W, RWKV_W)],
        out_specs=[chunk(RWKV_W)] + state_specs,
        out_shape=[jax.ShapeDtypeStruct((b, t, RWKV_W), F32),
                   jax.ShapeDtypeStruct((b, 1, RWKV_PROJ), F32),
                   jax.ShapeDtypeStruct((b, RWKV_HEADS, HEAD_DIM, HEAD_DIM), F32)],
        scratch_shapes=[pltpu.VMEM((nseq, HIST + rows, RWKV_PROJ), F32), pltpu.VMEM((nseq, rows, RWKV_W), F32)],
        compiler_params=_params(1, 1),
        name="rwkv_mix",
    )(pr, shift0, s0, mu, w0, wup, a0, aup, gup, kkp, kap, rk, lng, lnb, bd)


def _outproj_kernel(or_ref, om_ref, os_ref, x_ref, g1_ref, sh_ref, sc_ref, ng_ref, w_ref, xmid_ref, h2_ref,
                    *, per_token):
    pick = (lambda r: r[...]) if per_token else (lambda r: r[0])
    w = w_ref[...]
    mix = (_dot(or_ref[...].astype(BF16), w[:RWKV_W])
           + _dot(om_ref[...].astype(BF16), w[RWKV_W:RWKV_W + MLSTM_W])
           + _dot(os_ref[...].astype(BF16), w[RWKV_W + MLSTM_W:]))
    x = x_ref[...] + pick(g1_ref) * mix
    xmid_ref[...] = x
    h2_ref[...] = (_rms(x, ng_ref[...]) * (1.0 + pick(sc_ref)) + pick(sh_ref)).astype(BF16)


def _outproj(o_r, o_m, o_s, x, g1, sh, sc, ng, w, *, tm, rows_per_seq, per_token):
    n = x.shape[0]
    tok = lambda w_: pl.BlockSpec((tm, w_), lambda i: (i, 0))
    ms = _mod_spec(tm, rows_per_seq, per_token)
    return pl.pallas_call(
        functools.partial(_outproj_kernel, per_token=per_token),
        grid=(n // tm,),
        in_specs=[tok(RWKV_W), tok(MLSTM_W), tok(SSD_W), tok(D_MODEL), ms, ms, ms,
                  pl.BlockSpec((1, D_MODEL), lambda i: (0, 0)),
                  pl.BlockSpec((D_MODEL, D_MODEL), lambda i: (0, 0))],
        out_specs=[tok(D_MODEL), tok(D_MODEL)],
        out_shape=[jax.ShapeDtypeStruct((n, D_MODEL), F32), jax.ShapeDtypeStruct((n, D_MODEL), BF16)],
        compiler_params=_params(1),
        name="out_proj",
    )(o_r, o_m, o_s, x, g1, sh, sc, ng, w)


CAND_LEN = tuple(PEER_TOPK // (p + 1) for p in range(PEER_TOPK))
CAND_OFF = tuple(sum(CAND_LEN[:p]) for p in range(PEER_TOPK))
CAND_ROWS = -(-sum(CAND_LEN) // 8) * 8


def _over_sublanes(r8, op):
    for shift in (4, 2, 1):
        r8 = op(r8, pltpu.roll(r8, shift, axis=0))
    return r8


def _row_iota(tiles, n):
    shape = (tiles, 8, n)
    return (lax.broadcasted_iota(jnp.int32, shape, 0) * 8 + lax.broadcasted_iota(jnp.int32, shape, 1)).astype(F32)


def _top1(x3, iota3):
    m = _over_sublanes(jnp.max(x3, axis=0), jnp.maximum)
    idx = _over_sublanes(jnp.min(jnp.where(x3 == m[None], iota3, float(8 * x3.shape[0])), axis=0), jnp.minimum)
    return m, idx


def _gelu_tanh(x):
    c1 = math.sqrt(2.0 / math.pi)
    half = 0.5 * x
    return half * jnp.tanh(x * (c1 + (c1 * 0.044715) * (x * x))) + half


def _peer_kernel(h2c_ref, h2n_ref, wq_ref, sk_ref, u_ref, v_ref, xmid_ref, g2_ref, fg_ref, o_ref,
                 acc_ref, act_ref, gm_ref, cand_ref, eid_ref, et_ref, gt_ref, a_ref, b_ref, g_ref, tmp_ref,
                 *, tm, per_token, final_norm):
    j = pl.program_id(1)

    @pl.when((pl.program_id(0) == 0) & (j == 0))
    def _():
        acc_ref[...] = jnp.zeros_like(acc_ref)
        act_ref[...] = jnp.zeros_like(act_ref)
        gm_ref[...] = jnp.zeros_like(gm_ref)

    q = _dot(h2n_ref[...], wq_ref[...]).astype(BF16)
    s12 = [_dot_nt(sk_ref[c], q[:, c * PEER_HALF:(c + 1) * PEER_HALF]) for c in range(2)]
    acc_ref[...] += _dot(act_ref[...], v_ref[...])
    s = _dot_nt(h2c_ref[...], u_ref[...])

    rows_per_block = u_ref.shape[0] // PEER_KEYS
    for r in range(rows_per_block):
        lanes = slice(r * PEER_KEYS, (r + 1) * PEER_KEYS)
        act_ref[:, lanes] = _gelu_tanh(s[:, lanes].astype(BF16)) * gm_ref[j * rows_per_block + r]

    key_iota = _row_iota(PEER_KEYS // 8, tm)
    s12 = [s_.reshape(PEER_KEYS // 8, 8, tm) for s_ in s12]
    vals = ([], [])
    idxs = ([], [])
    for _ in range(PEER_TOPK):
        for c in range(2):
            m, idx = _top1(s12[c], key_iota)
            s12[c] = jnp.where(key_iota == idx[None], -jnp.inf, s12[c])
            vals[c].append(m[0:1])
            idxs[c].append(idx[0:1])
    v1, v2 = (jnp.concatenate(v_, axis=0) for v_ in vals)
    i1, i2 = (jnp.concatenate(i_, axis=0) for i_ in idxs)

    cand_ref[...] = jnp.full(cand_ref.shape, -jnp.inf, F32)
    eid_ref[...] = jnp.zeros(eid_ref.shape, F32)
    for p in range(PEER_TOPK):
        cand_ref[pl.ds(CAND_OFF[p], CAND_LEN[p]), :] = v1[p:p + 1, :] + v2[:CAND_LEN[p], :]
        eid_ref[pl.ds(CAND_OFF[p], CAND_LEN[p]), :] = i1[p:p + 1, :] * float(PEER_KEYS) + i2[:CAND_LEN[p], :]
    cand_iota = _row_iota(CAND_ROWS // 8, tm)
    cand = cand_ref[...].reshape(CAND_ROWS // 8, 8, tm)
    eid = eid_ref[...].reshape(CAND_ROWS // 8, 8, tm)
    tops, picks = [], []
    for _ in range(PEER_TOPK):
        m, pos = _top1(cand, cand_iota)
        hit = cand_iota == pos[None]
        cand = jnp.where(hit, -jnp.inf, cand)
        tops.append(m[0:1])
        picks.append(_over_sublanes(jnp.sum(jnp.where(hit, eid, 0.0), axis=0), jnp.add)[0:1])
    top = jnp.concatenate(tops, axis=0)
    e = jnp.exp(top - top[0:1, :])
    row0 = pl.multiple_of(j * PEER_TOPK, PEER_TOPK)
    et_ref[pl.ds(row0, PEER_TOPK), :] = jnp.concatenate(picks, axis=0)
    gt_ref[pl.ds(row0, PEER_TOPK), :] = e / jnp.sum(e, axis=0, keepdims=True)

    @pl.when(j == 0)
    def _():
        g2 = g2_ref[...] if per_token else g2_ref[0]
        x = xmid_ref[...] + g2 * acc_ref[...]
        o_ref[...] = _rms(x, fg_ref[...]) if final_norm else x
        acc_ref[...] = jnp.zeros_like(acc_ref)

    @pl.when(j == PEER_HEADS - 1)
    def _():
        picked = et_ref[...]
        key1 = jnp.floor(picked * (1.0 / PEER_KEYS))
        a_ref[...] = key1.T
        b_ref[...] = (picked - key1 * float(PEER_KEYS)).T
        g_ref[...] = gt_ref[...].T
        j_iota = lax.broadcasted_iota(jnp.int32, (PEER_KEYS, PEER_HEADS * PEER_TOPK), 0).astype(F32)

        def token_groups(gi, carry):
            for half in range(2):
                t0 = pl.multiple_of((gi * 2 + half) * TOKEN_GROUP, TOKEN_GROUP)
                base = half * TOKEN_GROUP * TOKEN_PITCH
                for tt in range(TOKEN_GROUP):
                    t = t0 + tt
                    sel1 = jnp.where(j_iota == a_ref[pl.ds(t, 1), :], 1.0, 0.0).astype(BF16)
                    sel2 = jnp.where(j_iota == b_ref[pl.ds(t, 1), :], g_ref[pl.ds(t, 1), :], 0.0).astype(BF16)
                    tmp_ref[pl.ds(base + tt * TOKEN_PITCH, PEER_KEYS), :] = _dot_nt(sel1, sel2)
            for half in range(2):
                t0 = pl.multiple_of((gi * 2 + half) * TOKEN_GROUP, TOKEN_GROUP)
                base = half * TOKEN_GROUP * TOKEN_PITCH
                for r in range(PEER_KEYS):
                    rows = tmp_ref[pl.ds(base + r, TOKEN_GROUP, stride=TOKEN_PITCH), :]
                    gm_ref[r, pl.ds(t0, TOKEN_GROUP), :] = rows.astype(BF16)
            return carry

        lax.fori_loop(0, tm // (2 * TOKEN_GROUP), token_groups, 0)


def _peer(h2, wq, sk, u, v, xmid, g2, fg, *, layer, tm, rows_per_seq, per_token, final_norm):
    n = h2.shape[0]
    ntiles = n // tm
    eb = PEER_EXPERTS // PEER_HEADS
    npick = PEER_HEADS * PEER_TOPK
    clamp = lambda t: jnp.clip(t, 0, ntiles - 1)
    routed = lambda g, j: clamp(g)
    stage_a = lambda g, j: clamp(g - 1)
    done = lambda g, j: clamp(g - 2 + jnp.minimum(j, 1))
    if per_token:
        ms = pl.BlockSpec((tm, D_MODEL), lambda g, j: (done(g, j), 0))
    else:
        ms = pl.BlockSpec((1, 1, D_MODEL), lambda g, j: (done(g, j) * tm // rows_per_seq, 0, 0))
    return pl.pallas_call(
        functools.partial(_peer_kernel, tm=tm, per_token=per_token, final_norm=final_norm),
        grid=(ntiles + 2, PEER_HEADS),
        in_specs=[pl.BlockSpec((tm, D_MODEL), lambda g, j: (stage_a(g, j), 0)),
                  pl.BlockSpec((tm, D_MODEL), lambda g, j: (routed(g, j), 0)),
                  pl.BlockSpec((D_MODEL, PEER_QDIM), lambda g, j: (0, j)),
                  pl.BlockSpec((2, PEER_KEYS, PEER_HALF), lambda g, j: (0, 0, 0)),
                  pl.BlockSpec((None, eb, D_MODEL), lambda g, j: (layer, j, 0)),
                  pl.BlockSpec((None, eb, D_MODEL), lambda g, j: (layer, (j + PEER_HEADS - 1) % PEER_HEADS, 0)),
                  pl.BlockSpec((tm, D_MODEL), lambda g, j: (done(g, j), 0)),
                  ms,
                  pl.BlockSpec((1, D_MODEL), lambda g, j: (0, 0))],
        out_specs=pl.BlockSpec((tm, D_MODEL), lambda g, j: (done(g, j), 0)),
        out_shape=jax.ShapeDtypeStruct((n, D_MODEL), F32),
        scratch_shapes=[pltpu.VMEM((tm, D_MODEL), F32),
                        pltpu.VMEM((tm, eb), BF16),
                        pltpu.VMEM((PEER_KEYS, tm, PEER_KEYS), BF16),
                        pltpu.VMEM((CAND_ROWS, tm), F32), pltpu.VMEM((CAND_ROWS, tm), F32),
                        pltpu.VMEM((npick, tm), F32), pltpu.VMEM((npick, tm), F32),
                        pltpu.VMEM((tm, npick), F32), pltpu.VMEM((tm, npick), F32), pltpu.VMEM((tm, npick), F32),
                        pltpu.VMEM((2 * TOKEN_GROUP * TOKEN_PITCH, PEER_KEYS), F32)],
        compiler_params=_params(0, 2),
        name="peer",
    )(h2, h2, wq, sk, u, v, xmid, g2, fg)


TOKEN_TILE = 512
PEER_TILE = 256
SAMPLE_ROWS = 8
PROMPT_SEQS_PER_STEP = 4
SAMPLE_SEQS_PER_STEP = 8


def _lane_pad(vec, offset=0, width=LANES):
    return jnp.zeros((1, width), F32).at[0, offset:offset + vec.shape[0]].set(vec)


def _layer_weights(l, w):
    w_in = w['w_in'][l]
    o_m = RWKV_PROJ
    o_mg = o_m + MLSTM_MAIN
    o_s = o_mg + 2 * MLSTM_HEADS
    o_sg = o_s + SSD_MAIN
    gate_cols = jnp.zeros((D_MODEL, GATE_W), F32)
    gate_cols = gate_cols.at[:, 0:MLSTM_HEADS].set(w_in[:, o_mg:o_mg + MLSTM_HEADS])
    gate_cols = gate_cols.at[:, GATE_F:GATE_F + MLSTM_HEADS].set(w_in[:, o_mg + MLSTM_HEADS:o_s])
    gate_cols = gate_cols.at[:, GATE_DT:GATE_DT + SSD_HEADS].set(w_in[:, o_sg:o_sg + SSD_HEADS])
    w_cat = jnp.concatenate([w_in[:, :o_m], w_in[:, o_m:o_mg], w_in[:, o_s:o_sg], gate_cols], axis=1).astype(BF16)
    gate_bias = jnp.concatenate([_lane_pad(w['mlstm_i_b'][l]), _lane_pad(w['mlstm_f_b'][l]),
                                 _lane_pad(w['ssd_dt_bias'][l])], axis=1)
    row = lambda name: w[name][l].reshape(1, -1)
    return dict(
        w_cat=w_cat, gate_bias=gate_bias, w_out=w['w_out'][l].astype(BF16),
        norm1_g=row('norm1_g'), norm2_g=row('norm2_g'),
        rwkv=(row('rwkv_mu'), row('rwkv_w0'), w['rwkv_w_up'][l].astype(BF16), row('rwkv_a0'),
              w['rwkv_a_up'][l].astype(BF16), w['rwkv_g_up'][l].astype(BF16), row('rwkv_k_k'), row('rwkv_k_a'),
              row('rwkv_r_k'), row('rwkv_ln_g'), row('rwkv_ln_b')),
        mlstm=(w['mlstm_conv_w'][l], row('mlstm_conv_b'), gate_bias, row('mlstm_norm_g')),
        ssd=(w['ssd_conv_w'][l], row('ssd_conv_b'), gate_bias, _lane_pad(w['ssd_A_log'][l]),
             _lane_pad(w['ssd_D'][l]), row('ssd_norm_g')),
        wq=w['peer_wq'][l].astype(BF16), sk=w['peer_subkeys'][l].astype(BF16),
        u=w['peer_u_bf16'], v=w['peer_v_bf16'], layer=l,
    )


def _run_layer(x, mods, state, lw, head_sum, final_g, *, batch, seq, rows, nseq, per_token, final_norm):
    sh1, sc1, g1, sh2, sc2, g2 = mods
    shift, wkv, mconv, m_c, m_n, m_m, sconv, sst = state
    tm = min(TOKEN_TILE, batch * seq)
    tok = dict(tm=tm, rows_per_seq=seq, per_token=per_token)
    p_r, p_m, p_s, p_g = _inproj(x, sh1, sc1, lw['norm1_g'], lw['w_cat'], **tok)

    t_pad = -(-seq // rows) * rows
    t_valid = seq if seq < rows else rows

    def seqs(a):
        a = a.reshape(batch, seq, a.shape[-1])
        return a if t_pad == seq else jnp.pad(a, ((0, 0), (0, t_pad - seq), (0, 0)))

    def toks(a):
        return a[:, :seq].reshape(batch * seq, a.shape[-1])

    mix = dict(rows=rows, t_valid=t_valid, nseq=nseq)
    o_r, shift, wkv = _rwkv(seqs(p_r), shift[:, None, :], wkv, *lw['rwkv'], head_sum, **mix)
    m_m = jnp.pad(m_m, ((0, 0), (0, LANES - MLSTM_HEADS)))[:, None, :]
    o_m, mconv, m_c, m_n, m_m = _mlstm(seqs(p_m), seqs(p_g), mconv, m_c, m_n, m_m, *lw['mlstm'], **mix)
    o_s, sconv, sst = _ssd(seqs(p_s), seqs(p_g), sconv, sst, *lw['ssd'], **mix)
    new_state = (shift[:, 0, :], wkv, mconv, m_c, m_n, m_m[:, 0, :MLSTM_HEADS], sconv, sst)

    x_mid, h2 = _outproj(toks(o_r), toks(o_m), toks(o_s), x, g1, sh2, sc2, lw['norm2_g'], lw['w_out'], **tok)
    x_new = _peer(h2, lw['wq'], lw['sk'], lw['u'], lw['v'], x_mid, g2, final_g, layer=lw['layer'],
                  tm=min(PEER_TILE, batch * seq), rows_per_seq=seq, per_token=per_token, final_norm=final_norm)
    return x_new, new_state


def kernel(x_prompt, x_sample, state_rwkv_shift, state_rwkv_wkv, state_mlstm_conv, state_mlstm_C, state_mlstm_n, state_mlstm_m, state_ssd_conv, state_ssd, c_prompt, c_sample, ada_w, ada_b, norm1_g, norm2_g, w_in, w_out, rwkv_mu, rwkv_w0, rwkv_w_up, rwkv_a0, rwkv_a_up, rwkv_g_up, rwkv_k_k, rwkv_k_a, rwkv_r_k, rwkv_ln_g, rwkv_ln_b, mlstm_conv_w, mlstm_conv_b, mlstm_i_b, mlstm_f_b, mlstm_norm_g, ssd_conv_w, ssd_conv_b, ssd_dt_bias, ssd_A_log, ssd_D, ssd_norm_g, peer_wq, peer_subkeys, peer_u, peer_v, final_g):
    weights = dict(norm1_g=norm1_g, norm2_g=norm2_g, w_in=w_in, w_out=w_out, rwkv_mu=rwkv_mu, rwkv_w0=rwkv_w0,
                   rwkv_w_up=rwkv_w_up, rwkv_a0=rwkv_a0, rwkv_a_up=rwkv_a_up, rwkv_g_up=rwkv_g_up,
                   rwkv_k_k=rwkv_k_k, rwkv_k_a=rwkv_k_a, rwkv_r_k=rwkv_r_k, rwkv_ln_g=rwkv_ln_g,
                   rwkv_ln_b=rwkv_ln_b, mlstm_conv_w=mlstm_conv_w, mlstm_conv_b=mlstm_conv_b,
                   mlstm_i_b=mlstm_i_b, mlstm_f_b=mlstm_f_b, mlstm_norm_g=mlstm_norm_g, ssd_conv_w=ssd_conv_w,
                   ssd_conv_b=ssd_conv_b, ssd_dt_bias=ssd_dt_bias, ssd_A_log=ssd_A_log, ssd_D=ssd_D,
                   ssd_norm_g=ssd_norm_g, peer_wq=peer_wq, peer_subkeys=peer_subkeys, peer_u=peer_u, peer_v=peer_v)
    weights['peer_u_bf16'] = peer_u.astype(BF16)
    weights['peer_v_bf16'] = peer_v.astype(BF16)
    nb, seq, d = x_prompt.shape
    nd, dseq, _ = x_sample.shape
    cache = (state_rwkv_shift, state_rwkv_wkv, state_mlstm_conv, state_mlstm_C, state_mlstm_n, state_mlstm_m,
             state_ssd_conv, state_ssd)

    mod = _ada_mod(jnp.concatenate([c_prompt, c_sample], axis=0), ada_w, ada_b)
    head_id = jnp.arange(RWKV_W) // HEAD_DIM
    head_sum = (head_id[:, None] == head_id[None, :]).astype(F32)
    fg = final_g.reshape(1, d)

    xp = x_prompt.reshape(nb * seq, d)
    xs = x_sample.reshape(nd * dseq, d)
    new_p, new_s = [], []
    for l in range(DEPTH):
        lw = _layer_weights(l, weights)
        last = l == DEPTH - 1
        mod_p = mod[l, :nb].reshape(nb, 6, 1, d)
        mods_p = tuple(mod_p[:, i] for i in range(6))
        mod_s = jnp.repeat(mod[l, nb:].reshape(nd, 6, d), dseq, axis=0)
        mods_s = tuple(mod_s[:, i] for i in range(6))
        zeros = tuple(jnp.zeros((nb,) + s.shape[2:], F32) for s in cache)
        xp, sp = _run_layer(xp, mods_p, zeros, lw, head_sum, fg, batch=nb, seq=seq, rows=CHUNK,
                            nseq=PROMPT_SEQS_PER_STEP, per_token=False, final_norm=last)
        xs, ss = _run_layer(xs, mods_s, tuple(s[l] for s in cache), lw, head_sum, fg, batch=nd, seq=dseq,
                            rows=SAMPLE_ROWS, nseq=SAMPLE_SEQS_PER_STEP, per_token=True, final_norm=last)
        new_p.append(sp)
        new_s.append(ss)

    def stk(states, i):
        return jnp.stack([st[i] for st in states])

    return ((xp.reshape(nb, seq, d), xs.reshape(nd, dseq, d))
            + tuple(stk(new_p, i) for i in range(8)) + tuple(stk(new_s, i) for i in range(8)))
```

```python
import functools
import math

import jax
import jax.numpy as jnp
from jax import lax
from jax.experimental import pallas as pl
from jax.experimental.pallas import tpu as pltpu

F32 = jnp.float32
BF16 = jnp.bfloat16

D_MODEL = 1024
DEPTH = 4
HEAD_DIM = 64
RWKV_W = 256
RWKV_HEADS = 4
RWKV_DECAY_RANK = 64
RWKV_ICLR_RANK = 64
RWKV_GATE_RANK = 128
RWKV_LN_EPS = 1e-5 * HEAD_DIM
RWKV_PROJ = 3 * RWKV_W + RWKV_DECAY_RANK + RWKV_ICLR_RANK + RWKV_GATE_RANK
MLSTM_W = 256
MLSTM_HEADS = 4
MLSTM_CONV_W = 2 * MLSTM_W
MLSTM_MAIN = MLSTM_CONV_W + 2 * MLSTM_W
SSD_W = 512
SSD_HEADS = 8
SSD_STATE = 128
SSD_GROUPS = 2
SSD_HEADS_PER_GROUP = SSD_HEADS // SSD_GROUPS
SSD_CONV_W = SSD_W + 2 * SSD_GROUPS * SSD_STATE
SSD_MAIN = SSD_W + SSD_CONV_W
CONV_WIDTH = 4
CHUNK = 64
LANES = 128
GATE_I, GATE_F, GATE_DT = 0, LANES, 2 * LANES
GATE_W = 3 * LANES
PEER_KEYS = 128
PEER_EXPERTS = PEER_KEYS * PEER_KEYS
PEER_HEADS = 8
PEER_TOPK = 16
PEER_QDIM = 256
PEER_HALF = 128
NORM_EPS = 1e-6
NEG_BIG = -1e30
TOKEN_GROUP = 16
GROUPS_PER_TRIP = 4
TOKEN_PITCH = PEER_KEYS + 4

VMEM_LIMIT_BYTES = 56 * 1024 * 1024
HIST = 8


def _dot(a, b, prec=None):
    return jnp.dot(a, b, preferred_element_type=F32, precision=prec)


def _dot_nt(a, b, prec=None):
    return lax.dot_general(a, b, (((1,), (1,)), ((), ())), preferred_element_type=F32, precision=prec)


def _dot_tn(a, b, prec=None):
    return lax.dot_general(a, b, (((0,), (0,)), ((), ())), preferred_element_type=F32, precision=prec)


def _hi_lo(x):
    hi = x.astype(BF16)
    return hi, (x - hi.astype(F32)).astype(BF16)


def _mm(a, b, form, b_exact=False, single=False):
    dot = {'nn': _dot, 'nt': _dot_nt, 'tn': _dot_tn}[form]
    if single:
        return dot(a.astype(BF16), b.astype(BF16))
    ah, al = _hi_lo(a)
    a_axis = 0 if form == 'tn' else 1
    b_axis = 1 if form == 'nt' else 0
    if b_exact:
        bb = b.astype(BF16)
        a3 = jnp.concatenate([ah, al], axis=a_axis)
        b3 = jnp.concatenate([bb, bb], axis=b_axis)
    else:
        bh, bl = _hi_lo(b)
        a3 = jnp.concatenate([ah, ah, al], axis=a_axis)
        b3 = jnp.concatenate([bh, bl, bh], axis=b_axis)
    return dot(a3, b3)


def _cumsum_rows(x, tril_bf16):
    hi = x.astype(BF16)
    r1 = x - hi.astype(F32)
    mid = r1.astype(BF16)
    lo = (r1 - mid.astype(F32)).astype(BF16)
    return _dot(jnp.concatenate([tril_bf16] * 3, axis=1), jnp.concatenate([hi, mid, lo], axis=0))


def _sigmoid(x):
    return 1.0 / (1.0 + jnp.exp(-x))


def _silu(x):
    return x * _sigmoid(x)


def _softplus(x):
    return jnp.maximum(x, 0.0) + jnp.log(1.0 + jnp.exp(-jnp.abs(x)))


def _tri(n, strict=False):
    r = lax.broadcasted_iota(jnp.int32, (n, n), 0)
    c = lax.broadcasted_iota(jnp.int32, (n, n), 1)
    return (r > c) if strict else (r >= c)


def _to_row(col, eye):
    return jnp.sum(jnp.where(eye, col, 0.0), axis=0, keepdims=True)


def _params(n_parallel, n_arbitrary=0):
    sem = ("parallel",) * n_parallel + ("arbitrary",) * n_arbitrary
    return pltpu.CompilerParams(dimension_semantics=sem, vmem_limit_bytes=VMEM_LIMIT_BYTES)


def _ada_kernel(c_ref, w_ref, b_ref, o_ref):
    c = _silu(c_ref[...]).astype(BF16)
    o_ref[0] = _dot(c, w_ref[0].astype(BF16)) + b_ref[0]


def _ada_mod(c_all, ada_w, ada_b):
    nb = c_all.shape[0]
    tn = 1536
    return pl.pallas_call(
        _ada_kernel,
        grid=(DEPTH, 6 * D_MODEL // tn),
        in_specs=[pl.BlockSpec((nb, D_MODEL), lambda l, j: (0, 0)),
                  pl.BlockSpec((1, D_MODEL, tn), lambda l, j: (l, 0, j)),
                  pl.BlockSpec((1, 1, tn), lambda l, j: (l, 0, j))],
        out_specs=pl.BlockSpec((1, nb, tn), lambda l, j: (l, 0, j)),
        out_shape=jax.ShapeDtypeStruct((DEPTH, nb, 6 * D_MODEL), F32),
        compiler_params=_params(2),
        name="ada_mod",
    )(c_all, ada_w, ada_b.reshape(DEPTH, 1, 6 * D_MODEL))


def _rms(x, g):
    return x * lax.rsqrt(jnp.mean(x * x, axis=-1, keepdims=True) + NORM_EPS) * g


def _inproj_kernel(x_ref, sh_ref, sc_ref, g_ref, w_ref, pr_ref, pm_ref, ps_ref, pg_ref, *, per_token):
    sh = sh_ref[...] if per_token else sh_ref[0]
    sc = sc_ref[...] if per_token else sc_ref[0]
    h = _rms(x_ref[...], g_ref[...]) * (1.0 + sc) + sh
    p = _dot(h.astype(BF16), w_ref[...])
    o = 0
    for ref, w in ((pr_ref, RWKV_PROJ), (pm_ref, MLSTM_MAIN), (ps_ref, SSD_MAIN), (pg_ref, GATE_W)):
        ref[...] = p[:, o:o + w]
        o += w


def _mod_spec(tm, rows_per_seq, per_token):
    if per_token:
        return pl.BlockSpec((tm, D_MODEL), lambda i: (i, 0))
    return pl.BlockSpec((1, 1, D_MODEL), lambda i: (i * tm // rows_per_seq, 0, 0))


def _inproj(x, sh, sc, g, w, *, tm, rows_per_seq, per_token):
    n = x.shape[0]
    widths = (RWKV_PROJ, MLSTM_MAIN, SSD_MAIN, GATE_W)
    wtot = sum(widths)
    return pl.pallas_call(
        functools.partial(_inproj_kernel, per_token=per_token),
        grid=(n // tm,),
        in_specs=[pl.BlockSpec((tm, D_MODEL), lambda i: (i, 0)),
                  _mod_spec(tm, rows_per_seq, per_token), _mod_spec(tm, rows_per_seq, per_token),
                  pl.BlockSpec((1, D_MODEL), lambda i: (0, 0)),
                  pl.BlockSpec((D_MODEL, wtot), lambda i: (0, 0))],
        out_specs=[pl.BlockSpec((tm, w_), lambda i: (i, 0)) for w_ in widths],
        out_shape=[jax.ShapeDtypeStruct((n, w_), F32) for w_ in widths],
        compiler_params=_params(1),
        name="in_proj",
    )(x, sh, sc, g, w)


HIST0 = HIST - (CONV_WIDTH - 1)


def _causal_conv(ext_ref, s, u, w_ref, b_ref, rows, t_valid, new_buf_ref):
    ext_ref[s, pl.ds(HIST, rows), :] = u
    out = b_ref[...]
    for j in range(CONV_WIDTH):
        out = out + ext_ref[s, pl.ds(HIST0 + j, rows), :] * w_ref[pl.ds(j, 1), :]
    last = ext_ref[s, pl.ds(HIST0 + t_valid, CONV_WIDTH - 1), :]
    new_buf_ref[s] = last
    ext_ref[s, pl.ds(HIST0, CONV_WIDTH - 1), :] = last
    return out


def _mixer_specs(nseq, rows):
    chunk = lambda w, col=0: pl.BlockSpec((nseq, rows, w), lambda i, c: (i, c, col))
    full = lambda *s: pl.BlockSpec(s, lambda i, c: (0,) * len(s))
    per_seq = lambda *s: pl.BlockSpec((nseq,) + s, lambda i, c: (i,) + (0,) * len(s))
    return chunk, full, per_seq


def _row_mask(rows, t_valid):
    if t_valid == rows:
        return None
    return lax.broadcasted_iota(jnp.int32, (rows, 1), 0) < t_valid


def _mlstm_kernel(pm_ref, pg_ref, conv0_ref, c0_ref, n0_ref, m0_ref, cw_ref, cb_ref, gb_ref, ng_ref,
                  out_ref, conv_ref, c_ref, n_ref, m_ref, ext_ref, *, rows, t_valid, nseq):
    @pl.when(pl.program_id(1) == 0)
    def _():
        ext_ref[:, pl.ds(HIST0, CONV_WIDTH - 1), :] = conv0_ref[...]
        c_ref[...] = c0_ref[...]
        n_ref[...] = n0_ref[...]
        m_ref[...] = m0_ref[...]

    valid = _row_mask(rows, t_valid)
    tril = _tri(rows)
    tril_b = tril.astype(BF16)
    eye = lax.broadcasted_iota(jnp.int32, (rows, rows), 0) == lax.broadcasted_iota(jnp.int32, (rows, rows), 1)
    gb = gb_ref[...]

    seqs = []
    for s in range(nseq):
        pm = pm_ref[s]
        qk = _silu(_causal_conv(ext_ref, s, pm[:, :MLSTM_CONV_W], cw_ref, cb_ref, rows, t_valid, conv_ref))
        gates = pg_ref[s]
        ig = gates[:, GATE_I:GATE_I + LANES] + gb[:, GATE_I:GATE_I + LANES]
        fg = gates[:, GATE_F:GATE_F + LANES] + gb[:, GATE_F:GATE_F + LANES]
        logf = -_softplus(-fg)
        if valid is not None:
            ig = jnp.where(valid, ig, NEG_BIG)
            logf = jnp.where(valid, logf, 0.0)
        bcum = _cumsum_rows(logf, tril_b)
        m_prev = m_ref[s]
        b_end = bcum[rows - 1:rows, :]
        ws = b_end - bcum + ig
        m_new = jnp.maximum(b_end + m_prev, jnp.max(ws, axis=0, keepdims=True))
        m_ref[s] = m_new
        seqs.append(dict(q=qk[:, :MLSTM_W], k=qk[:, MLSTM_W:] * HEAD_DIM ** -0.5,
                         v=pm[:, MLSTM_CONV_W:MLSTM_CONV_W + MLSTM_W], o=pm[:, MLSTM_CONV_W + MLSTM_W:],
                         ig=ig, bcum=bcum, inter=bcum + m_prev, ws=jnp.exp(ws - m_new),
                         a_end=jnp.exp(b_end + m_prev - m_new)))

    chains = [(s, h) for s in range(nseq) for h in range(MLSTM_HEADS)]
    lanes = lambda c: slice(c[1] * HEAD_DIM, (c[1] + 1) * HEAD_DIM)
    col = lambda name, c: seqs[c[0]][name][:, c[1]:c[1] + 1]
    q = [seqs[c[0]]['q'][:, lanes(c)] for c in chains]
    k = [seqs[c[0]]['k'][:, lanes(c)] for c in chains]
    v = [seqs[c[0]]['v'][:, lanes(c)] for c in chains]
    c0 = [c_ref[c[0], c[1]] for c in chains]
    n0 = [n_ref[c[0], pl.ds(c[1], 1), :] for c in chains]
    qk_t = [_dot_nt(q_, k_) for q_, k_ in zip(q, k)]
    q_c = [_dot_nt(q_, c_) for q_, c_ in zip(q, c0)]
    c_add = [_dot_tn(col('ws', c) * v_, k_) for c, v_, k_ in zip(chains, v, k)]
    dmat = [jnp.where(tril, col('bcum', c) - _to_row(col('bcum', c), eye) + _to_row(col('ig', c), eye), NEG_BIG)
            for c in chains]
    m_t = [jnp.maximum(col('inter', c), jnp.max(d_, axis=1, keepdims=True)) for c, d_ in zip(chains, dmat)]
    sc = [g_ * jnp.exp(d_ - m_) for g_, d_, m_ in zip(qk_t, dmat, m_t)]
    s_v = [_dot(s_, v_) for s_, v_ in zip(sc, v)]
    ones = jnp.ones((rows, HEAD_DIM), BF16)
    ones_hh = jnp.ones((HEAD_DIM, HEAD_DIM), BF16)
    s_sum = [_mm(s_, ones, 'nn', b_exact=True) for s_ in sc]
    q_n = [_mm(q_, jnp.broadcast_to(n_, (HEAD_DIM, HEAD_DIM)), 'nt') for q_, n_ in zip(q, n0)]
    hh = []
    for i, c in enumerate(chains):
        a_in = jnp.exp(col('inter', c) - m_t[i])
        num = s_v[i] + a_in * q_c[i]
        den = s_sum[i] + a_in * q_n[i]
        hh.append(num / jnp.maximum(jnp.abs(den), jnp.exp(-m_t[i])))
        ae = col('a_end', c)
        c_ref[c[0], c[1]] = ae * c0[i] + c_add[i]
        n_ref[c[0], pl.ds(c[1], 1), :] = ae * n0[i] + jnp.sum(col('ws', c) * k[i], axis=0, keepdims=True)
    inv_n = 1.0 / HEAD_DIM
    hc = [h_ - _mm(h_, ones_hh, 'nn', b_exact=True) * inv_n for h_ in hh]
    var = [_mm(h_ * h_, ones_hh, 'nn', b_exact=True) * inv_n for h_ in hc]
    for i, c in enumerate(chains):
        hn = hc[i] * lax.rsqrt(var[i] + NORM_EPS)
        out_ref[c[0], :, lanes(c)] = hn * ng_ref[:, lanes(c)] * _sigmoid(seqs[c[0]]['o'][:, lanes(c)])


def _mlstm(pm, pg, conv0, c0, n0, m0, cw, cb, gb, ng, *, rows, t_valid, nseq):
    b, t, _ = pm.shape
    chunk, full, per_seq = _mixer_specs(nseq, rows)
    state_specs = [per_seq(CONV_WIDTH - 1, MLSTM_CONV_W), per_seq(MLSTM_HEADS, HEAD_DIM, HEAD_DIM),
                   per_seq(MLSTM_HEADS, HEAD_DIM), per_seq(1, LANES)]
    return pl.pallas_call(
        functools.partial(_mlstm_kernel, rows=rows, t_valid=t_valid, nseq=nseq),
        grid=(b // nseq, t // rows),
        in_specs=[chunk(MLSTM_MAIN), chunk(2 * LANES)] + state_specs
                 + [full(CONV_WIDTH, MLSTM_CONV_W), full(1, MLSTM_CONV_W), full(1, GATE_W), full(1, MLSTM_W)],
        out_specs=[chunk(MLSTM_W)] + state_specs,
        out_shape=[jax.ShapeDtypeStruct((b, t, MLSTM_W), F32),
                   jax.ShapeDtypeStruct((b, CONV_WIDTH - 1, MLSTM_CONV_W), F32),
                   jax.ShapeDtypeStruct((b, MLSTM_HEADS, HEAD_DIM, HEAD_DIM), F32),
                   jax.ShapeDtypeStruct((b, MLSTM_HEADS, HEAD_DIM), F32),
                   jax.ShapeDtypeStruct((b, 1, LANES), F32)],
        scratch_shapes=[pltpu.VMEM((nseq, HIST + rows, MLSTM_CONV_W), F32)],
        compiler_params=_params(1, 1),
        name="mlstm_mix",
    )(pm, pg, conv0, c0, n0, m0, cw, cb, gb, ng)


def _ssd_kernel(ps_ref, dt_ref, conv0_ref, h0_ref, cw_ref, cb_ref, gb_ref, alog_ref, dskip_ref, ng_ref,
                out_ref, conv_ref, h_ref, ext_ref, y_ref, *, rows, t_valid, nseq):
    @pl.when(pl.program_id(1) == 0)
    def _():
        ext_ref[:, pl.ds(HIST0, CONV_WIDTH - 1), :] = conv0_ref[...]
        h_ref[...] = h0_ref[...]

    valid = _row_mask(rows, t_valid)
    tril = _tri(rows)
    tril_b = tril.astype(BF16)
    eye = lax.broadcasted_iota(jnp.int32, (rows, rows), 0) == lax.broadcasted_iota(jnp.int32, (rows, rows), 1)
    dskip = dskip_ref[...]
    neg_a = -jnp.exp(alog_ref[...])
    gs = SSD_GROUPS * SSD_STATE

    seqs = []
    for s in range(nseq):
        ps = ps_ref[s]
        xbc = _silu(_causal_conv(ext_ref, s, ps[:, SSD_W:], cw_ref, cb_ref, rows, t_valid, conv_ref))
        dt = _softplus(dt_ref[s] + gb_ref[:, GATE_DT:GATE_DT + LANES])
        if valid is not None:
            dt = jnp.where(valid, dt, 0.0)
        seqs.append(dict(z=ps[:, :SSD_W], xs=xbc[:, :SSD_W], bm=xbc[:, SSD_W:SSD_W + gs], cm=xbc[:, SSD_W + gs:],
                         dt=dt, acum=_cumsum_rows(dt * neg_a, tril_b)))

    groups = [(s, g) for s in range(nseq) for g in range(SSD_GROUPS)]
    grp = lambda name, sg: seqs[sg[0]][name][:, sg[1] * SSD_STATE:(sg[1] + 1) * SSD_STATE]
    cb = {sg: _dot_nt(grp('cm', sg), grp('bm', sg)) for sg in groups}
    chains = [(s, hd) for s in range(nseq) for hd in range(SSD_HEADS)]
    lanes = lambda c: slice(c[1] * HEAD_DIM, (c[1] + 1) * HEAD_DIM)
    col = lambda name, c: seqs[c[0]][name][:, c[1]:c[1] + 1]
    group_of = lambda c: (c[0], c[1] // SSD_HEADS_PER_GROUP)
    x = [seqs[c[0]]['xs'][:, lanes(c)] for c in chains]
    h0 = [h_ref[c[0], c[1]] for c in chains]
    from_state = [_dot_nt(grp('cm', group_of(c)), h_) for c, h_ in zip(chains, h0)]
    h_add = [_dot_tn(jnp.exp(col('acum', c)[rows - 1:rows, :] - col('acum', c)) * col('dt', c) * x_,
                     grp('bm', group_of(c))) for c, x_ in zip(chains, x)]
    wmat = [jnp.exp(jnp.where(tril, col('acum', c) - _to_row(col('acum', c), eye), NEG_BIG))
            * cb[group_of(c)] * _to_row(col('dt', c), eye) for c in chains]
    y = [_dot(w_, x_) for w_, x_ in zip(wmat, x)]
    for i, c in enumerate(chains):
        a_col = col('acum', c)
        h_ref[c[0], c[1]] = jnp.exp(a_col[rows - 1:rows, :]) * h0[i] + h_add[i]
        y_ref[c[0], :, lanes(c)] = y[i] + jnp.exp(a_col) * from_state[i] + dskip[:, c[1]:c[1] + 1] * x[i]
    for s in range(nseq):
        out_ref[s] = _rms(y_ref[s] * _silu(seqs[s]['z']), ng_ref[...])


def _ssd(ps, pg, conv0, h0, cw, cb, gb, alog, dskip, ng, *, rows, t_valid, nseq):
    b, t, _ = ps.shape
    chunk, full, per_seq = _mixer_specs(nseq, rows)
    state_specs = [per_seq(CONV_WIDTH - 1, SSD_CONV_W), per_seq(SSD_HEADS, HEAD_DIM, SSD_STATE)]
    return pl.pallas_call(
        functools.partial(_ssd_kernel, rows=rows, t_valid=t_valid, nseq=nseq),
        grid=(b // nseq, t // rows),
        in_specs=[chunk(SSD_MAIN), chunk(LANES, col=GATE_DT // LANES)] + state_specs
                 + [full(CONV_WIDTH, SSD_CONV_W), full(1, SSD_CONV_W), full(1, GATE_W),
                    full(1, LANES), full(1, LANES), full(1, SSD_W)],
        out_specs=[chunk(SSD_W)] + state_specs,
        out_shape=[jax.ShapeDtypeStruct((b, t, SSD_W), F32),
                   jax.ShapeDtypeStruct((b, CONV_WIDTH - 1, SSD_CONV_W), F32),
                   jax.ShapeDtypeStruct((b, SSD_HEADS, HEAD_DIM, SSD_STATE), F32)],
        scratch_shapes=[pltpu.VMEM((nseq, HIST + rows, SSD_CONV_W), F32), pltpu.VMEM((nseq, rows, SSD_W), F32)],
        compiler_params=_params(1, 1),
        name="ssd_mix",
    )(ps, pg, conv0, h0, cw, cb, gb, alog, dskip, ng)


def _rwkv_kernel(pr_ref, shift0_ref, s0_ref, mu_ref, w0_ref, wup_ref, a0_ref, aup_ref, gup_ref, kk_ref, ka_ref,
                 rk_ref, lng_ref, lnb_ref, bd_ref,
                 out_ref, shift_ref, s_ref, ext_ref, y_ref, *, rows, t_valid, nseq):
    @pl.when(pl.program_id(1) == 0)
    def _():
        ext_ref[:, pl.ds(HIST - 1, 1), :] = shift0_ref[...]
        s_ref[...] = s0_ref[...]

    tril = _tri(rows)
    stril = _tri(rows, strict=True)
    bd = bd_ref[...]
    head_sum = lambda t: _mm(t, bd, 'nn', b_exact=True)
    w3 = 3 * RWKV_W
    n = nseq * rows

    p3 = pr_ref[...]
    ext_ref[:, pl.ds(HIST, rows), :] = p3
    prev = ext_ref[:, pl.ds(HIST - 1, rows), :].reshape(n, RWKV_PROJ)
    last = ext_ref[:, pl.ds(HIST - 1 + t_valid, 1), :]
    shift_ref[...] = last
    ext_ref[:, pl.ds(HIST - 1, 1), :] = last
    p = p3.reshape(n, RWKV_PROJ)
    x = p + (prev - p) * mu_ref[...]
    r = x[:, :RWKV_W]
    k = x[:, RWKV_W:2 * RWKV_W]
    v = x[:, 2 * RWKV_W:w3]
    xw = x[:, w3:w3 + RWKV_DECAY_RANK]
    xa = x[:, w3 + RWKV_DECAY_RANK:w3 + RWKV_DECAY_RANK + RWKV_ICLR_RANK]
    xg = x[:, w3 + RWKV_DECAY_RANK + RWKV_ICLR_RANK:]
    w_log = -_softplus(-(w0_ref[...] + _dot(jnp.tanh(xw).astype(BF16), wup_ref[...]))) - 0.5
    logw = -jnp.exp(w_log)
    a = _sigmoid(a0_ref[...] + _dot(xa.astype(BF16), aup_ref[...]))
    g = _dot(_sigmoid(xg).astype(BF16), gup_ref[...])
    kk = k * kk_ref[...]
    kk = kk / jnp.maximum(jnp.sqrt(head_sum(kk * kk)), 1e-12)
    k = k * (1.0 + (a - 1.0) * ka_ref[...])
    ri = lax.broadcasted_iota(jnp.int32, (n, n), 0)
    ci = lax.broadcasted_iota(jnp.int32, (n, n), 1)
    same_seq = (ri // rows) == (ci // rows)
    if t_valid != rows:
        valid = (lax.broadcasted_iota(jnp.int32, (n, 1), 0) % rows) < t_valid
        logw = jnp.where(valid, logw, 0.0)
        kk = jnp.where(valid, kk, 0.0)
        k = jnp.where(valid, k, 0.0)

    cum = _cumsum_rows(logw, (same_seq & (ri >= ci)).astype(BF16))
    to_end = jnp.exp(_cumsum_rows(logw, (same_seq & (ri < ci)).astype(BF16)))
    p_inv = jnp.exp(-cum)
    a_t = -kk * jnp.exp(cum - logw)
    kka = kk * a
    b_t = kka * p_inv
    k_t = k * p_inv
    r_t = r * jnp.exp(cum)
    b_e = kka * to_end
    k_e = k * to_end
    p_end = jnp.exp(cum)

    chains = [(s, h) for s in range(nseq) for h in range(RWKV_HEADS)]
    blk = lambda t, c: t[c[0] * rows:(c[0] + 1) * rows, c[1] * HEAD_DIM:(c[1] + 1) * HEAD_DIM]
    s0 = [s_ref[c[0], c[1]] for c in chains]
    vh = [blk(v, c) for c in chains]
    ar = [jnp.concatenate([blk(a_t, c), blk(r_t, c)], axis=0) for c in chains]
    bk = [jnp.concatenate([blk(b_t, c), blk(k_t, c)], axis=0) for c in chains]
    mm = functools.partial(_mm, single=True)
    gram = [mm(x_, y_, 'nt') for x_, y_ in zip(ar, bk)]
    from_state = [mm(x_, y_, 'nt') for x_, y_ in zip(ar, s0)]
    n_mat = [jnp.where(stril, g_[:rows, :rows], 0.0) for g_ in gram]
    u = [f_[:rows] + mm(jnp.where(stril, g_[:rows, rows:], 0.0), v_, 'nn')
         for f_, g_, v_ in zip(from_state, gram, vh)]
    span = 1
    while span < rows:
        u = [u_ + mm(n_, u_, 'nn') for u_, n_ in zip(u, n_mat)]
        span *= 2
        if span < rows:
            n_mat = [mm(n_, n_, 'nn') for n_ in n_mat]
    y_u = [mm(jnp.where(tril, g_[rows:, :rows], 0.0), u_, 'nn') for g_, u_ in zip(gram, u)]
    y_v = [mm(jnp.where(tril, g_[rows:, rows:], 0.0), v_, 'nn') for g_, v_ in zip(gram, vh)]
    s_add = [mm(jnp.concatenate([u_, v_], axis=0), jnp.concatenate([blk(b_e, c), blk(k_e, c)], axis=0), 'tn')
             for u_, v_, c in zip(u, vh, chains)]
    for i, c in enumerate(chains):
        lanes = slice(c[1] * HEAD_DIM, (c[1] + 1) * HEAD_DIM)
        y_ref[c[0], :, lanes] = from_state[i][rows:] + y_u[i] + y_v[i]
        end_row = (c[0] + 1) * rows - 1
        s_ref[c[0], c[1]] = p_end[end_row:end_row + 1, lanes] * s0[i] + s_add[i]

    y = y_ref[...].reshape(n, RWKV_W)
    inv_n = 1.0 / HEAD_DIM
    yc = y - head_sum(y) * inv_n
    yn = yc * lax.rsqrt(head_sum(yc * yc) * inv_n + RWKV_LN_EPS)
    bonus = head_sum(r * k * rk_ref[...]) * v
    out_ref[...] = ((yn * lng_ref[...] + lnb_ref[...] + bonus) * g).reshape(nseq, rows, RWKV_W)


def _rwkv(pr, shift0, s0, mu, w0, wup, a0, aup, gup, kkp, kap, rk, lng, lnb, bd, *, rows, t_valid, nseq):
    b, t, _ = pr.shape
    chunk, full, per_seq = _mixer_specs(nseq, rows)
    state_specs = [per_seq(1, RWKV_PROJ), per_seq(RWKV_HEADS, HEAD_DIM, HEAD_DIM)]
    return pl.pallas_call(
        functools.partial(_rwkv_kernel, rows=rows, t_valid=t_valid, nseq=nseq),
        grid=(b // nseq, t // rows),
        in_specs=[chunk(RWKV_PROJ)] + state_specs
                 + [full(1, RWKV_PROJ), full(1, RWKV_W), full(RWKV_DECAY_RANK, RWKV_W), full(1, RWKV_W),
                    full(RWKV_ICLR_RANK, RWKV_W), full(RWKV_GATE_RANK, RWKV_W), full(1, RWKV_W), full(1, RWKV_W),
                    full(1, RWKV_W), full(1, RWKV_W), full(1, RWKV_W), full(RWKV_W, RWKV_W)],
        out_specs=[chunk(RWKV_W)] + state_specs,
        out_shape=[jax.ShapeDtypeStruct((b, t, RWKV_W), F32),
                   jax.ShapeDtypeStruct((b, 1, RWKV_PROJ), F32),
                   jax.ShapeDtypeStruct((b, RWKV_HEADS, HEAD_DIM, HEAD_DIM), F32)],
        scratch_shapes=[pltpu.VMEM((nseq, HIST + rows, RWKV_PROJ), F32), pltpu.VMEM((nseq, rows, RWKV_W), F32)],
        compiler_params=_params(1, 1),
        name="rwkv_mix",
    )(pr, shift0, s0, mu, w0, wup, a0, aup, gup, kkp, kap, rk, lng, lnb, bd)


def _outproj_kernel(or_ref, om_ref, os_ref, x_ref, g1_ref, sh_ref, sc_ref, ng_ref, w_ref, xmid_ref, h2_ref,
                    *, per_token):
    pick = (lambda r: r[...]) if per_token else (lambda r: r[0])
    w = w_ref[...]
    mix = (_dot(or_ref[...].astype(BF16), w[:RWKV_W])
           + _dot(om_ref[...].astype(BF16), w[RWKV_W:RWKV_W + MLSTM_W])
           + _dot(os_ref[...].astype(BF16), w[RWKV_W + MLSTM_W:]))
    x = x_ref[...] + pick(g1_ref) * mix
    xmid_ref[...] = x
    h2_ref[...] = (_rms(x, ng_ref[...]) * (1.0 + pick(sc_ref)) + pick(sh_ref)).astype(BF16)


def _outproj(o_r, o_m, o_s, x, g1, sh, sc, ng, w, *, tm, rows_per_seq, per_token):
    n = x.shape[0]
    tok = lambda w_: pl.BlockSpec((tm, w_), lambda i: (i, 0))
    ms = _mod_spec(tm, rows_per_seq, per_token)
    return pl.pallas_call(
        functools.partial(_outproj_kernel, per_token=per_token),
        grid=(n // tm,),
        in_specs=[tok(RWKV_W), tok(MLSTM_W), tok(SSD_W), tok(D_MODEL), ms, ms, ms,
                  pl.BlockSpec((1, D_MODEL), lambda i: (0, 0)),
                  pl.BlockSpec((D_MODEL, D_MODEL), lambda i: (0, 0))],
        out_specs=[tok(D_MODEL), tok(D_MODEL)],
        out_shape=[jax.ShapeDtypeStruct((n, D_MODEL), F32), jax.ShapeDtypeStruct((n, D_MODEL), BF16)],
        compiler_params=_params(1),
        name="out_proj",
    )(o_r, o_m, o_s, x, g1, sh, sc, ng, w)


CAND_LEN = tuple(PEER_TOPK // (p + 1) for p in range(PEER_TOPK))
CAND_OFF = tuple(sum(CAND_LEN[:p]) for p in range(PEER_TOPK))
CAND_ROWS = -(-sum(CAND_LEN) // 8) * 8


def _over_sublanes(r8, op):
    for shift in (4, 2, 1):
        r8 = op(r8, pltpu.roll(r8, shift, axis=0))
    return r8


def _row_iota(tiles, n):
    shape = (tiles, 8, n)
    return (lax.broadcasted_iota(jnp.int32, shape, 0) * 8 + lax.broadcasted_iota(jnp.int32, shape, 1)).astype(F32)


def _top1(x3, iota3):
    m = _over_sublanes(jnp.max(x3, axis=0), jnp.maximum)
    idx = _over_sublanes(jnp.min(jnp.where(x3 == m[None], iota3, float(8 * x3.shape[0])), axis=0), jnp.minimum)
    return m, idx


def _gelu_tanh(x):
    c1 = math.sqrt(2.0 / math.pi)
    half = 0.5 * x
    return half * jnp.tanh(x * (c1 + (c1 * 0.044715) * (x * x))) + half


def _peer_kernel(h2c_ref, h2n_ref, wq_ref, sk_ref, u_ref, v_ref, xmid_ref, g2_ref, fg_ref, o_ref,
                 acc_ref, act_ref, gm_ref, cand_ref, eid_ref, et_ref, gt_ref, a_ref, b_ref, g_ref, tmp_ref,
                 *, tm, per_token, final_norm):
    j = pl.program_id(1)

    @pl.when((pl.program_id(0) == 0) & (j == 0))
    def _():
        acc_ref[...] = jnp.zeros_like(acc_ref)
        act_ref[...] = jnp.zeros_like(act_ref)
        gm_ref[...] = jnp.zeros_like(gm_ref)

    q = _dot(h2n_ref[...], wq_ref[...]).astype(BF16)
    s12 = [_dot_nt(sk_ref[c], q[:, c * PEER_HALF:(c + 1) * PEER_HALF]) for c in range(2)]
    acc_ref[...] += _dot(act_ref[...], v_ref[...])
    s = _dot_nt(h2c_ref[...], u_ref[...])

    rows_per_block = u_ref.shape[0] // PEER_KEYS
    for r in range(rows_per_block):
        lanes = slice(r * PEER_KEYS, (r + 1) * PEER_KEYS)
        act_ref[:, lanes] = _gelu_tanh(s[:, lanes].astype(BF16)) * gm_ref[j * rows_per_block + r]

    key_iota = _row_iota(PEER_KEYS // 8, tm)
    s12 = [s_.reshape(PEER_KEYS // 8, 8, tm) for s_ in s12]
    vals = ([], [])
    idxs = ([], [])
    for _ in range(PEER_TOPK):
        for c in range(2):
            m, idx = _top1(s12[c], key_iota)
            s12[c] = jnp.where(key_iota == idx[None], -jnp.inf, s12[c])
            vals[c].append(m[0:1])
            idxs[c].append(idx[0:1])
    v1, v2 = (jnp.concatenate(v_, axis=0) for v_ in vals)
    i1, i2 = (jnp.concatenate(i_, axis=0) for i_ in idxs)

    cand_ref[...] = jnp.full(cand_ref.shape, -jnp.inf, F32)
    eid_ref[...] = jnp.zeros(eid_ref.shape, F32)
    for p in range(PEER_TOPK):
        cand_ref[pl.ds(CAND_OFF[p], CAND_LEN[p]), :] = v1[p:p + 1, :] + v2[:CAND_LEN[p], :]
        eid_ref[pl.ds(CAND_OFF[p], CAND_LEN[p]), :] = i1[p:p + 1, :] * float(PEER_KEYS) + i2[:CAND_LEN[p], :]
    cand_iota = _row_iota(CAND_ROWS // 8, tm)
    cand = cand_ref[...].reshape(CAND_ROWS // 8, 8, tm)
    eid = eid_ref[...].reshape(CAND_ROWS // 8, 8, tm)
    tops, picks = [], []
    for _ in range(PEER_TOPK):
        m, pos = _top1(cand, cand_iota)
        hit = cand_iota == pos[None]
        cand = jnp.where(hit, -jnp.inf, cand)
        tops.append(m[0:1])
        picks.append(_over_sublanes(jnp.sum(jnp.where(hit, eid, 0.0), axis=0), jnp.add)[0:1])
    top = jnp.concatenate(tops, axis=0)
    e = jnp.exp(top - top[0:1, :])
    row0 = pl.multiple_of(j * PEER_TOPK, PEER_TOPK)
    et_ref[pl.ds(row0, PEER_TOPK), :] = jnp.concatenate(picks, axis=0)
    gt_ref[pl.ds(row0, PEER_TOPK), :] = e / jnp.sum(e, axis=0, keepdims=True)

    @pl.when(j == 0)
    def _():
        g2 = g2_ref[...] if per_token else g2_ref[0]
        x = xmid_ref[...] + g2 * acc_ref[...]
        o_ref[...] = _rms(x, fg_ref[...]) if final_norm else x
        acc_ref[...] = jnp.zeros_like(acc_ref)

    @pl.when(j == PEER_HEADS - 1)
    def _():
        picked = et_ref[...]
        key1 = jnp.floor(picked * (1.0 / PEER_KEYS))
        a_ref[...] = key1.T
        b_ref[...] = (picked - key1 * float(PEER_KEYS)).T
        g_ref[...] = gt_ref[...].T
        j_iota = lax.broadcasted_iota(jnp.int32, (PEER_KEYS, PEER_HEADS * PEER_TOPK), 0).astype(F32)

        def token_groups(gi, carry):
            for part in range(GROUPS_PER_TRIP):
                t0 = pl.multiple_of((gi * GROUPS_PER_TRIP + part) * TOKEN_GROUP, TOKEN_GROUP)
                base = part * TOKEN_GROUP * TOKEN_PITCH
                for tt in range(TOKEN_GROUP):
                    t = t0 + tt
                    sel1 = jnp.where(j_iota == a_ref[pl.ds(t, 1), :], 1.0, 0.0).astype(BF16)
                    sel2 = jnp.where(j_iota == b_ref[pl.ds(t, 1), :], g_ref[pl.ds(t, 1), :], 0.0).astype(BF16)
                    tmp_ref[pl.ds(base + tt * TOKEN_PITCH, PEER_KEYS), :] = _dot_nt(sel1, sel2)
            for part in range(GROUPS_PER_TRIP):
                t0 = pl.multiple_of((gi * GROUPS_PER_TRIP + part) * TOKEN_GROUP, TOKEN_GROUP)
                base = part * TOKEN_GROUP * TOKEN_PITCH
                for r in range(PEER_KEYS):
                    rows = tmp_ref[pl.ds(base + r, TOKEN_GROUP, stride=TOKEN_PITCH), :]
                    gm_ref[r, pl.ds(t0, TOKEN_GROUP), :] = rows.astype(BF16)
            return carry

        lax.fori_loop(0, tm // (GROUPS_PER_TRIP * TOKEN_GROUP), token_groups, 0)


def _peer(h2, wq, sk, u, v, xmid, g2, fg, *, layer, tm, rows_per_seq, per_token, final_norm):
    n = h2.shape[0]
    ntiles = n // tm
    eb = PEER_EXPERTS // PEER_HEADS
    npick = PEER_HEADS * PEER_TOPK
    clamp = lambda t: jnp.clip(t, 0, ntiles - 1)
    routed = lambda g, j: clamp(g)
    stage_a = lambda g, j: clamp(g - 1)
    done = lambda g, j: clamp(g - 2 + jnp.minimum(j, 1))
    if per_token:
        ms = pl.BlockSpec((tm, D_MODEL), lambda g, j: (done(g, j), 0))
    else:
        ms = pl.BlockSpec((1, 1, D_MODEL), lambda g, j: (done(g, j) * tm // rows_per_seq, 0, 0))
    return pl.pallas_call(
        functools.partial(_peer_kernel, tm=tm, per_token=per_token, final_norm=final_norm),
        grid=(ntiles + 2, PEER_HEADS),
        in_specs=[pl.BlockSpec((tm, D_MODEL), lambda g, j: (stage_a(g, j), 0)),
                  pl.BlockSpec((tm, D_MODEL), lambda g, j: (routed(g, j), 0)),
                  pl.BlockSpec((D_MODEL, PEER_QDIM), lambda g, j: (0, j)),
                  pl.BlockSpec((2, PEER_KEYS, PEER_HALF), lambda g, j: (0, 0, 0)),
                  pl.BlockSpec((None, eb, D_MODEL), lambda g, j: (layer, j, 0)),
                  pl.BlockSpec((None, eb, D_MODEL), lambda g, j: (layer, (j + PEER_HEADS - 1) % PEER_HEADS, 0)),
                  pl.BlockSpec((tm, D_MODEL), lambda g, j: (done(g, j), 0)),
                  ms,
                  pl.BlockSpec((1, D_MODEL), lambda g, j: (0, 0))],
        out_specs=pl.BlockSpec((tm, D_MODEL), lambda g, j: (done(g, j), 0)),
        out_shape=jax.ShapeDtypeStruct((n, D_MODEL), F32),
        scratch_shapes=[pltpu.VMEM((tm, D_MODEL), F32),
                        pltpu.VMEM((tm, eb), BF16),
                        pltpu.VMEM((PEER_KEYS, tm, PEER_KEYS), BF16),
                        pltpu.VMEM((CAND_ROWS, tm), F32), pltpu.VMEM((CAND_ROWS, tm), F32),
                        pltpu.VMEM((npick, tm), F32), pltpu.VMEM((npick, tm), F32),
                        pltpu.VMEM((tm, npick), F32), pltpu.VMEM((tm, npick), F32), pltpu.VMEM((tm, npick), F32),
                        pltpu.VMEM((GROUPS_PER_TRIP * TOKEN_GROUP * TOKEN_PITCH, PEER_KEYS), F32)],
        compiler_params=_params(0, 2),
        name="peer",
    )(h2, h2, wq, sk, u, v, xmid, g2, fg)


TOKEN_TILE = 512
PEER_TILE = 256
SAMPLE_ROWS = 8
PROMPT_SEQS_PER_STEP = 4
SAMPLE_SEQS_PER_STEP = 8


def _lane_pad(vec, offset=0, width=LANES):
    return jnp.zeros((1, width), F32).at[0, offset:offset + vec.shape[0]].set(vec)


def _layer_weights(l, w):
    w_in = w['w_in'][l]
    o_m = RWKV_PROJ
    o_mg = o_m + MLSTM_MAIN
    o_s = o_mg + 2 * MLSTM_HEADS
    o_sg = o_s + SSD_MAIN
    gate_cols = jnp.zeros((D_MODEL, GATE_W), F32)
    gate_cols = gate_cols.at[:, 0:MLSTM_HEADS].set(w_in[:, o_mg:o_mg + MLSTM_HEADS])
    gate_cols = gate_cols.at[:, GATE_F:GATE_F + MLSTM_HEADS].set(w_in[:, o_mg + MLSTM_HEADS:o_s])
    gate_cols = gate_cols.at[:, GATE_DT:GATE_DT + SSD_HEADS].set(w_in[:, o_sg:o_sg + SSD_HEADS])
    w_cat = jnp.concatenate([w_in[:, :o_m], w_in[:, o_m:o_mg], w_in[:, o_s:o_sg], gate_cols], axis=1).astype(BF16)
    gate_bias = jnp.concatenate([_lane_pad(w['mlstm_i_b'][l]), _lane_pad(w['mlstm_f_b'][l]),
                                 _lane_pad(w['ssd_dt_bias'][l])], axis=1)
    row = lambda name: w[name][l].reshape(1, -1)
    return dict(
        w_cat=w_cat, gate_bias=gate_bias, w_out=w['w_out'][l].astype(BF16),
        norm1_g=row('norm1_g'), norm2_g=row('norm2_g'),
        rwkv=(row('rwkv_mu'), row('rwkv_w0'), w['rwkv_w_up'][l].astype(BF16), row('rwkv_a0'),
              w['rwkv_a_up'][l].astype(BF16), w['rwkv_g_up'][l].astype(BF16), row('rwkv_k_k'), row('rwkv_k_a'),
              row('rwkv_r_k'), row('rwkv_ln_g'), row('rwkv_ln_b')),
        mlstm=(w['mlstm_conv_w'][l], row('mlstm_conv_b'), gate_bias, row('mlstm_norm_g')),
        ssd=(w['ssd_conv_w'][l], row('ssd_conv_b'), gate_bias, _lane_pad(w['ssd_A_log'][l]),
             _lane_pad(w['ssd_D'][l]), row('ssd_norm_g')),
        wq=w['peer_wq'][l].astype(BF16), sk=w['peer_subkeys'][l].astype(BF16),
        u=w['peer_u_bf16'], v=w['peer_v_bf16'], layer=l,
    )


def _run_layer(x, mods, state, lw, head_sum, final_g, *, batch, seq, rows, nseq, per_token, final_norm):
    sh1, sc1, g1, sh2, sc2, g2 = mods
    shift, wkv, mconv, m_c, m_n, m_m, sconv, sst = state
    tm = min(TOKEN_TILE, batch * seq)
    tok = dict(tm=tm, rows_per_seq=seq, per_token=per_token)
    p_r, p_m, p_s, p_g = _inproj(x, sh1, sc1, lw['norm1_g'], lw['w_cat'], **tok)

    t_pad = -(-seq // rows) * rows
    t_valid = seq if seq < rows else rows

    def seqs(a):
        a = a.reshape(batch, seq, a.shape[-1])
        return a if t_pad == seq else jnp.pad(a, ((0, 0), (0, t_pad - seq), (0, 0)))

    def toks(a):
        return a[:, :seq].reshape(batch * seq, a.shape[-1])

    mix = dict(rows=rows, t_valid=t_valid, nseq=nseq)
    o_r, shift, wkv = _rwkv(seqs(p_r), shift[:, None, :], wkv, *lw['rwkv'], head_sum, **mix)
    m_m = jnp.pad(m_m, ((0, 0), (0, LANES - MLSTM_HEADS)))[:, None, :]
    o_m, mconv, m_c, m_n, m_m = _mlstm(seqs(p_m), seqs(p_g), mconv, m_c, m_n, m_m, *lw['mlstm'], **mix)
    o_s, sconv, sst = _ssd(seqs(p_s), seqs(p_g), sconv, sst, *lw['ssd'], **mix)
    new_state = (shift[:, 0, :], wkv, mconv, m_c, m_n, m_m[:, 0, :MLSTM_HEADS], sconv, sst)

    x_mid, h2 = _outproj(toks(o_r), toks(o_m), toks(o_s), x, g1, sh2, sc2, lw['norm2_g'], lw['w_out'], **tok)
    x_new = _peer(h2, lw['wq'], lw['sk'], lw['u'], lw['v'], x_mid, g2, final_g, layer=lw['layer'],
                  tm=min(PEER_TILE, batch * seq), rows_per_seq=seq, per_token=per_token, final_norm=final_norm)
    return x_new, new_state


def kernel(x_prompt, x_sample, state_rwkv_shift, state_rwkv_wkv, state_mlstm_conv, state_mlstm_C, state_mlstm_n, state_mlstm_m, state_ssd_conv, state_ssd, c_prompt, c_sample, ada_w, ada_b, norm1_g, norm2_g, w_in, w_out, rwkv_mu, rwkv_w0, rwkv_w_up, rwkv_a0, rwkv_a_up, rwkv_g_up, rwkv_k_k, rwkv_k_a, rwkv_r_k, rwkv_ln_g, rwkv_ln_b, mlstm_conv_w, mlstm_conv_b, mlstm_i_b, mlstm_f_b, mlstm_norm_g, ssd_conv_w, ssd_conv_b, ssd_dt_bias, ssd_A_log, ssd_D, ssd_norm_g, peer_wq, peer_subkeys, peer_u, peer_v, final_g):
    weights = dict(norm1_g=norm1_g, norm2_g=norm2_g, w_in=w_in, w_out=w_out, rwkv_mu=rwkv_mu, rwkv_w0=rwkv_w0,
                   rwkv_w_up=rwkv_w_up, rwkv_a0=rwkv_a0, rwkv_a_up=rwkv_a_up, rwkv_g_up=rwkv_g_up,
                   rwkv_k_k=rwkv_k_k, rwkv_k_a=rwkv_k_a, rwkv_r_k=rwkv_r_k, rwkv_ln_g=rwkv_ln_g,
                   rwkv_ln_b=rwkv_ln_b, mlstm_conv_w=mlstm_conv_w, mlstm_conv_b=mlstm_conv_b,
                   mlstm_i_b=mlstm_i_b, mlstm_f_b=mlstm_f_b, mlstm_norm_g=mlstm_norm_g, ssd_conv_w=ssd_conv_w,
                   ssd_conv_b=ssd_conv_b, ssd_dt_bias=ssd_dt_bias, ssd_A_log=ssd_A_log, ssd_D=ssd_D,
                   ssd_norm_g=ssd_norm_g, peer_wq=peer_wq, peer_subkeys=peer_subkeys, peer_u=peer_u, peer_v=peer_v)
    weights['peer_u_bf16'] = peer_u.astype(BF16)
    weights['peer_v_bf16'] = peer_v.astype(BF16)
    nb, seq, d = x_prompt.shape
    nd, dseq, _ = x_sample.shape
    cache = (state_rwkv_shift, state_rwkv_wkv, state_mlstm_conv, state_mlstm_C, state_mlstm_n, state_mlstm_m,
             state_ssd_conv, state_ssd)

    mod = _ada_mod(jnp.concatenate([c_prompt, c_sample], axis=0), ada_w, ada_b)
    head_id = jnp.arange(RWKV_W) // HEAD_DIM
    head_sum = (head_id[:, None] == head_id[None, :]).astype(F32)
    fg = final_g.reshape(1, d)

    xp = x_prompt.reshape(nb * seq, d)
    xs = x_sample.reshape(nd * dseq, d)
    new_p, new_s = [], []
    for l in range(DEPTH):
        lw = _layer_weights(l, weights)
        last = l == DEPTH - 1
        mod_p = mod[l, :nb].reshape(nb, 6, 1, d)
        mods_p = tuple(mod_p[:, i] for i in range(6))
        mod_s = jnp.repeat(mod[l, nb:].reshape(nd, 6, d), dseq, axis=0)
        mods_s = tuple(mod_s[:, i] for i in range(6))
        zeros = tuple(jnp.zeros((nb,) + s.shape[2:], F32) for s in cache)
        xp, sp = _run_layer(xp, mods_p, zeros, lw, head_sum, fg, batch=nb, seq=seq, rows=CHUNK,
                            nseq=PROMPT_SEQS_PER_STEP, per_token=False, final_norm=last)
        xs, ss = _run_layer(xs, mods_s, tuple(s[l] for s in cache), lw, head_sum, fg, batch=nd, seq=dseq,
                            rows=SAMPLE_ROWS, nseq=SAMPLE_SEQS_PER_STEP, per_token=True, final_norm=last)
        new_p.append(sp)
        new_s.append(ss)

    def stk(states, i):
        return jnp.stack([st[i] for st in states])

    return ((xp.reshape(nb, seq, d), xs.reshape(nd, dseq, d))
            + tuple(stk(new_p, i) for i in range(8)) + tuple(stk(new_s, i) for i in range(8)))
```

```python
import functools
import math

import jax
import jax.numpy as jnp
from jax import lax
from jax.experimental import pallas as pl
from jax.experimental.pallas import tpu as pltpu

F32 = jnp.float32
BF16 = jnp.bfloat16

D_MODEL = 1024
DEPTH = 4
HEAD_DIM = 64
RWKV_W = 256
RWKV_HEADS = 4
RWKV_DECAY_RANK = 64
RWKV_ICLR_RANK = 64
RWKV_GATE_RANK = 128
RWKV_LN_EPS = 1e-5 * HEAD_DIM
RWKV_PROJ = 3 * RWKV_W + RWKV_DECAY_RANK + RWKV_ICLR_RANK + RWKV_GATE_RANK
MLSTM_W = 256
MLSTM_HEADS = 4
MLSTM_CONV_W = 2 * MLSTM_W
MLSTM_MAIN = MLSTM_CONV_W + 2 * MLSTM_W
SSD_W = 512
SSD_HEADS = 8
SSD_STATE = 128
SSD_GROUPS = 2
SSD_HEADS_PER_GROUP = SSD_HEADS // SSD_GROUPS
SSD_CONV_W = SSD_W + 2 * SSD_GROUPS * SSD_STATE
SSD_MAIN = SSD_W + SSD_CONV_W
CONV_WIDTH = 4
CHUNK = 64
LANES = 128
GATE_I, GATE_F, GATE_DT = 0, LANES, 2 * LANES
GATE_W = 3 * LANES
PEER_KEYS = 128
PEER_EXPERTS = PEER_KEYS * PEER_KEYS
PEER_HEADS = 8
PEER_TOPK = 16
PEER_QDIM = 256
PEER_HALF = 128
NORM_EPS = 1e-6
NEG_BIG = -1e30
TOKEN_GROUP = 16
GROUPS_PER_TRIP = 8
TOKEN_PITCH = PEER_KEYS + 4

VMEM_LIMIT_BYTES = 56 * 1024 * 1024
HIST = 8


def _dot(a, b, prec=None):
    return jnp.dot(a, b, preferred_element_type=F32, precision=prec)


def _dot_nt(a, b, prec=None):
    return lax.dot_general(a, b, (((1,), (1,)), ((), ())), preferred_element_type=F32, precision=prec)


def _dot_tn(a, b, prec=None):
    return lax.dot_general(a, b, (((0,), (0,)), ((), ())), preferred_element_type=F32, precision=prec)


def _hi_lo(x):
    hi = x.astype(BF16)
    return hi, (x - hi.astype(F32)).astype(BF16)


def _mm(a, b, form, b_exact=False, single=False):
    dot = {'nn': _dot, 'nt': _dot_nt, 'tn': _dot_tn}[form]
    if single:
        return dot(a.astype(BF16), b.astype(BF16))
    ah, al = _hi_lo(a)
    a_axis = 0 if form == 'tn' else 1
    b_axis = 1 if form == 'nt' else 0
    if b_exact:
        bb = b.astype(BF16)
        a3 = jnp.concatenate([ah, al], axis=a_axis)
        b3 = jnp.concatenate([bb, bb], axis=b_axis)
    else:
        bh, bl = _hi_lo(b)
        a3 = jnp.concatenate([ah, ah, al], axis=a_axis)
        b3 = jnp.concatenate([bh, bl, bh], axis=b_axis)
    return dot(a3, b3)


def _cumsum_rows(x, tril_bf16):
    hi = x.astype(BF16)
    r1 = x - hi.astype(F32)
    mid = r1.astype(BF16)
    lo = (r1 - mid.astype(F32)).astype(BF16)
    return _dot(jnp.concatenate([tril_bf16] * 3, axis=1), jnp.concatenate([hi, mid, lo], axis=0))


def _sigmoid(x):
    return 1.0 / (1.0 + jnp.exp(-x))


def _silu(x):
    return x * _sigmoid(x)


def _softplus(x):
    return jnp.maximum(x, 0.0) + jnp.log(1.0 + jnp.exp(-jnp.abs(x)))


def _tri(n, strict=False):
    r = lax.broadcasted_iota(jnp.int32, (n, n), 0)
    c = lax.broadcasted_iota(jnp.int32, (n, n), 1)
    return (r > c) if strict else (r >= c)


def _to_row(col, eye):
    return jnp.sum(jnp.where(eye, col, 0.0), axis=0, keepdims=True)


def _params(n_parallel, n_arbitrary=0):
    sem = ("parallel",) * n_parallel + ("arbitrary",) * n_arbitrary
    return pltpu.CompilerParams(dimension_semantics=sem, vmem_limit_bytes=VMEM_LIMIT_BYTES)


def _ada_kernel(c_ref, w_ref, b_ref, o_ref):
    c = _silu(c_ref[...]).astype(BF16)
    o_ref[0] = _dot(c, w_ref[0].astype(BF16)) + b_ref[0]


def _ada_mod(c_all, ada_w, ada_b):
    nb = c_all.shape[0]
    tn = 1536
    return pl.pallas_call(
        _ada_kernel,
        grid=(DEPTH, 6 * D_MODEL // tn),
        in_specs=[pl.BlockSpec((nb, D_MODEL), lambda l, j: (0, 0)),
                  pl.BlockSpec((1, D_MODEL, tn), lambda l, j: (l, 0, j)),
                  pl.BlockSpec((1, 1, tn), lambda l, j: (l, 0, j))],
        out_specs=pl.BlockSpec((1, nb, tn), lambda l, j: (l, 0, j)),
        out_shape=jax.ShapeDtypeStruct((DEPTH, nb, 6 * D_MODEL), F32),
        compiler_params=_params(2),
        name="ada_mod",
    )(c_all, ada_w, ada_b.reshape(DEPTH, 1, 6 * D_MODEL))


def _rms(x, g):
    return x * lax.rsqrt(jnp.mean(x * x, axis=-1, keepdims=True) + NORM_EPS) * g


def _inproj_kernel(x_ref, sh_ref, sc_ref, g_ref, w_ref, pr_ref, pm_ref, ps_ref, pg_ref, *, per_token):
    sh = sh_ref[...] if per_token else sh_ref[0]
    sc = sc_ref[...] if per_token else sc_ref[0]
    h = _rms(x_ref[...], g_ref[...]) * (1.0 + sc) + sh
    p = _dot(h.astype(BF16), w_ref[...])
    o = 0
    for ref, w in ((pr_ref, RWKV_PROJ), (pm_ref, MLSTM_MAIN), (ps_ref, SSD_MAIN), (pg_ref, GATE_W)):
        ref[...] = p[:, o:o + w]
        o += w


def _mod_spec(tm, rows_per_seq, per_token):
    if per_token:
        return pl.BlockSpec((tm, D_MODEL), lambda i: (i, 0))
    return pl.BlockSpec((1, 1, D_MODEL), lambda i: (i * tm // rows_per_seq, 0, 0))


def _inproj(x, sh, sc, g, w, *, tm, rows_per_seq, per_token):
    n = x.shape[0]
    widths = (RWKV_PROJ, MLSTM_MAIN, SSD_MAIN, GATE_W)
    wtot = sum(widths)
    return pl.pallas_call(
        functools.partial(_inproj_kernel, per_token=per_token),
        grid=(n // tm,),
        in_specs=[pl.BlockSpec((tm, D_MODEL), lambda i: (i, 0)),
                  _mod_spec(tm, rows_per_seq, per_token), _mod_spec(tm, rows_per_seq, per_token),
                  pl.BlockSpec((1, D_MODEL), lambda i: (0, 0)),
                  pl.BlockSpec((D_MODEL, wtot), lambda i: (0, 0))],
        out_specs=[pl.BlockSpec((tm, w_), lambda i: (i, 0)) for w_ in widths],
        out_shape=[jax.ShapeDtypeStruct((n, w_), F32) for w_ in widths],
        compiler_params=_params(1),
        name="in_proj",
    )(x, sh, sc, g, w)


HIST0 = HIST - (CONV_WIDTH - 1)


def _causal_conv(ext_ref, s, u, w_ref, b_ref, rows, t_valid, new_buf_ref):
    ext_ref[s, pl.ds(HIST, rows), :] = u
    out = b_ref[...]
    for j in range(CONV_WIDTH):
        out = out + ext_ref[s, pl.ds(HIST0 + j, rows), :] * w_ref[pl.ds(j, 1), :]
    last = ext_ref[s, pl.ds(HIST0 + t_valid, CONV_WIDTH - 1), :]
    new_buf_ref[s] = last
    ext_ref[s, pl.ds(HIST0, CONV_WIDTH - 1), :] = last
    return out


def _mixer_specs(nseq, rows):
    chunk = lambda w, col=0: pl.BlockSpec((nseq, rows, w), lambda i, c: (i, c, col))
    full = lambda *s: pl.BlockSpec(s, lambda i, c: (0,) * len(s))
    per_seq = lambda *s: pl.BlockSpec((nseq,) + s, lambda i, c: (i,) + (0,) * len(s))
    return chunk, full, per_seq


def _row_mask(rows, t_valid):
    if t_valid == rows:
        return None
    return lax.broadcasted_iota(jnp.int32, (rows, 1), 0) < t_valid


def _mlstm_kernel(pm_ref, pg_ref, conv0_ref, c0_ref, n0_ref, m0_ref, cw_ref, cb_ref, gb_ref, ng_ref,
                  out_ref, conv_ref, c_ref, n_ref, m_ref, ext_ref, *, rows, t_valid, nseq):
    @pl.when(pl.program_id(1) == 0)
    def _():
        ext_ref[:, pl.ds(HIST0, CONV_WIDTH - 1), :] = conv0_ref[...]
        c_ref[...] = c0_ref[...]
        n_ref[...] = n0_ref[...]
        m_ref[...] = m0_ref[...]

    valid = _row_mask(rows, t_valid)
    tril = _tri(rows)
    tril_b = tril.astype(BF16)
    eye = lax.broadcasted_iota(jnp.int32, (rows, rows), 0) == lax.broadcasted_iota(jnp.int32, (rows, rows), 1)
    gb = gb_ref[...]

    seqs = []
    for s in range(nseq):
        pm = pm_ref[s]
        qk = _silu(_causal_conv(ext_ref, s, pm[:, :MLSTM_CONV_W], cw_ref, cb_ref, rows, t_valid, conv_ref))
        gates = pg_ref[s]
        ig = gates[:, GATE_I:GATE_I + LANES] + gb[:, GATE_I:GATE_I + LANES]
        fg = gates[:, GATE_F:GATE_F + LANES] + gb[:, GATE_F:GATE_F + LANES]
        logf = -_softplus(-fg)
        if valid is not None:
            ig = jnp.where(valid, ig, NEG_BIG)
            logf = jnp.where(valid, logf, 0.0)
        bcum = _cumsum_rows(logf, tril_b)
        m_prev = m_ref[s]
        b_end = bcum[rows - 1:rows, :]
        ws = b_end - bcum + ig
        m_new = jnp.maximum(b_end + m_prev, jnp.max(ws, axis=0, keepdims=True))
        m_ref[s] = m_new
        seqs.append(dict(q=qk[:, :MLSTM_W], k=qk[:, MLSTM_W:] * HEAD_DIM ** -0.5,
                         v=pm[:, MLSTM_CONV_W:MLSTM_CONV_W + MLSTM_W], o=pm[:, MLSTM_CONV_W + MLSTM_W:],
                         ig=ig, bcum=bcum, inter=bcum + m_prev, ws=jnp.exp(ws - m_new),
                         a_end=jnp.exp(b_end + m_prev - m_new)))

    chains = [(s, h) for s in range(nseq) for h in range(MLSTM_HEADS)]
    lanes = lambda c: slice(c[1] * HEAD_DIM, (c[1] + 1) * HEAD_DIM)
    col = lambda name, c: seqs[c[0]][name][:, c[1]:c[1] + 1]
    q = [seqs[c[0]]['q'][:, lanes(c)] for c in chains]
    k = [seqs[c[0]]['k'][:, lanes(c)] for c in chains]
    v = [seqs[c[0]]['v'][:, lanes(c)] for c in chains]
    c0 = [c_ref[c[0], c[1]] for c in chains]
    n0 = [n_ref[c[0], pl.ds(c[1], 1), :] for c in chains]
    qk_t = [_dot_nt(q_, k_) for q_, k_ in zip(q, k)]
    q_c = [_dot_nt(q_, c_) for q_, c_ in zip(q, c0)]
    c_add = [_dot_tn(col('ws', c) * v_, k_) for c, v_, k_ in zip(chains, v, k)]
    dmat = [jnp.where(tril, col('bcum', c) - _to_row(col('bcum', c), eye) + _to_row(col('ig', c), eye), NEG_BIG)
            for c in chains]
    m_t = [jnp.maximum(col('inter', c), jnp.max(d_, axis=1, keepdims=True)) for c, d_ in zip(chains, dmat)]
    sc = [g_ * jnp.exp(d_ - m_) for g_, d_, m_ in zip(qk_t, dmat, m_t)]
    s_v = [_dot(s_, v_) for s_, v_ in zip(sc, v)]
    ones = jnp.ones((rows, HEAD_DIM), BF16)
    ones_hh = jnp.ones((HEAD_DIM, HEAD_DIM), BF16)
    s_sum = [_mm(s_, ones, 'nn', b_exact=True) for s_ in sc]
    q_n = [_mm(q_, jnp.broadcast_to(n_, (HEAD_DIM, HEAD_DIM)), 'nt') for q_, n_ in zip(q, n0)]
    hh = []
    for i, c in enumerate(chains):
        a_in = jnp.exp(col('inter', c) - m_t[i])
        num = s_v[i] + a_in * q_c[i]
        den = s_sum[i] + a_in * q_n[i]
        hh.append(num / jnp.maximum(jnp.abs(den), jnp.exp(-m_t[i])))
        ae = col('a_end', c)
        c_ref[c[0], c[1]] = ae * c0[i] + c_add[i]
        n_ref[c[0], pl.ds(c[1], 1), :] = ae * n0[i] + jnp.sum(col('ws', c) * k[i], axis=0, keepdims=True)
    inv_n = 1.0 / HEAD_DIM
    hc = [h_ - _mm(h_, ones_hh, 'nn', b_exact=True) * inv_n for h_ in hh]
    var = [_mm(h_ * h_, ones_hh, 'nn', b_exact=True) * inv_n for h_ in hc]
    for i, c in enumerate(chains):
        hn = hc[i] * lax.rsqrt(var[i] + NORM_EPS)
        out_ref[c[0], :, lanes(c)] = hn * ng_ref[:, lanes(c)] * _sigmoid(seqs[c[0]]['o'][:, lanes(c)])


def _mlstm(pm, pg, conv0, c0, n0, m0, cw, cb, gb, ng, *, rows, t_valid, nseq):
    b, t, _ = pm.shape
    chunk, full, per_seq = _mixer_specs(nseq, rows)
    state_specs = [per_seq(CONV_WIDTH - 1, MLSTM_CONV_W), per_seq(MLSTM_HEADS, HEAD_DIM, HEAD_DIM),
                   per_seq(MLSTM_HEADS, HEAD_DIM), per_seq(1, LANES)]
    return pl.pallas_call(
        functools.partial(_mlstm_kernel, rows=rows, t_valid=t_valid, nseq=nseq),
        grid=(b // nseq, t // rows),
        in_specs=[chunk(MLSTM_MAIN), chunk(2 * LANES)] + state_specs
                 + [full(CONV_WIDTH, MLSTM_CONV_W), full(1, MLSTM_CONV_W), full(1, GATE_W), full(1, MLSTM_W)],
        out_specs=[chunk(MLSTM_W)] + state_specs,
        out_shape=[jax.ShapeDtypeStruct((b, t, MLSTM_W), F32),
                   jax.ShapeDtypeStruct((b, CONV_WIDTH - 1, MLSTM_CONV_W), F32),
                   jax.ShapeDtypeStruct((b, MLSTM_HEADS, HEAD_DIM, HEAD_DIM), F32),
                   jax.ShapeDtypeStruct((b, MLSTM_HEADS, HEAD_DIM), F32),
                   jax.ShapeDtypeStruct((b, 1, LANES), F32)],
        scratch_shapes=[pltpu.VMEM((nseq, HIST + rows, MLSTM_CONV_W), F32)],
        compiler_params=_params(1, 1),
        name="mlstm_mix",
    )(pm, pg, conv0, c0, n0, m0, cw, cb, gb, ng)


def _ssd_kernel(ps_ref, dt_ref, conv0_ref, h0_ref, cw_ref, cb_ref, gb_ref, alog_ref, dskip_ref, ng_ref,
                out_ref, conv_ref, h_ref, ext_ref, y_ref, *, rows, t_valid, nseq):
    @pl.when(pl.program_id(1) == 0)
    def _():
        ext_ref[:, pl.ds(HIST0, CONV_WIDTH - 1), :] = conv0_ref[...]
        h_ref[...] = h0_ref[...]

    valid = _row_mask(rows, t_valid)
    tril = _tri(rows)
    tril_b = tril.astype(BF16)
    eye = lax.broadcasted_iota(jnp.int32, (rows, rows), 0) == lax.broadcasted_iota(jnp.int32, (rows, rows), 1)
    dskip = dskip_ref[...]
    neg_a = -jnp.exp(alog_ref[...])
    gs = SSD_GROUPS * SSD_STATE

    seqs = []
    for s in range(nseq):
        ps = ps_ref[s]
        xbc = _silu(_causal_conv(ext_ref, s, ps[:, SSD_W:], cw_ref, cb_ref, rows, t_valid, conv_ref))
        dt = _softplus(dt_ref[s] + gb_ref[:, GATE_DT:GATE_DT + LANES])
        if valid is not None:
            dt = jnp.where(valid, dt, 0.0)
        seqs.append(dict(z=ps[:, :SSD_W], xs=xbc[:, :SSD_W], bm=xbc[:, SSD_W:SSD_W + gs], cm=xbc[:, SSD_W + gs:],
                         dt=dt, acum=_cumsum_rows(dt * neg_a, tril_b)))

    groups = [(s, g) for s in range(nseq) for g in range(SSD_GROUPS)]
    grp = lambda name, sg: seqs[sg[0]][name][:, sg[1] * SSD_STATE:(sg[1] + 1) * SSD_STATE]
    cb = {sg: _dot_nt(grp('cm', sg), grp('bm', sg)) for sg in groups}
    chains = [(s, hd) for s in range(nseq) for hd in range(SSD_HEADS)]
    lanes = lambda c: slice(c[1] * HEAD_DIM, (c[1] + 1) * HEAD_DIM)
    col = lambda name, c: seqs[c[0]][name][:, c[1]:c[1] + 1]
    group_of = lambda c: (c[0], c[1] // SSD_HEADS_PER_GROUP)
    x = [seqs[c[0]]['xs'][:, lanes(c)] for c in chains]
    h0 = [h_ref[c[0], c[1]] for c in chains]
    from_state = [_dot_nt(grp('cm', group_of(c)), h_) for c, h_ in zip(chains, h0)]
    h_add = [_dot_tn(jnp.exp(col('acum', c)[rows - 1:rows, :] - col('acum', c)) * col('dt', c) * x_,
                     grp('bm', group_of(c))) for c, x_ in zip(chains, x)]
    wmat = [jnp.exp(jnp.where(tril, col('acum', c) - _to_row(col('acum', c), eye), NEG_BIG))
            * cb[group_of(c)] * _to_row(col('dt', c), eye) for c in chains]
    y = [_dot(w_, x_) for w_, x_ in zip(wmat, x)]
    for i, c in enumerate(chains):
        a_col = col('acum', c)
        h_ref[c[0], c[1]] = jnp.exp(a_col[rows - 1:rows, :]) * h0[i] + h_add[i]
        y_ref[c[0], :, lanes(c)] = y[i] + jnp.exp(a_col) * from_state[i] + dskip[:, c[1]:c[1] + 1] * x[i]
    for s in range(nseq):
        out_ref[s] = _rms(y_ref[s] * _silu(seqs[s]['z']), ng_ref[...])


def _ssd(ps, pg, conv0, h0, cw, cb, gb, alog, dskip, ng, *, rows, t_valid, nseq):
    b, t, _ = ps.shape
    chunk, full, per_seq = _mixer_specs(nseq, rows)
    state_specs = [per_seq(CONV_WIDTH - 1, SSD_CONV_W), per_seq(SSD_HEADS, HEAD_DIM, SSD_STATE)]
    return pl.pallas_call(
        functools.partial(_ssd_kernel, rows=rows, t_valid=t_valid, nseq=nseq),
        grid=(b // nseq, t // rows),
        in_specs=[chunk(SSD_MAIN), chunk(LANES, col=GATE_DT // LANES)] + state_specs
                 + [full(CONV_WIDTH, SSD_CONV_W), full(1, SSD_CONV_W), full(1, GATE_W),
                    full(1, LANES), full(1, LANES), full(1, SSD_W)],
        out_specs=[chunk(SSD_W)] + state_specs,
        out_shape=[jax.ShapeDtypeStruct((b, t, SSD_W), F32),
                   jax.ShapeDtypeStruct((b, CONV_WIDTH - 1, SSD_CONV_W), F32),
                   jax.ShapeDtypeStruct((b, SSD_HEADS, HEAD_DIM, SSD_STATE), F32)],
        scratch_shapes=[pltpu.VMEM((nseq, HIST + rows, SSD_CONV_W), F32), pltpu.VMEM((nseq, rows, SSD_W), F32)],
        compiler_params=_params(1, 1),
        name="ssd_mix",
    )(ps, pg, conv0, h0, cw, cb, gb, alog, dskip, ng)


def _rwkv_kernel(pr_ref, shift0_ref, s0_ref, mu_ref, w0_ref, wup_ref, a0_ref, aup_ref, gup_ref, kk_ref, ka_ref,
                 rk_ref, lng_ref, lnb_ref, bd_ref,
                 out_ref, shift_ref, s_ref, ext_ref, y_ref, *, rows, t_valid, nseq):
    @pl.when(pl.program_id(1) == 0)
    def _():
        ext_ref[:, pl.ds(HIST - 1, 1), :] = shift0_ref[...]
        s_ref[...] = s0_ref[...]

    tril = _tri(rows)
    stril = _tri(rows, strict=True)
    bd = bd_ref[...]
    head_sum = lambda t: _mm(t, bd, 'nn', b_exact=True)
    w3 = 3 * RWKV_W
    n = nseq * rows

    p3 = pr_ref[...]
    ext_ref[:, pl.ds(HIST, rows), :] = p3
    prev = ext_ref[:, pl.ds(HIST - 1, rows), :].reshape(n, RWKV_PROJ)
    last = ext_ref[:, pl.ds(HIST - 1 + t_valid, 1), :]
    shift_ref[...] = last
    ext_ref[:, pl.ds(HIST - 1, 1), :] = last
    p = p3.reshape(n, RWKV_PROJ)
    x = p + (prev - p) * mu_ref[...]
    r = x[:, :RWKV_W]
    k = x[:, RWKV_W:2 * RWKV_W]
    v = x[:, 2 * RWKV_W:w3]
    xw = x[:, w3:w3 + RWKV_DECAY_RANK]
    xa = x[:, w3 + RWKV_DECAY_RANK:w3 + RWKV_DECAY_RANK + RWKV_ICLR_RANK]
    xg = x[:, w3 + RWKV_DECAY_RANK + RWKV_ICLR_RANK:]
    w_log = -_softplus(-(w0_ref[...] + _dot(jnp.tanh(xw).astype(BF16), wup_ref[...]))) - 0.5
    logw = -jnp.exp(w_log)
    a = _sigmoid(a0_ref[...] + _dot(xa.astype(BF16), aup_ref[...]))
    g = _dot(_sigmoid(xg).astype(BF16), gup_ref[...])
    kk = k * kk_ref[...]
    kk = kk / jnp.maximum(jnp.sqrt(head_sum(kk * kk)), 1e-12)
    k = k * (1.0 + (a - 1.0) * ka_ref[...])
    ri = lax.broadcasted_iota(jnp.int32, (n, n), 0)
    ci = lax.broadcasted_iota(jnp.int32, (n, n), 1)
    same_seq = (ri // rows) == (ci // rows)
    if t_valid != rows:
        valid = (lax.broadcasted_iota(jnp.int32, (n, 1), 0) % rows) < t_valid
        logw = jnp.where(valid, logw, 0.0)
        kk = jnp.where(valid, kk, 0.0)
        k = jnp.where(valid, k, 0.0)

    cum = _cumsum_rows(logw, (same_seq & (ri >= ci)).astype(BF16))
    to_end = jnp.exp(_cumsum_rows(logw, (same_seq & (ri < ci)).astype(BF16)))
    p_inv = jnp.exp(-cum)
    a_t = -kk * jnp.exp(cum - logw)
    kka = kk * a
    b_t = kka * p_inv
    k_t = k * p_inv
    r_t = r * jnp.exp(cum)
    b_e = kka * to_end
    k_e = k * to_end
    p_end = jnp.exp(cum)

    chains = [(s, h) for s in range(nseq) for h in range(RWKV_HEADS)]
    blk = lambda t, c: t[c[0] * rows:(c[0] + 1) * rows, c[1] * HEAD_DIM:(c[1] + 1) * HEAD_DIM]
    s0 = [s_ref[c[0], c[1]] for c in chains]
    vh = [blk(v, c) for c in chains]
    ar = [jnp.concatenate([blk(a_t, c), blk(r_t, c)], axis=0) for c in chains]
    bk = [jnp.concatenate([blk(b_t, c), blk(k_t, c)], axis=0) for c in chains]
    mm = functools.partial(_mm, single=True)
    gram = [mm(x_, y_, 'nt') for x_, y_ in zip(ar, bk)]
    from_state = [mm(x_, y_, 'nt') for x_, y_ in zip(ar, s0)]
    n_mat = [jnp.where(stril, g_[:rows, :rows], 0.0) for g_ in gram]
    u = [f_[:rows] + mm(jnp.where(stril, g_[:rows, rows:], 0.0), v_, 'nn')
         for f_, g_, v_ in zip(from_state, gram, vh)]
    span = 1
    while span < rows:
        u = [u_ + mm(n_, u_, 'nn') for u_, n_ in zip(u, n_mat)]
        span *= 2
        if span < rows:
            n_mat = [mm(n_, n_, 'nn') for n_ in n_mat]
    y_u = [mm(jnp.where(tril, g_[rows:, :rows], 0.0), u_, 'nn') for g_, u_ in zip(gram, u)]
    y_v = [mm(jnp.where(tril, g_[rows:, rows:], 0.0), v_, 'nn') for g_, v_ in zip(gram, vh)]
    s_add = [mm(jnp.concatenate([u_, v_], axis=0), jnp.concatenate([blk(b_e, c), blk(k_e, c)], axis=0), 'tn')
             for u_, v_, c in zip(u, vh, chains)]
    for i, c in enumerate(chains):
        lanes = slice(c[1] * HEAD_DIM, (c[1] + 1) * HEAD_DIM)
        y_ref[c[0], :, lanes] = from_state[i][rows:] + y_u[i] + y_v[i]
        end_row = (c[0] + 1) * rows - 1
        s_ref[c[0], c[1]] = p_end[end_row:end_row + 1, lanes] * s0[i] + s_add[i]

    y = y_ref[...].reshape(n, RWKV_W)
    inv_n = 1.0 / HEAD_DIM
    yc = y - head_sum(y) * inv_n
    yn = yc * lax.rsqrt(head_sum(yc * yc) * inv_n + RWKV_LN_EPS)
    bonus = head_sum(r * k * rk_ref[...]) * v
    out_ref[...] = ((yn * lng_ref[...] + lnb_ref[...] + bonus) * g).reshape(nseq, rows, RWKV_W)


def _rwkv(pr, shift0, s0, mu, w0, wup, a0, aup, gup, kkp, kap, rk, lng, lnb, bd, *, rows, t_valid, nseq):
    b, t, _ = pr.shape
    chunk, full, per_seq = _mixer_specs(nseq, rows)
    state_specs = [per_seq(1, RWKV_PROJ), per_seq(RWKV_HEADS, HEAD_DIM, HEAD_DIM)]
    return pl.pallas_call(
        functools.partial(_rwkv_kernel, rows=rows, t_valid=t_valid, nseq=nseq),
        grid=(b // nseq, t // rows),
        in_specs=[chunk(RWKV_PROJ)] + state_specs
                 + [full(1, RWKV_PROJ), full(1, RWKV_W), full(RWKV_DECAY_RANK, RWKV_W), full(1, RWKV_W),
                    full(RWKV_ICLR_RANK, RWKV_W), full(RWKV_GATE_RANK, RWKV_W), full(1, RWKV_W), full(1, RWKV_W),
                    full(1, RWKV_W), full(1, RWKV_W), full(1, RWKV_W), full(RWKV_W, RWKV_W)],
        out_specs=[chunk(RWKV_W)] + state_specs,
        out_shape=[jax.ShapeDtypeStruct((b, t, RWKV_W), F32),
                   jax.ShapeDtypeStruct((b, 1, RWKV_PROJ), F32),
                   jax.ShapeDtypeStruct((b, RWKV_HEADS, HEAD_DIM, HEAD_DIM), F32)],
        scratch_shapes=[pltpu.VMEM((nseq, HIST + rows, RWKV_PROJ), F32), pltpu.VMEM((nseq, rows, RWKV_W), F32)],
        compiler_params=_params(1, 1),
        name="rwkv_mix",
    )(pr, shift0, s0, mu, w0, wup, a0, aup, gup, kkp, kap, rk, lng, lnb, bd)


def _outproj_kernel(or_ref, om_ref, os_ref, x_ref, g1_ref, sh_ref, sc_ref, ng_ref, w_ref, xmid_ref, h2_ref,
                    *, per_token):
    pick = (lambda r: r[...]) if per_token else (lambda r: r[0])
    w = w_ref[...]
    mix = (_dot(or_ref[...].astype(BF16), w[:RWKV_W])
           + _dot(om_ref[...].astype(BF16), w[RWKV_W:RWKV_W + MLSTM_W])
           + _dot(os_ref[...].astype(BF16), w[RWKV_W + MLSTM_W:]))
    x = x_ref[...] + pick(g1_ref) * mix
    xmid_ref[...] = x
    h2_ref[...] = (_rms(x, ng_ref[...]) * (1.0 + pick(sc_ref)) + pick(sh_ref)).astype(BF16)


def _outproj(o_r, o_m, o_s, x, g1, sh, sc, ng, w, *, tm, rows_per_seq, per_token):
    n = x.shape[0]
    tok = lambda w_: pl.BlockSpec((tm, w_), lambda i: (i, 0))
    ms = _mod_spec(tm, rows_per_seq, per_token)
    return pl.pallas_call(
        functools.partial(_outproj_kernel, per_token=per_token),
        grid=(n // tm,),
        in_specs=[tok(RWKV_W), tok(MLSTM_W), tok(SSD_W), tok(D_MODEL), ms, ms, ms,
                  pl.BlockSpec((1, D_MODEL), lambda i: (0, 0)),
                  pl.BlockSpec((D_MODEL, D_MODEL), lambda i: (0, 0))],
        out_specs=[tok(D_MODEL), tok(D_MODEL)],
        out_shape=[jax.ShapeDtypeStruct((n, D_MODEL), F32), jax.ShapeDtypeStruct((n, D_MODEL), BF16)],
        compiler_params=_params(1),
        name="out_proj",
    )(o_r, o_m, o_s, x, g1, sh, sc, ng, w)


CAND_LEN = tuple(PEER_TOPK // (p + 1) for p in range(PEER_TOPK))
CAND_OFF = tuple(sum(CAND_LEN[:p]) for p in range(PEER_TOPK))
CAND_ROWS = -(-sum(CAND_LEN) // 8) * 8


def _over_sublanes(r8, op):
    for shift in (4, 2, 1):
        r8 = op(r8, pltpu.roll(r8, shift, axis=0))
    return r8


def _row_iota(tiles, n):
    shape = (tiles, 8, n)
    return (lax.broadcasted_iota(jnp.int32, shape, 0) * 8 + lax.broadcasted_iota(jnp.int32, shape, 1)).astype(F32)


def _top1(x3, iota3):
    m = _over_sublanes(jnp.max(x3, axis=0), jnp.maximum)
    idx = _over_sublanes(jnp.min(jnp.where(x3 == m[None], iota3, float(8 * x3.shape[0])), axis=0), jnp.minimum)
    return m, idx


def _gelu_tanh(x):
    c1 = math.sqrt(2.0 / math.pi)
    half = 0.5 * x
    return half * jnp.tanh(x * (c1 + (c1 * 0.044715) * (x * x))) + half


def _peer_kernel(h2c_ref, h2n_ref, wq_ref, sk_ref, u_ref, v_ref, xmid_ref, g2_ref, fg_ref, o_ref,
                 acc_ref, act_ref, gm_ref, cand_ref, eid_ref, et_ref, gt_ref, a_ref, b_ref, g_ref, tmp_ref,
                 *, tm, per_token, final_norm):
    j = pl.program_id(1)

    @pl.when((pl.program_id(0) == 0) & (j == 0))
    def _():
        acc_ref[...] = jnp.zeros_like(acc_ref)
        act_ref[...] = jnp.zeros_like(act_ref)
        gm_ref[...] = jnp.zeros_like(gm_ref)

    q = _dot(h2n_ref[...], wq_ref[...]).astype(BF16)
    s12 = [_dot_nt(sk_ref[c], q[:, c * PEER_HALF:(c + 1) * PEER_HALF]) for c in range(2)]
    acc_ref[...] += _dot(act_ref[...], v_ref[...])
    s = _dot_nt(h2c_ref[...], u_ref[...])

    rows_per_block = u_ref.shape[0] // PEER_KEYS
    for r in range(rows_per_block):
        lanes = slice(r * PEER_KEYS, (r + 1) * PEER_KEYS)
        act_ref[:, lanes] = _gelu_tanh(s[:, lanes].astype(BF16)) * gm_ref[j * rows_per_block + r]

    key_iota = _row_iota(PEER_KEYS // 8, tm)
    s12 = [s_.reshape(PEER_KEYS // 8, 8, tm) for s_ in s12]
    vals = ([], [])
    idxs = ([], [])
    for _ in range(PEER_TOPK):
        for c in range(2):
            m, idx = _top1(s12[c], key_iota)
            s12[c] = jnp.where(key_iota == idx[None], -jnp.inf, s12[c])
            vals[c].append(m[0:1])
            idxs[c].append(idx[0:1])
    v1, v2 = (jnp.concatenate(v_, axis=0) for v_ in vals)
    i1, i2 = (jnp.concatenate(i_, axis=0) for i_ in idxs)

    cand_ref[...] = jnp.full(cand_ref.shape, -jnp.inf, F32)
    eid_ref[...] = jnp.zeros(eid_ref.shape, F32)
    for p in range(PEER_TOPK):
        cand_ref[pl.ds(CAND_OFF[p], CAND_LEN[p]), :] = v1[p:p + 1, :] + v2[:CAND_LEN[p], :]
        eid_ref[pl.ds(CAND_OFF[p], CAND_LEN[p]), :] = i1[p:p + 1, :] * float(PEER_KEYS) + i2[:CAND_LEN[p], :]
    cand_iota = _row_iota(CAND_ROWS // 8, tm)
    cand = cand_ref[...].reshape(CAND_ROWS // 8, 8, tm)
    eid = eid_ref[...].reshape(CAND_ROWS // 8, 8, tm)
    tops, picks = [], []
    for _ in range(PEER_TOPK):
        m, pos = _top1(cand, cand_iota)
        hit = cand_iota == pos[None]
        cand = jnp.where(hit, -jnp.inf, cand)
        tops.append(m[0:1])
        picks.append(_over_sublanes(jnp.sum(jnp.where(hit, eid, 0.0), axis=0), jnp.add)[0:1])
    top = jnp.concatenate(tops, axis=0)
    e = jnp.exp(top - top[0:1, :])
    row0 = pl.multiple_of(j * PEER_TOPK, PEER_TOPK)
    et_ref[pl.ds(row0, PEER_TOPK), :] = jnp.concatenate(picks, axis=0)
    gt_ref[pl.ds(row0, PEER_TOPK), :] = e / jnp.sum(e, axis=0, keepdims=True)

    @pl.when(j == 0)
    def _():
        g2 = g2_ref[...] if per_token else g2_ref[0]
        x = xmid_ref[...] + g2 * acc_ref[...]
        o_ref[...] = _rms(x, fg_ref[...]) if final_norm else x
        acc_ref[...] = jnp.zeros_like(acc_ref)

    @pl.when(j == PEER_HEADS - 1)
    def _():
        picked = et_ref[...]
        key1 = jnp.floor(picked * (1.0 / PEER_KEYS))
        a_ref[...] = key1.T
        b_ref[...] = (picked - key1 * float(PEER_KEYS)).T
        g_ref[...] = gt_ref[...].T
        j_iota = lax.broadcasted_iota(jnp.int32, (PEER_KEYS, PEER_HEADS * PEER_TOPK), 0).astype(F32)

        def token_groups(gi, carry):
            for part in range(GROUPS_PER_TRIP):
                t0 = pl.multiple_of((gi * GROUPS_PER_TRIP + part) * TOKEN_GROUP, TOKEN_GROUP)
                base = part * TOKEN_GROUP * TOKEN_PITCH
                for tt in range(TOKEN_GROUP):
                    t = t0 + tt
                    sel1 = jnp.where(j_iota == a_ref[pl.ds(t, 1), :], 1.0, 0.0).astype(BF16)
                    sel2 = jnp.where(j_iota == b_ref[pl.ds(t, 1), :], g_ref[pl.ds(t, 1), :], 0.0).astype(BF16)
                    tmp_ref[pl.ds(base + tt * TOKEN_PITCH, PEER_KEYS), :] = _dot_nt(sel1, sel2)
            for part in range(GROUPS_PER_TRIP):
                t0 = pl.multiple_of((gi * GROUPS_PER_TRIP + part) * TOKEN_GROUP, TOKEN_GROUP)
                base = part * TOKEN_GROUP * TOKEN_PITCH
                for r in range(PEER_KEYS):
                    rows = tmp_ref[pl.ds(base + r, TOKEN_GROUP, stride=TOKEN_PITCH), :]
                    gm_ref[r, pl.ds(t0, TOKEN_GROUP), :] = rows.astype(BF16)
            return carry

        lax.fori_loop(0, tm // (GROUPS_PER_TRIP * TOKEN_GROUP), token_groups, 0)


def _peer(h2, wq, sk, u, v, xmid, g2, fg, *, layer, tm, rows_per_seq, per_token, final_norm):
    n = h2.shape[0]
    ntiles = n // tm
    eb = PEER_EXPERTS // PEER_HEADS
    npick = PEER_HEADS * PEER_TOPK
    clamp = lambda t: jnp.clip(t, 0, ntiles - 1)
    routed = lambda g, j: clamp(g)
    stage_a = lambda g, j: clamp(g - 1)
    done = lambda g, j: clamp(g - 2 + jnp.minimum(j, 1))
    if per_token:
        ms = pl.BlockSpec((tm, D_MODEL), lambda g, j: (done(g, j), 0))
    else:
        ms = pl.BlockSpec((1, 1, D_MODEL), lambda g, j: (done(g, j) * tm // rows_per_seq, 0, 0))
    return pl.pallas_call(
        functools.partial(_peer_kernel, tm=tm, per_token=per_token, final_norm=final_norm),
        grid=(ntiles + 2, PEER_HEADS),
        in_specs=[pl.BlockSpec((tm, D_MODEL), lambda g, j: (stage_a(g, j), 0)),
                  pl.BlockSpec((tm, D_MODEL), lambda g, j: (routed(g, j), 0)),
                  pl.BlockSpec((D_MODEL, PEER_QDIM), lambda g, j: (0, j)),
                  pl.BlockSpec((2, PEER_KEYS, PEER_HALF), lambda g, j: (0, 0, 0)),
                  pl.BlockSpec((None, eb, D_MODEL), lambda g, j: (layer, j, 0)),
                  pl.BlockSpec((None, eb, D_MODEL), lambda g, j: (layer, (j + PEER_HEADS - 1) % PEER_HEADS, 0)),
                  pl.BlockSpec((tm, D_MODEL), lambda g, j: (done(g, j), 0)),
                  ms,
                  pl.BlockSpec((1, D_MODEL), lambda g, j: (0, 0))],
        out_specs=pl.BlockSpec((tm, D_MODEL), lambda g, j: (done(g, j), 0)),
        out_shape=jax.ShapeDtypeStruct((n, D_MODEL), F32),
        scratch_shapes=[pltpu.VMEM((tm, D_MODEL), F32),
                        pltpu.VMEM((tm, eb), BF16),
                        pltpu.VMEM((PEER_KEYS, tm, PEER_KEYS), BF16),
                        pltpu.VMEM((CAND_ROWS, tm), F32), pltpu.VMEM((CAND_ROWS, tm), F32),
                        pltpu.VMEM((npick, tm), F32), pltpu.VMEM((npick, tm), F32),
                        pltpu.VMEM((tm, npick), F32), pltpu.VMEM((tm, npick), F32), pltpu.VMEM((tm, npick), F32),
                        pltpu.VMEM((GROUPS_PER_TRIP * TOKEN_GROUP * TOKEN_PITCH, PEER_KEYS), F32)],
        compiler_params=_params(0, 2),
        name="peer",
    )(h2, h2, wq, sk, u, v, xmid, g2, fg)


TOKEN_TILE = 512
PEER_TILE = 256
SAMPLE_ROWS = 8
PROMPT_SEQS_PER_STEP = (8, 4, 4)
SAMPLE_SEQS_PER_STEP = (8, 8, 8)


def _lane_pad(vec, offset=0, width=LANES):
    return jnp.zeros((1, width), F32).at[0, offset:offset + vec.shape[0]].set(vec)


def _layer_weights(l, w):
    w_in = w['w_in'][l]
    o_m = RWKV_PROJ
    o_mg = o_m + MLSTM_MAIN
    o_s = o_mg + 2 * MLSTM_HEADS
    o_sg = o_s + SSD_MAIN
    gate_cols = jnp.zeros((D_MODEL, GATE_W), F32)
    gate_cols = gate_cols.at[:, 0:MLSTM_HEADS].set(w_in[:, o_mg:o_mg + MLSTM_HEADS])
    gate_cols = gate_cols.at[:, GATE_F:GATE_F + MLSTM_HEADS].set(w_in[:, o_mg + MLSTM_HEADS:o_s])
    gate_cols = gate_cols.at[:, GATE_DT:GATE_DT + SSD_HEADS].set(w_in[:, o_sg:o_sg + SSD_HEADS])
    w_cat = jnp.concatenate([w_in[:, :o_m], w_in[:, o_m:o_mg], w_in[:, o_s:o_sg], gate_cols], axis=1).astype(BF16)
    gate_bias = jnp.concatenate([_lane_pad(w['mlstm_i_b'][l]), _lane_pad(w['mlstm_f_b'][l]),
                                 _lane_pad(w['ssd_dt_bias'][l])], axis=1)
    row = lambda name: w[name][l].reshape(1, -1)
    return dict(
        w_cat=w_cat, gate_bias=gate_bias, w_out=w['w_out'][l].astype(BF16),
        norm1_g=row('norm1_g'), norm2_g=row('norm2_g'),
        rwkv=(row('rwkv_mu'), row('rwkv_w0'), w['rwkv_w_up'][l].astype(BF16), row('rwkv_a0'),
              w['rwkv_a_up'][l].astype(BF16), w['rwkv_g_up'][l].astype(BF16), row('rwkv_k_k'), row('rwkv_k_a'),
              row('rwkv_r_k'), row('rwkv_ln_g'), row('rwkv_ln_b')),
        mlstm=(w['mlstm_conv_w'][l], row('mlstm_conv_b'), gate_bias, row('mlstm_norm_g')),
        ssd=(w['ssd_conv_w'][l], row('ssd_conv_b'), gate_bias, _lane_pad(w['ssd_A_log'][l]),
             _lane_pad(w['ssd_D'][l]), row('ssd_norm_g')),
        wq=w['peer_wq'][l].astype(BF16), sk=w['peer_subkeys'][l].astype(BF16),
        u=w['peer_u_bf16'], v=w['peer_v_bf16'], layer=l,
    )


def _run_layer(x, mods, state, lw, head_sum, final_g, *, batch, seq, rows, nseq, per_token, final_norm):
    sh1, sc1, g1, sh2, sc2, g2 = mods
    shift, wkv, mconv, m_c, m_n, m_m, sconv, sst = state
    tm = min(TOKEN_TILE, batch * seq)
    tok = dict(tm=tm, rows_per_seq=seq, per_token=per_token)
    p_r, p_m, p_s, p_g = _inproj(x, sh1, sc1, lw['norm1_g'], lw['w_cat'], **tok)

    t_pad = -(-seq // rows) * rows
    t_valid = seq if seq < rows else rows

    def seqs(a):
        a = a.reshape(batch, seq, a.shape[-1])
        return a if t_pad == seq else jnp.pad(a, ((0, 0), (0, t_pad - seq), (0, 0)))

    def toks(a):
        return a[:, :seq].reshape(batch * seq, a.shape[-1])

    mix = dict(rows=rows, t_valid=t_valid)
    nseq_rwkv, nseq_mlstm, nseq_ssd = nseq
    o_r, shift, wkv = _rwkv(seqs(p_r), shift[:, None, :], wkv, *lw['rwkv'], head_sum, nseq=nseq_rwkv, **mix)
    m_m = jnp.pad(m_m, ((0, 0), (0, LANES - MLSTM_HEADS)))[:, None, :]
    o_m, mconv, m_c, m_n, m_m = _mlstm(seqs(p_m), seqs(p_g), mconv, m_c, m_n, m_m, *lw['mlstm'],
                                       nseq=nseq_mlstm, **mix)
    o_s, sconv, sst = _ssd(seqs(p_s), seqs(p_g), sconv, sst, *lw['ssd'], nseq=nseq_ssd, **mix)
    new_state = (shift[:, 0, :], wkv, mconv, m_c, m_n, m_m[:, 0, :MLSTM_HEADS], sconv, sst)

    x_mid, h2 = _outproj(toks(o_r), toks(o_m), toks(o_s), x, g1, sh2, sc2, lw['norm2_g'], lw['w_out'], **tok)
    x_new = _peer(h2, lw['wq'], lw['sk'], lw['u'], lw['v'], x_mid, g2, final_g, layer=lw['layer'],
                  tm=min(PEER_TILE, batch * seq), rows_per_seq=seq, per_token=per_token, final_norm=final_norm)
    return x_new, new_state


def kernel(x_prompt, x_sample, state_rwkv_shift, state_rwkv_wkv, state_mlstm_conv, state_mlstm_C, state_mlstm_n, state_mlstm_m, state_ssd_conv, state_ssd, c_prompt, c_sample, ada_w, ada_b, norm1_g, norm2_g, w_in, w_out, rwkv_mu, rwkv_w0, rwkv_w_up, rwkv_a0, rwkv_a_up, rwkv_g_up, rwkv_k_k, rwkv_k_a, rwkv_r_k, rwkv_ln_g, rwkv_ln_b, mlstm_conv_w, mlstm_conv_b, mlstm_i_b, mlstm_f_b, mlstm_norm_g, ssd_conv_w, ssd_conv_b, ssd_dt_bias, ssd_A_log, ssd_D, ssd_norm_g, peer_wq, peer_subkeys, peer_u, peer_v, final_g):
    weights = dict(norm1_g=norm1_g, norm2_g=norm2_g, w_in=w_in, w_out=w_out, rwkv_mu=rwkv_mu, rwkv_w0=rwkv_w0,
                   rwkv_w_up=rwkv_w_up, rwkv_a0=rwkv_a0, rwkv_a_up=rwkv_a_up, rwkv_g_up=rwkv_g_up,
                   rwkv_k_k=rwkv_k_k, rwkv_k_a=rwkv_k_a, rwkv_r_k=rwkv_r_k, rwkv_ln_g=rwkv_ln_g,
                   rwkv_ln_b=rwkv_ln_b, mlstm_conv_w=mlstm_conv_w, mlstm_conv_b=mlstm_conv_b,
                   mlstm_i_b=mlstm_i_b, mlstm_f_b=mlstm_f_b, mlstm_norm_g=mlstm_norm_g, ssd_conv_w=ssd_conv_w,
                   ssd_conv_b=ssd_conv_b, ssd_dt_bias=ssd_dt_bias, ssd_A_log=ssd_A_log, ssd_D=ssd_D,
                   ssd_norm_g=ssd_norm_g, peer_wq=peer_wq, peer_subkeys=peer_subkeys, peer_u=peer_u, peer_v=peer_v)
    weights['peer_u_bf16'] = peer_u.astype(BF16)
    weights['peer_v_bf16'] = peer_v.astype(BF16)
    nb, seq, d = x_prompt.shape
    nd, dseq, _ = x_sample.shape
    cache = (state_rwkv_shift, state_rwkv_wkv, state_mlstm_conv, state_mlstm_C, state_mlstm_n, state_mlstm_m,
             state_ssd_conv, state_ssd)

    mod = _ada_mod(jnp.concatenate([c_prompt, c_sample], axis=0), ada_w, ada_b)
    head_id = jnp.arange(RWKV_W) // HEAD_DIM
    head_sum = (head_id[:, None] == head_id[None, :]).astype(F32)
    fg = final_g.reshape(1, d)

    xp = x_prompt.reshape(nb * seq, d)
    xs = x_sample.reshape(nd * dseq, d)
    new_p, new_s = [], []
    for l in range(DEPTH):
        lw = _layer_weights(l, weights)
        last = l == DEPTH - 1
        mod_p = mod[l, :nb].reshape(nb, 6, 1, d)
        mods_p = tuple(mod_p[:, i] for i in range(6))
        mod_s = jnp.repeat(mod[l, nb:].reshape(nd, 6, d), dseq, axis=0)
        mods_s = tuple(mod_s[:, i] for i in range(6))
        zeros = tuple(jnp.zeros((nb,) + s.shape[2:], F32) for s in cache)
        xp, sp = _run_layer(xp, mods_p, zeros, lw, head_sum, fg, batch=nb, seq=seq, rows=CHUNK,
                            nseq=PROMPT_SEQS_PER_STEP, per_token=False, final_norm=last)
        xs, ss = _run_layer(xs, mods_s, tuple(s[l] for s in cache), lw, head_sum, fg, batch=nd, seq=dseq,
                            rows=SAMPLE_ROWS, nseq=SAMPLE_SEQS_PER_STEP, per_token=True, final_norm=last)
        new_p.append(sp)
        new_s.append(ss)

    def stk(states, i):
        return jnp.stack([st[i] for st in states])

    return ((xp.reshape(nb, seq, d), xs.reshape(nd, dseq, d))
            + tuple(stk(new_p, i) for i in range(8)) + tuple(stk(new_s, i) for i in range(8)))
```

```python
import functools
import math

import jax
import jax.numpy as jnp
from jax import lax
from jax.experimental import pallas as pl
from jax.experimental.pallas import tpu as pltpu

F32 = jnp.float32
BF16 = jnp.bfloat16

D_MODEL = 1024
DEPTH = 4
HEAD_DIM = 64
RWKV_W = 256
RWKV_HEADS = 4
RWKV_DECAY_RANK = 64
RWKV_ICLR_RANK = 64
RWKV_GATE_RANK = 128
RWKV_LN_EPS = 1e-5 * HEAD_DIM
RWKV_PROJ = 3 * RWKV_W + RWKV_DECAY_RANK + RWKV_ICLR_RANK + RWKV_GATE_RANK
MLSTM_W = 256
MLSTM_HEADS = 4
MLSTM_CONV_W = 2 * MLSTM_W
MLSTM_MAIN = MLSTM_CONV_W + 2 * MLSTM_W
SSD_W = 512
SSD_HEADS = 8
SSD_STATE = 128
SSD_GROUPS = 2
SSD_HEADS_PER_GROUP = SSD_HEADS // SSD_GROUPS
SSD_CONV_W = SSD_W + 2 * SSD_GROUPS * SSD_STATE
SSD_MAIN = SSD_W + SSD_CONV_W
CONV_WIDTH = 4
CHUNK = 64
LANES = 128
GATE_I, GATE_F, GATE_DT = 0, LANES, 2 * LANES
GATE_W = 3 * LANES
PEER_KEYS = 128
PEER_EXPERTS = PEER_KEYS * PEER_KEYS
PEER_HEADS = 8
PEER_TOPK = 16
PEER_QDIM = 256
PEER_HALF = 128
NORM_EPS = 1e-6
NEG_BIG = -1e30
TOKEN_GROUP = 16
TABLE_BUFS = 3
GROUPS_PER_TRIP = 8
TOKEN_PITCH = PEER_KEYS + 4

VMEM_LIMIT_BYTES = 56 * 1024 * 1024
HIST = 8


def _dot(a, b, prec=None):
    return jnp.dot(a, b, preferred_element_type=F32, precision=prec)


def _dot_nt(a, b, prec=None):
    return lax.dot_general(a, b, (((1,), (1,)), ((), ())), preferred_element_type=F32, precision=prec)


def _dot_tn(a, b, prec=None):
    return lax.dot_general(a, b, (((0,), (0,)), ((), ())), preferred_element_type=F32, precision=prec)


def _hi_lo(x):
    hi = x.astype(BF16)
    return hi, (x - hi.astype(F32)).astype(BF16)


def _mm(a, b, form, b_exact=False, single=False):
    dot = {'nn': _dot, 'nt': _dot_nt, 'tn': _dot_tn}[form]
    if single:
        return dot(a.astype(BF16), b.astype(BF16))
    ah, al = _hi_lo(a)
    a_axis = 0 if form == 'tn' else 1
    b_axis = 1 if form == 'nt' else 0
    if b_exact:
        bb = b.astype(BF16)
        a3 = jnp.concatenate([ah, al], axis=a_axis)
        b3 = jnp.concatenate([bb, bb], axis=b_axis)
    else:
        bh, bl = _hi_lo(b)
        a3 = jnp.concatenate([ah, ah, al], axis=a_axis)
        b3 = jnp.concatenate([bh, bl, bh], axis=b_axis)
    return dot(a3, b3)


def _cumsum_rows(x, tril_bf16):
    hi = x.astype(BF16)
    r1 = x - hi.astype(F32)
    mid = r1.astype(BF16)
    lo = (r1 - mid.astype(F32)).astype(BF16)
    return _dot(jnp.concatenate([tril_bf16] * 3, axis=1), jnp.concatenate([hi, mid, lo], axis=0))


def _sigmoid(x):
    return 1.0 / (1.0 + jnp.exp(-x))


def _silu(x):
    return x * _sigmoid(x)


def _softplus(x):
    return jnp.maximum(x, 0.0) + jnp.log(1.0 + jnp.exp(-jnp.abs(x)))


def _tri(n, strict=False):
    r = lax.broadcasted_iota(jnp.int32, (n, n), 0)
    c = lax.broadcasted_iota(jnp.int32, (n, n), 1)
    return (r > c) if strict else (r >= c)


def _to_row(col, eye):
    return jnp.sum(jnp.where(eye, col, 0.0), axis=0, keepdims=True)


def _params(n_parallel, n_arbitrary=0):
    sem = ("parallel",) * n_parallel + ("arbitrary",) * n_arbitrary
    return pltpu.CompilerParams(dimension_semantics=sem, vmem_limit_bytes=VMEM_LIMIT_BYTES)


def _ada_kernel(c_ref, w_ref, b_ref, o_ref):
    c = _silu(c_ref[...]).astype(BF16)
    o_ref[0] = _dot(c, w_ref[0].astype(BF16)) + b_ref[0]


def _ada_mod(c_all, ada_w, ada_b):
    nb = c_all.shape[0]
    tn = 1536
    return pl.pallas_call(
        _ada_kernel,
        grid=(DEPTH, 6 * D_MODEL // tn),
        in_specs=[pl.BlockSpec((nb, D_MODEL), lambda l, j: (0, 0)),
                  pl.BlockSpec((1, D_MODEL, tn), lambda l, j: (l, 0, j)),
                  pl.BlockSpec((1, 1, tn), lambda l, j: (l, 0, j))],
        out_specs=pl.BlockSpec((1, nb, tn), lambda l, j: (l, 0, j)),
        out_shape=jax.ShapeDtypeStruct((DEPTH, nb, 6 * D_MODEL), F32),
        compiler_params=_params(2),
        name="ada_mod",
    )(c_all, ada_w, ada_b.reshape(DEPTH, 1, 6 * D_MODEL))


def _rms(x, g):
    return x * lax.rsqrt(jnp.mean(x * x, axis=-1, keepdims=True) + NORM_EPS) * g


def _inproj_kernel(x_ref, sh_ref, sc_ref, g_ref, w_ref, pr_ref, pm_ref, ps_ref, pg_ref, *, per_token):
    sh = sh_ref[...] if per_token else sh_ref[0]
    sc = sc_ref[...] if per_token else sc_ref[0]
    h = _rms(x_ref[...], g_ref[...]) * (1.0 + sc) + sh
    p = _dot(h.astype(BF16), w_ref[...])
    o = 0
    for ref, w in ((pr_ref, RWKV_PROJ), (pm_ref, MLSTM_MAIN), (ps_ref, SSD_MAIN), (pg_ref, GATE_W)):
        ref[...] = p[:, o:o + w]
        o += w


def _mod_spec(tm, rows_per_seq, per_token):
    if per_token:
        return pl.BlockSpec((tm, D_MODEL), lambda i: (i, 0))
    return pl.BlockSpec((1, 1, D_MODEL), lambda i: (i * tm // rows_per_seq, 0, 0))


def _inproj(x, sh, sc, g, w, *, tm, rows_per_seq, per_token):
    n = x.shape[0]
    widths = (RWKV_PROJ, MLSTM_MAIN, SSD_MAIN, GATE_W)
    wtot = sum(widths)
    return pl.pallas_call(
        functools.partial(_inproj_kernel, per_token=per_token),
        grid=(n // tm,),
        in_specs=[pl.BlockSpec((tm, D_MODEL), lambda i: (i, 0)),
                  _mod_spec(tm, rows_per_seq, per_token), _mod_spec(tm, rows_per_seq, per_token),
                  pl.BlockSpec((1, D_MODEL), lambda i: (0, 0)),
                  pl.BlockSpec((D_MODEL, wtot), lambda i: (0, 0))],
        out_specs=[pl.BlockSpec((tm, w_), lambda i: (i, 0)) for w_ in widths],
        out_shape=[jax.ShapeDtypeStruct((n, w_), F32) for w_ in widths],
        compiler_params=_params(1),
        name="in_proj",
    )(x, sh, sc, g, w)


HIST0 = HIST - (CONV_WIDTH - 1)


def _causal_conv(ext_ref, s, u, w_ref, b_ref, rows, t_valid, new_buf_ref):
    ext_ref[s, pl.ds(HIST, rows), :] = u
    out = b_ref[...]
    for j in range(CONV_WIDTH):
        out = out + ext_ref[s, pl.ds(HIST0 + j, rows), :] * w_ref[pl.ds(j, 1), :]
    last = ext_ref[s, pl.ds(HIST0 + t_valid, CONV_WIDTH - 1), :]
    new_buf_ref[s] = last
    ext_ref[s, pl.ds(HIST0, CONV_WIDTH - 1), :] = last
    return out


def _mixer_specs(nseq, rows):
    chunk = lambda w, col=0: pl.BlockSpec((nseq, rows, w), lambda i, c: (i, c, col))
    full = lambda *s: pl.BlockSpec(s, lambda i, c: (0,) * len(s))
    per_seq = lambda *s: pl.BlockSpec((nseq,) + s, lambda i, c: (i,) + (0,) * len(s))
    return chunk, full, per_seq


def _row_mask(rows, t_valid):
    if t_valid == rows:
        return None
    return lax.broadcasted_iota(jnp.int32, (rows, 1), 0) < t_valid


def _mlstm_kernel(pm_ref, pg_ref, conv0_ref, c0_ref, n0_ref, m0_ref, cw_ref, cb_ref, gb_ref, ng_ref,
                  out_ref, conv_ref, c_ref, n_ref, m_ref, ext_ref, *, rows, t_valid, nseq):
    @pl.when(pl.program_id(1) == 0)
    def _():
        ext_ref[:, pl.ds(HIST0, CONV_WIDTH - 1), :] = conv0_ref[...]
        c_ref[...] = c0_ref[...]
        n_ref[...] = n0_ref[...]
        m_ref[...] = m0_ref[...]

    valid = _row_mask(rows, t_valid)
    tril = _tri(rows)
    tril_b = tril.astype(BF16)
    eye = lax.broadcasted_iota(jnp.int32, (rows, rows), 0) == lax.broadcasted_iota(jnp.int32, (rows, rows), 1)
    gb = gb_ref[...]

    seqs = []
    for s in range(nseq):
        pm = pm_ref[s]
        qk = _silu(_causal_conv(ext_ref, s, pm[:, :MLSTM_CONV_W], cw_ref, cb_ref, rows, t_valid, conv_ref))
        gates = pg_ref[s]
        ig = gates[:, GATE_I:GATE_I + LANES] + gb[:, GATE_I:GATE_I + LANES]
        fg = gates[:, GATE_F:GATE_F + LANES] + gb[:, GATE_F:GATE_F + LANES]
        logf = -_softplus(-fg)
        if valid is not None:
            ig = jnp.where(valid, ig, NEG_BIG)
            logf = jnp.where(valid, logf, 0.0)
        bcum = _cumsum_rows(logf, tril_b)
        m_prev = m_ref[s]
        b_end = bcum[rows - 1:rows, :]
        ws = b_end - bcum + ig
        m_new = jnp.maximum(b_end + m_prev, jnp.max(ws, axis=0, keepdims=True))
        m_ref[s] = m_new
        seqs.append(dict(q=qk[:, :MLSTM_W], k=qk[:, MLSTM_W:] * HEAD_DIM ** -0.5,
                         v=pm[:, MLSTM_CONV_W:MLSTM_CONV_W + MLSTM_W], o=pm[:, MLSTM_CONV_W + MLSTM_W:],
                         ig=ig, bcum=bcum, inter=bcum + m_prev, ws=jnp.exp(ws - m_new),
                         a_end=jnp.exp(b_end + m_prev - m_new)))

    chains = [(s, h) for s in range(nseq) for h in range(MLSTM_HEADS)]
    lanes = lambda c: slice(c[1] * HEAD_DIM, (c[1] + 1) * HEAD_DIM)
    col = lambda name, c: seqs[c[0]][name][:, c[1]:c[1] + 1]
    q = [seqs[c[0]]['q'][:, lanes(c)] for c in chains]
    k = [seqs[c[0]]['k'][:, lanes(c)] for c in chains]
    v = [seqs[c[0]]['v'][:, lanes(c)] for c in chains]
    c0 = [c_ref[c[0], c[1]] for c in chains]
    n0 = [n_ref[c[0], pl.ds(c[1], 1), :] for c in chains]
    qk_t = [_dot_nt(q_, k_) for q_, k_ in zip(q, k)]
    q_c = [_dot_nt(q_, c_) for q_, c_ in zip(q, c0)]
    c_add = [_dot_tn(col('ws', c) * v_, k_) for c, v_, k_ in zip(chains, v, k)]
    dmat = [jnp.where(tril, col('bcum', c) - _to_row(col('bcum', c), eye) + _to_row(col('ig', c), eye), NEG_BIG)
            for c in chains]
    m_t = [jnp.maximum(col('inter', c), jnp.max(d_, axis=1, keepdims=True)) for c, d_ in zip(chains, dmat)]
    sc = [g_ * jnp.exp(d_ - m_) for g_, d_, m_ in zip(qk_t, dmat, m_t)]
    s_v = [_dot(s_, v_) for s_, v_ in zip(sc, v)]
    ones = jnp.ones((rows, HEAD_DIM), BF16)
    ones_hh = jnp.ones((HEAD_DIM, HEAD_DIM), BF16)
    s_sum = [_mm(s_, ones, 'nn', b_exact=True) for s_ in sc]
    q_n = [_mm(q_, jnp.broadcast_to(n_, (HEAD_DIM, HEAD_DIM)), 'nt') for q_, n_ in zip(q, n0)]
    hh = []
    for i, c in enumerate(chains):
        a_in = jnp.exp(col('inter', c) - m_t[i])
        num = s_v[i] + a_in * q_c[i]
        den = s_sum[i] + a_in * q_n[i]
        hh.append(num / jnp.maximum(jnp.abs(den), jnp.exp(-m_t[i])))
        ae = col('a_end', c)
        c_ref[c[0], c[1]] = ae * c0[i] + c_add[i]
        n_ref[c[0], pl.ds(c[1], 1), :] = ae * n0[i] + jnp.sum(col('ws', c) * k[i], axis=0, keepdims=True)
    inv_n = 1.0 / HEAD_DIM
    hc = [h_ - _mm(h_, ones_hh, 'nn', b_exact=True) * inv_n for h_ in hh]
    var = [_mm(h_ * h_, ones_hh, 'nn', b_exact=True) * inv_n for h_ in hc]
    for i, c in enumerate(chains):
        hn = hc[i] * lax.rsqrt(var[i] + NORM_EPS)
        out_ref[c[0], :, lanes(c)] = hn * ng_ref[:, lanes(c)] * _sigmoid(seqs[c[0]]['o'][:, lanes(c)])


def _mlstm(pm, pg, conv0, c0, n0, m0, cw, cb, gb, ng, *, rows, t_valid, nseq):
    b, t, _ = pm.shape
    chunk, full, per_seq = _mixer_specs(nseq, rows)
    state_specs = [per_seq(CONV_WIDTH - 1, MLSTM_CONV_W), per_seq(MLSTM_HEADS, HEAD_DIM, HEAD_DIM),
                   per_seq(MLSTM_HEADS, HEAD_DIM), per_seq(1, LANES)]
    return pl.pallas_call(
        functools.partial(_mlstm_kernel, rows=rows, t_valid=t_valid, nseq=nseq),
        grid=(b // nseq, t // rows),
        in_specs=[chunk(MLSTM_MAIN), chunk(2 * LANES)] + state_specs
                 + [full(CONV_WIDTH, MLSTM_CONV_W), full(1, MLSTM_CONV_W), full(1, GATE_W), full(1, MLSTM_W)],
        out_specs=[chunk(MLSTM_W)] + state_specs,
        out_shape=[jax.ShapeDtypeStruct((b, t, MLSTM_W), F32),
                   jax.ShapeDtypeStruct((b, CONV_WIDTH - 1, MLSTM_CONV_W), F32),
                   jax.ShapeDtypeStruct((b, MLSTM_HEADS, HEAD_DIM, HEAD_DIM), F32),
                   jax.ShapeDtypeStruct((b, MLSTM_HEADS, HEAD_DIM), F32),
                   jax.ShapeDtypeStruct((b, 1, LANES), F32)],
        scratch_shapes=[pltpu.VMEM((nseq, HIST + rows, MLSTM_CONV_W), F32)],
        compiler_params=_params(1, 1),
        name="mlstm_mix",
    )(pm, pg, conv0, c0, n0, m0, cw, cb, gb, ng)


def _ssd_kernel(ps_ref, dt_ref, conv0_ref, h0_ref, cw_ref, cb_ref, gb_ref, alog_ref, dskip_ref, ng_ref,
                out_ref, conv_ref, h_ref, ext_ref, y_ref, *, rows, t_valid, nseq):
    @pl.when(pl.program_id(1) == 0)
    def _():
        ext_ref[:, pl.ds(HIST0, CONV_WIDTH - 1), :] = conv0_ref[...]
        h_ref[...] = h0_ref[...]

    valid = _row_mask(rows, t_valid)
    tril = _tri(rows)
    tril_b = tril.astype(BF16)
    eye = lax.broadcasted_iota(jnp.int32, (rows, rows), 0) == lax.broadcasted_iota(jnp.int32, (rows, rows), 1)
    dskip = dskip_ref[...]
    neg_a = -jnp.exp(alog_ref[...])
    gs = SSD_GROUPS * SSD_STATE

    seqs = []
    for s in range(nseq):
        ps = ps_ref[s]
        xbc = _silu(_causal_conv(ext_ref, s, ps[:, SSD_W:], cw_ref, cb_ref, rows, t_valid, conv_ref))
        dt = _softplus(dt_ref[s] + gb_ref[:, GATE_DT:GATE_DT + LANES])
        if valid is not None:
            dt = jnp.where(valid, dt, 0.0)
        seqs.append(dict(z=ps[:, :SSD_W], xs=xbc[:, :SSD_W], bm=xbc[:, SSD_W:SSD_W + gs], cm=xbc[:, SSD_W + gs:],
                         dt=dt, acum=_cumsum_rows(dt * neg_a, tril_b)))

    groups = [(s, g) for s in range(nseq) for g in range(SSD_GROUPS)]
    grp = lambda name, sg: seqs[sg[0]][name][:, sg[1] * SSD_STATE:(sg[1] + 1) * SSD_STATE]
    cb = {sg: _dot_nt(grp('cm', sg), grp('bm', sg)) for sg in groups}
    chains = [(s, hd) for s in range(nseq) for hd in range(SSD_HEADS)]
    lanes = lambda c: slice(c[1] * HEAD_DIM, (c[1] + 1) * HEAD_DIM)
    col = lambda name, c: seqs[c[0]][name][:, c[1]:c[1] + 1]
    group_of = lambda c: (c[0], c[1] // SSD_HEADS_PER_GROUP)
    x = [seqs[c[0]]['xs'][:, lanes(c)] for c in chains]
    h0 = [h_ref[c[0], c[1]] for c in chains]
    from_state = [_dot_nt(grp('cm', group_of(c)), h_) for c, h_ in zip(chains, h0)]
    h_add = [_dot_tn(jnp.exp(col('acum', c)[rows - 1:rows, :] - col('acum', c)) * col('dt', c) * x_,
                     grp('bm', group_of(c))) for c, x_ in zip(chains, x)]
    wmat = [jnp.exp(jnp.where(tril, col('acum', c) - _to_row(col('acum', c), eye), NEG_BIG))
            * cb[group_of(c)] * _to_row(col('dt', c), eye) for c in chains]
    y = [_dot(w_, x_) for w_, x_ in zip(wmat, x)]
    for i, c in enumerate(chains):
        a_col = col('acum', c)
        h_ref[c[0], c[1]] = jnp.exp(a_col[rows - 1:rows, :]) * h0[i] + h_add[i]
        y_ref[c[0], :, lanes(c)] = y[i] + jnp.exp(a_col) * from_state[i] + dskip[:, c[1]:c[1] + 1] * x[i]
    for s in range(nseq):
        out_ref[s] = _rms(y_ref[s] * _silu(seqs[s]['z']), ng_ref[...])


def _ssd(ps, pg, conv0, h0, cw, cb, gb, alog, dskip, ng, *, rows, t_valid, nseq):
    b, t, _ = ps.shape
    chunk, full, per_seq = _mixer_specs(nseq, rows)
    state_specs = [per_seq(CONV_WIDTH - 1, SSD_CONV_W), per_seq(SSD_HEADS, HEAD_DIM, SSD_STATE)]
    return pl.pallas_call(
        functools.partial(_ssd_kernel, rows=rows, t_valid=t_valid, nseq=nseq),
        grid=(b // nseq, t // rows),
        in_specs=[chunk(SSD_MAIN), chunk(LANES, col=GATE_DT // LANES)] + state_specs
                 + [full(CONV_WIDTH, SSD_CONV_W), full(1, SSD_CONV_W), full(1, GATE_W),
                    full(1, LANES), full(1, LANES), full(1, SSD_W)],
        out_specs=[chunk(SSD_W)] + state_specs,
        out_shape=[jax.ShapeDtypeStruct((b, t, SSD_W), F32),
                   jax.ShapeDtypeStruct((b, CONV_WIDTH - 1, SSD_CONV_W), F32),
                   jax.ShapeDtypeStruct((b, SSD_HEADS, HEAD_DIM, SSD_STATE), F32)],
        scratch_shapes=[pltpu.VMEM((nseq, HIST + rows, SSD_CONV_W), F32), pltpu.VMEM((nseq, rows, SSD_W), F32)],
        compiler_params=_params(1, 1),
        name="ssd_mix",
    )(ps, pg, conv0, h0, cw, cb, gb, alog, dskip, ng)


def _rwkv_kernel(pr_ref, shift0_ref, s0_ref, mu_ref, w0_ref, wup_ref, a0_ref, aup_ref, gup_ref, kk_ref, ka_ref,
                 rk_ref, lng_ref, lnb_ref, bd_ref,
                 out_ref, shift_ref, s_ref, ext_ref, y_ref, *, rows, t_valid, nseq):
    @pl.when(pl.program_id(1) == 0)
    def _():
        ext_ref[:, pl.ds(HIST - 1, 1), :] = shift0_ref[...]
        s_ref[...] = s0_ref[...]

    tril = _tri(rows)
    stril = _tri(rows, strict=True)
    bd = bd_ref[...]
    head_sum = lambda t: _mm(t, bd, 'nn', b_exact=True)
    w3 = 3 * RWKV_W
    n = nseq * rows

    p3 = pr_ref[...]
    ext_ref[:, pl.ds(HIST, rows), :] = p3
    prev = ext_ref[:, pl.ds(HIST - 1, rows), :].reshape(n, RWKV_PROJ)
    last = ext_ref[:, pl.ds(HIST - 1 + t_valid, 1), :]
    shift_ref[...] = last
    ext_ref[:, pl.ds(HIST - 1, 1), :] = last
    p = p3.reshape(n, RWKV_PROJ)
    x = p + (prev - p) * mu_ref[...]
    r = x[:, :RWKV_W]
    k = x[:, RWKV_W:2 * RWKV_W]
    v = x[:, 2 * RWKV_W:w3]
    xw = x[:, w3:w3 + RWKV_DECAY_RANK]
    xa = x[:, w3 + RWKV_DECAY_RANK:w3 + RWKV_DECAY_RANK + RWKV_ICLR_RANK]
    xg = x[:, w3 + RWKV_DECAY_RANK + RWKV_ICLR_RANK:]
    w_log = -_softplus(-(w0_ref[...] + _dot(jnp.tanh(xw).astype(BF16), wup_ref[...]))) - 0.5
    logw = -jnp.exp(w_log)
    a = _sigmoid(a0_ref[...] + _dot(xa.astype(BF16), aup_ref[...]))
    g = _dot(_sigmoid(xg).astype(BF16), gup_ref[...])
    kk = k * kk_ref[...]
    kk = kk / jnp.maximum(jnp.sqrt(head_sum(kk * kk)), 1e-12)
    k = k * (1.0 + (a - 1.0) * ka_ref[...])
    ri = lax.broadcasted_iota(jnp.int32, (n, n), 0)
    ci = lax.broadcasted_iota(jnp.int32, (n, n), 1)
    same_seq = (ri // rows) == (ci // rows)
    if t_valid != rows:
        valid = (lax.broadcasted_iota(jnp.int32, (n, 1), 0) % rows) < t_valid
        logw = jnp.where(valid, logw, 0.0)
        kk = jnp.where(valid, kk, 0.0)
        k = jnp.where(valid, k, 0.0)

    cum = _cumsum_rows(logw, (same_seq & (ri >= ci)).astype(BF16))
    to_end = jnp.exp(_cumsum_rows(logw, (same_seq & (ri < ci)).astype(BF16)))
    p_inv = jnp.exp(-cum)
    a_t = -kk * jnp.exp(cum - logw)
    kka = kk * a
    b_t = kka * p_inv
    k_t = k * p_inv
    r_t = r * jnp.exp(cum)
    b_e = kka * to_end
    k_e = k * to_end
    p_end = jnp.exp(cum)

    chains = [(s, h) for s in range(nseq) for h in range(RWKV_HEADS)]
    blk = lambda t, c: t[c[0] * rows:(c[0] + 1) * rows, c[1] * HEAD_DIM:(c[1] + 1) * HEAD_DIM]
    s0 = [s_ref[c[0], c[1]] for c in chains]
    vh = [blk(v, c) for c in chains]
    ar = [jnp.concatenate([blk(a_t, c), blk(r_t, c)], axis=0) for c in chains]
    bk = [jnp.concatenate([blk(b_t, c), blk(k_t, c)], axis=0) for c in chains]
    mm = functools.partial(_mm, single=True)
    gram = [mm(x_, y_, 'nt') for x_, y_ in zip(ar, bk)]
    from_state = [mm(x_, y_, 'nt') for x_, y_ in zip(ar, s0)]
    n_mat = [jnp.where(stril, g_[:rows, :rows], 0.0) for g_ in gram]
    u = [f_[:rows] + mm(jnp.where(stril, g_[:rows, rows:], 0.0), v_, 'nn')
         for f_, g_, v_ in zip(from_state, gram, vh)]
    span = 1
    while span < rows:
        u = [u_ + mm(n_, u_, 'nn') for u_, n_ in zip(u, n_mat)]
        span *= 2
        if span < rows:
            n_mat = [mm(n_, n_, 'nn') for n_ in n_mat]
    y_u = [mm(jnp.where(tril, g_[rows:, :rows], 0.0), u_, 'nn') for g_, u_ in zip(gram, u)]
    y_v = [mm(jnp.where(tril, g_[rows:, rows:], 0.0), v_, 'nn') for g_, v_ in zip(gram, vh)]
    s_add = [mm(jnp.concatenate([u_, v_], axis=0), jnp.concatenate([blk(b_e, c), blk(k_e, c)], axis=0), 'tn')
             for u_, v_, c in zip(u, vh, chains)]
    for i, c in enumerate(chains):
        lanes = slice(c[1] * HEAD_DIM, (c[1] + 1) * HEAD_DIM)
        y_ref[c[0], :, lanes] = from_state[i][rows:] + y_u[i] + y_v[i]
        end_row = (c[0] + 1) * rows - 1
        s_ref[c[0], c[1]] = p_end[end_row:end_row + 1, lanes] * s0[i] + s_add[i]

    y = y_ref[...].reshape(n, RWKV_W)
    inv_n = 1.0 / HEAD_DIM
    yc = y - head_sum(y) * inv_n
    yn = yc * lax.rsqrt(head_sum(yc * yc) * inv_n + RWKV_LN_EPS)
    bonus = head_sum(r * k * rk_ref[...]) * v
    out_ref[...] = ((yn * lng_ref[...] + lnb_ref[...] + bonus) * g).reshape(nseq, rows, RWKV_W)


def _rwkv(pr, shift0, s0, mu, w0, wup, a0, aup, gup, kkp, kap, rk, lng, lnb, bd, *, rows, t_valid, nseq):
    b, t, _ = pr.shape
    chunk, full, per_seq = _mixer_specs(nseq, rows)
    state_specs = [per_seq(1, RWKV_PROJ), per_seq(RWKV_HEADS, HEAD_DIM, HEAD_DIM)]
    return pl.pallas_call(
        functools.partial(_rwkv_kernel, rows=rows, t_valid=t_valid, nseq=nseq),
        grid=(b // nseq, t // rows),
        in_specs=[chunk(RWKV_PROJ)] + state_specs
                 + [full(1, RWKV_PROJ), full(1, RWKV_W), full(RWKV_DECAY_RANK, RWKV_W), full(1, RWKV_W),
                    full(RWKV_ICLR_RANK, RWKV_W), full(RWKV_GATE_RANK, RWKV_W), full(1, RWKV_W), full(1, RWKV_W),
                    full(1, RWKV_W), full(1, RWKV_W), full(1, RWKV_W), full(RWKV_W, RWKV_W)],
        out_specs=[chunk(RWKV_W)] + state_specs,
        out_shape=[jax.ShapeDtypeStruct((b, t, RWKV_W), F32),
                   jax.ShapeDtypeStruct((b, 1, RWKV_PROJ), F32),
                   jax.ShapeDtypeStruct((b, RWKV_HEADS, HEAD_DIM, HEAD_DIM), F32)],
        scratch_shapes=[pltpu.VMEM((nseq, HIST + rows, RWKV_PROJ), F32), pltpu.VMEM((nseq, rows, RWKV_W), F32)],
        compiler_params=_params(1, 1),
        name="rwkv_mix",
    )(pr, shift0, s0, mu, w0, wup, a0, aup, gup, kkp, kap, rk, lng, lnb, bd)


def _outproj_kernel(or_ref, om_ref, os_ref, x_ref, g1_ref, sh_ref, sc_ref, ng_ref, w_ref, xmid_ref, h2_ref,
                    *, per_token):
    pick = (lambda r: r[...]) if per_token else (lambda r: r[0])
    w = w_ref[...]
    mix = (_dot(or_ref[...].astype(BF16), w[:RWKV_W])
           + _dot(om_ref[...].astype(BF16), w[RWKV_W:RWKV_W + MLSTM_W])
           + _dot(os_ref[...].astype(BF16), w[RWKV_W + MLSTM_W:]))
    x = x_ref[...] + pick(g1_ref) * mix
    xmid_ref[...] = x
    h2_ref[...] = (_rms(x, ng_ref[...]) * (1.0 + pick(sc_ref)) + pick(sh_ref)).astype(BF16)


def _outproj(o_r, o_m, o_s, x, g1, sh, sc, ng, w, *, tm, rows_per_seq, per_token):
    n = x.shape[0]
    tok = lambda w_: pl.BlockSpec((tm, w_), lambda i: (i, 0))
    ms = _mod_spec(tm, rows_per_seq, per_token)
    return pl.pallas_call(
        functools.partial(_outproj_kernel, per_token=per_token),
        grid=(n // tm,),
        in_specs=[tok(RWKV_W), tok(MLSTM_W), tok(SSD_W), tok(D_MODEL), ms, ms, ms,
                  pl.BlockSpec((1, D_MODEL), lambda i: (0, 0)),
                  pl.BlockSpec((D_MODEL, D_MODEL), lambda i: (0, 0))],
        out_specs=[tok(D_MODEL), tok(D_MODEL)],
        out_shape=[jax.ShapeDtypeStruct((n, D_MODEL), F32), jax.ShapeDtypeStruct((n, D_MODEL), BF16)],
        compiler_params=_params(1),
        name="out_proj",
    )(o_r, o_m, o_s, x, g1, sh, sc, ng, w)


CAND_LEN = tuple(PEER_TOPK // (p + 1) for p in range(PEER_TOPK))
CAND_OFF = tuple(sum(CAND_LEN[:p]) for p in range(PEER_TOPK))
CAND_ROWS = -(-sum(CAND_LEN) // 8) * 8


def _over_sublanes(r8, op):
    for shift in (4, 2, 1):
        r8 = op(r8, pltpu.roll(r8, shift, axis=0))
    return r8


def _row_iota(tiles, n):
    shape = (tiles, 8, n)
    return (lax.broadcasted_iota(jnp.int32, shape, 0) * 8 + lax.broadcasted_iota(jnp.int32, shape, 1)).astype(F32)


def _top1(x3, iota3):
    m = _over_sublanes(jnp.max(x3, axis=0), jnp.maximum)
    idx = _over_sublanes(jnp.min(jnp.where(x3 == m[None], iota3, float(8 * x3.shape[0])), axis=0), jnp.minimum)
    return m, idx


def _gelu_tanh(x):
    c1 = math.sqrt(2.0 / math.pi)
    half = 0.5 * x
    return half * jnp.tanh(x * (c1 + (c1 * 0.044715) * (x * x))) + half


def _peer_kernel(h2c_ref, h2n_ref, wq_ref, sk_ref, u_hbm, v_hbm, xmid_ref, g2_ref, fg_ref, o_ref,
                 acc_ref, act_ref, gm_ref, cand_ref, eid_ref, et_ref, gt_ref, a_ref, b_ref, g_ref, tmp_ref,
                 ubuf_ref, vbuf_ref, sem_ref, *, layer, tm, per_token, final_norm):
    j = pl.program_id(1)

    @pl.when((pl.program_id(0) == 0) & (j == 0))
    def _():
        acc_ref[...] = jnp.zeros_like(acc_ref)
        act_ref[...] = jnp.zeros_like(act_ref)
        gm_ref[...] = jnp.zeros_like(gm_ref)

    eb = ubuf_ref.shape[1]
    step = pl.program_id(0) * PEER_HEADS + j
    nsteps = pl.num_programs(0) * PEER_HEADS

    def table_copies(t, slot):
        ju = t % PEER_HEADS
        jv = (t + PEER_HEADS - 1) % PEER_HEADS
        return (pltpu.make_async_copy(u_hbm.at[layer, pl.ds(ju * eb, eb), :], ubuf_ref.at[slot], sem_ref.at[0, slot]),
                pltpu.make_async_copy(v_hbm.at[layer, pl.ds(jv * eb, eb), :], vbuf_ref.at[slot], sem_ref.at[1, slot]))

    @pl.when(step == 0)
    def _():
        for t in range(TABLE_BUFS - 1):
            for copy in table_copies(t, t):
                copy.start()

    slot = step % TABLE_BUFS
    for copy in table_copies(step, slot):
        copy.wait()

    @pl.when(step + TABLE_BUFS - 1 < nsteps)
    def _():
        ahead = step + TABLE_BUFS - 1
        for copy in table_copies(ahead, ahead % TABLE_BUFS):
            copy.start()

    q = _dot(h2n_ref[...], wq_ref[...]).astype(BF16)
    s12 = [_dot_nt(sk_ref[c], q[:, c * PEER_HALF:(c + 1) * PEER_HALF]) for c in range(2)]
    acc_ref[...] += _dot(act_ref[...], vbuf_ref[slot])
    s = _dot_nt(h2c_ref[...], ubuf_ref[slot])

    rows_per_block = eb // PEER_KEYS
    for r in range(rows_per_block):
        lanes = slice(r * PEER_KEYS, (r + 1) * PEER_KEYS)
        act_ref[:, lanes] = _gelu_tanh(s[:, lanes].astype(BF16)) * gm_ref[j * rows_per_block + r]

    key_iota = _row_iota(PEER_KEYS // 8, tm)
    s12 = [s_.reshape(PEER_KEYS // 8, 8, tm) for s_ in s12]
    vals = ([], [])
    idxs = ([], [])
    for _ in range(PEER_TOPK):
        for c in range(2):
            m, idx = _top1(s12[c], key_iota)
            s12[c] = jnp.where(key_iota == idx[None], -jnp.inf, s12[c])
            vals[c].append(m[0:1])
            idxs[c].append(idx[0:1])
    v1, v2 = (jnp.concatenate(v_, axis=0) for v_ in vals)
    i1, i2 = (jnp.concatenate(i_, axis=0) for i_ in idxs)

    cand_ref[...] = jnp.full(cand_ref.shape, -jnp.inf, F32)
    eid_ref[...] = jnp.zeros(eid_ref.shape, F32)
    for p in range(PEER_TOPK):
        cand_ref[pl.ds(CAND_OFF[p], CAND_LEN[p]), :] = v1[p:p + 1, :] + v2[:CAND_LEN[p], :]
        eid_ref[pl.ds(CAND_OFF[p], CAND_LEN[p]), :] = i1[p:p + 1, :] * float(PEER_KEYS) + i2[:CAND_LEN[p], :]
    cand_iota = _row_iota(CAND_ROWS // 8, tm)
    cand = cand_ref[...].reshape(CAND_ROWS // 8, 8, tm)
    eid = eid_ref[...].reshape(CAND_ROWS // 8, 8, tm)
    tops, picks = [], []
    for _ in range(PEER_TOPK):
        m, pos = _top1(cand, cand_iota)
        hit = cand_iota == pos[None]
        cand = jnp.where(hit, -jnp.inf, cand)
        tops.append(m[0:1])
        picks.append(_over_sublanes(jnp.sum(jnp.where(hit, eid, 0.0), axis=0), jnp.add)[0:1])
    top = jnp.concatenate(tops, axis=0)
    e = jnp.exp(top - top[0:1, :])
    row0 = pl.multiple_of(j * PEER_TOPK, PEER_TOPK)
    et_ref[pl.ds(row0, PEER_TOPK), :] = jnp.concatenate(picks, axis=0)
    gt_ref[pl.ds(row0, PEER_TOPK), :] = e / jnp.sum(e, axis=0, keepdims=True)

    @pl.when(j == 0)
    def _():
        g2 = g2_ref[...] if per_token else g2_ref[0]
        x = xmid_ref[...] + g2 * acc_ref[...]
        o_ref[...] = _rms(x, fg_ref[...]) if final_norm else x
        acc_ref[...] = jnp.zeros_like(acc_ref)

    @pl.when(j == PEER_HEADS - 1)
    def _():
        picked = et_ref[...]
        key1 = jnp.floor(picked * (1.0 / PEER_KEYS))
        a_ref[...] = key1.T
        b_ref[...] = (picked - key1 * float(PEER_KEYS)).T
        g_ref[...] = gt_ref[...].T
        j_iota = lax.broadcasted_iota(jnp.int32, (PEER_KEYS, PEER_HEADS * PEER_TOPK), 0).astype(F32)

        def token_groups(gi, carry):
            for part in range(GROUPS_PER_TRIP):
                t0 = pl.multiple_of((gi * GROUPS_PER_TRIP + part) * TOKEN_GROUP, TOKEN_GROUP)
                base = part * TOKEN_GROUP * TOKEN_PITCH
                for tt in range(TOKEN_GROUP):
                    t = t0 + tt
                    sel1 = jnp.where(j_iota == a_ref[pl.ds(t, 1), :], 1.0, 0.0).astype(BF16)
                    sel2 = jnp.where(j_iota == b_ref[pl.ds(t, 1), :], g_ref[pl.ds(t, 1), :], 0.0).astype(BF16)
                    tmp_ref[pl.ds(base + tt * TOKEN_PITCH, PEER_KEYS), :] = _dot_nt(sel1, sel2)
            for part in range(GROUPS_PER_TRIP):
                t0 = pl.multiple_of((gi * GROUPS_PER_TRIP + part) * TOKEN_GROUP, TOKEN_GROUP)
                base = part * TOKEN_GROUP * TOKEN_PITCH
                for r in range(PEER_KEYS):
                    rows = tmp_ref[pl.ds(base + r, TOKEN_GROUP, stride=TOKEN_PITCH), :]
                    gm_ref[r, pl.ds(t0, TOKEN_GROUP), :] = rows.astype(BF16)
            return carry

        lax.fori_loop(0, tm // (GROUPS_PER_TRIP * TOKEN_GROUP), token_groups, 0)


def _peer(h2, wq, sk, u, v, xmid, g2, fg, *, layer, tm, rows_per_seq, per_token, final_norm):
    n = h2.shape[0]
    ntiles = n // tm
    eb = PEER_EXPERTS // PEER_HEADS
    npick = PEER_HEADS * PEER_TOPK
    clamp = lambda t: jnp.clip(t, 0, ntiles - 1)
    routed = lambda g, j: clamp(g)
    stage_a = lambda g, j: clamp(g - 1)
    done = lambda g, j: clamp(g - 2 + jnp.minimum(j, 1))
    if per_token:
        ms = pl.BlockSpec((tm, D_MODEL), lambda g, j: (done(g, j), 0))
    else:
        ms = pl.BlockSpec((1, 1, D_MODEL), lambda g, j: (done(g, j) * tm // rows_per_seq, 0, 0))
    return pl.pallas_call(
        functools.partial(_peer_kernel, layer=layer, tm=tm, per_token=per_token, final_norm=final_norm),
        grid=(ntiles + 2, PEER_HEADS),
        in_specs=[pl.BlockSpec((tm, D_MODEL), lambda g, j: (stage_a(g, j), 0)),
                  pl.BlockSpec((tm, D_MODEL), lambda g, j: (routed(g, j), 0)),
                  pl.BlockSpec((D_MODEL, PEER_QDIM), lambda g, j: (0, j)),
                  pl.BlockSpec((2, PEER_KEYS, PEER_HALF), lambda g, j: (0, 0, 0)),
                  pl.BlockSpec(memory_space=pl.ANY),
                  pl.BlockSpec(memory_space=pl.ANY),
                  pl.BlockSpec((tm, D_MODEL), lambda g, j: (done(g, j), 0)),
                  ms,
                  pl.BlockSpec((1, D_MODEL), lambda g, j: (0, 0))],
        out_specs=pl.BlockSpec((tm, D_MODEL), lambda g, j: (done(g, j), 0)),
        out_shape=jax.ShapeDtypeStruct((n, D_MODEL), F32),
        scratch_shapes=[pltpu.VMEM((tm, D_MODEL), F32),
                        pltpu.VMEM((tm, eb), BF16),
                        pltpu.VMEM((PEER_KEYS, tm, PEER_KEYS), BF16),
                        pltpu.VMEM((CAND_ROWS, tm), F32), pltpu.VMEM((CAND_ROWS, tm), F32),
                        pltpu.VMEM((npick, tm), F32), pltpu.VMEM((npick, tm), F32),
                        pltpu.VMEM((tm, npick), F32), pltpu.VMEM((tm, npick), F32), pltpu.VMEM((tm, npick), F32),
                        pltpu.VMEM((GROUPS_PER_TRIP * TOKEN_GROUP * TOKEN_PITCH, PEER_KEYS), F32),
                        pltpu.VMEM((TABLE_BUFS, eb, D_MODEL), BF16), pltpu.VMEM((TABLE_BUFS, eb, D_MODEL), BF16),
                        pltpu.SemaphoreType.DMA((2, TABLE_BUFS))],
        compiler_params=_params(0, 2),
        name="peer",
    )(h2, h2, wq, sk, u, v, xmid, g2, fg)


TOKEN_TILE = 512
PEER_TILE = 256
SAMPLE_ROWS = 8
PROMPT_SEQS_PER_STEP = (8, 4, 4)
SAMPLE_SEQS_PER_STEP = (8, 8, 8)


def _lane_pad(vec, offset=0, width=LANES):
    return jnp.zeros((1, width), F32).at[0, offset:offset + vec.shape[0]].set(vec)


def _layer_weights(l, w):
    w_in = w['w_in'][l]
    o_m = RWKV_PROJ
    o_mg = o_m + MLSTM_MAIN
    o_s = o_mg + 2 * MLSTM_HEADS
    o_sg = o_s + SSD_MAIN
    gate_cols = jnp.zeros((D_MODEL, GATE_W), F32)
    gate_cols = gate_cols.at[:, 0:MLSTM_HEADS].set(w_in[:, o_mg:o_mg + MLSTM_HEADS])
    gate_cols = gate_cols.at[:, GATE_F:GATE_F + MLSTM_HEADS].set(w_in[:, o_mg + MLSTM_HEADS:o_s])
    gate_cols = gate_cols.at[:, GATE_DT:GATE_DT + SSD_HEADS].set(w_in[:, o_sg:o_sg + SSD_HEADS])
    w_cat = jnp.concatenate([w_in[:, :o_m], w_in[:, o_m:o_mg], w_in[:, o_s:o_sg], gate_cols], axis=1).astype(BF16)
    gate_bias = jnp.concatenate([_lane_pad(w['mlstm_i_b'][l]), _lane_pad(w['mlstm_f_b'][l]),
                                 _lane_pad(w['ssd_dt_bias'][l])], axis=1)
    row = lambda name: w[name][l].reshape(1, -1)
    return dict(
        w_cat=w_cat, gate_bias=gate_bias, w_out=w['w_out'][l].astype(BF16),
        norm1_g=row('norm1_g'), norm2_g=row('norm2_g'),
        rwkv=(row('rwkv_mu'), row('rwkv_w0'), w['rwkv_w_up'][l].astype(BF16), row('rwkv_a0'),
              w['rwkv_a_up'][l].astype(BF16), w['rwkv_g_up'][l].astype(BF16), row('rwkv_k_k'), row('rwkv_k_a'),
              row('rwkv_r_k'), row('rwkv_ln_g'), row('rwkv_ln_b')),
        mlstm=(w['mlstm_conv_w'][l], row('mlstm_conv_b'), gate_bias, row('mlstm_norm_g')),
        ssd=(w['ssd_conv_w'][l], row('ssd_conv_b'), gate_bias, _lane_pad(w['ssd_A_log'][l]),
             _lane_pad(w['ssd_D'][l]), row('ssd_norm_g')),
        wq=w['peer_wq'][l].astype(BF16), sk=w['peer_subkeys'][l].astype(BF16),
        u=w['peer_u_bf16'], v=w['peer_v_bf16'], layer=l,
    )


def _run_layer(x, mods, state, lw, head_sum, final_g, *, batch, seq, rows, nseq, per_token, final_norm):
    sh1, sc1, g1, sh2, sc2, g2 = mods
    shift, wkv, mconv, m_c, m_n, m_m, sconv, sst = state
    tm = min(TOKEN_TILE, batch * seq)
    tok = dict(tm=tm, rows_per_seq=seq, per_token=per_token)
    p_r, p_m, p_s, p_g = _inproj(x, sh1, sc1, lw['norm1_g'], lw['w_cat'], **tok)

    t_pad = -(-seq // rows) * rows
    t_valid = seq if seq < rows else rows

    def seqs(a):
        a = a.reshape(batch, seq, a.shape[-1])
        return a if t_pad == seq else jnp.pad(a, ((0, 0), (0, t_pad - seq), (0, 0)))

    def toks(a):
        return a[:, :seq].reshape(batch * seq, a.shape[-1])

    mix = dict(rows=rows, t_valid=t_valid)
    nseq_rwkv, nseq_mlstm, nseq_ssd = nseq
    o_r, shift, wkv = _rwkv(seqs(p_r), shift[:, None, :], wkv, *lw['rwkv'], head_sum, nseq=nseq_rwkv, **mix)
    m_m = jnp.pad(m_m, ((0, 0), (0, LANES - MLSTM_HEADS)))[:, None, :]
    o_m, mconv, m_c, m_n, m_m = _mlstm(seqs(p_m), seqs(p_g), mconv, m_c, m_n, m_m, *lw['mlstm'],
                                       nseq=nseq_mlstm, **mix)
    o_s, sconv, sst = _ssd(seqs(p_s), seqs(p_g), sconv, sst, *lw['ssd'], nseq=nseq_ssd, **mix)
    new_state = (shift[:, 0, :], wkv, mconv, m_c, m_n, m_m[:, 0, :MLSTM_HEADS], sconv, sst)

    x_mid, h2 = _outproj(toks(o_r), toks(o_m), toks(o_s), x, g1, sh2, sc2, lw['norm2_g'], lw['w_out'], **tok)
    x_new = _peer(h2, lw['wq'], lw['sk'], lw['u'], lw['v'], x_mid, g2, final_g, layer=lw['layer'],
                  tm=min(PEER_TILE, batch * seq), rows_per_seq=seq, per_token=per_token, final_norm=final_norm)
    return x_new, new_state


def kernel(x_prompt, x_sample, state_rwkv_shift, state_rwkv_wkv, state_mlstm_conv, state_mlstm_C, state_mlstm_n, state_mlstm_m, state_ssd_conv, state_ssd, c_prompt, c_sample, ada_w, ada_b, norm1_g, norm2_g, w_in, w_out, rwkv_mu, rwkv_w0, rwkv_w_up, rwkv_a0, rwkv_a_up, rwkv_g_up, rwkv_k_k, rwkv_k_a, rwkv_r_k, rwkv_ln_g, rwkv_ln_b, mlstm_conv_w, mlstm_conv_b, mlstm_i_b, mlstm_f_b, mlstm_norm_g, ssd_conv_w, ssd_conv_b, ssd_dt_bias, ssd_A_log, ssd_D, ssd_norm_g, peer_wq, peer_subkeys, peer_u, peer_v, final_g):
    weights = dict(norm1_g=norm1_g, norm2_g=norm2_g, w_in=w_in, w_out=w_out, rwkv_mu=rwkv_mu, rwkv_w0=rwkv_w0,
                   rwkv_w_up=rwkv_w_up, rwkv_a0=rwkv_a0, rwkv_a_up=rwkv_a_up, rwkv_g_up=rwkv_g_up,
                   rwkv_k_k=rwkv_k_k, rwkv_k_a=rwkv_k_a, rwkv_r_k=rwkv_r_k, rwkv_ln_g=rwkv_ln_g,
                   rwkv_ln_b=rwkv_ln_b, mlstm_conv_w=mlstm_conv_w, mlstm_conv_b=mlstm_conv_b,
                   mlstm_i_b=mlstm_i_b, mlstm_f_b=mlstm_f_b, mlstm_norm_g=mlstm_norm_g, ssd_conv_w=ssd_conv_w,
                   ssd_conv_b=ssd_conv_b, ssd_dt_bias=ssd_dt_bias, ssd_A_log=ssd_A_log, ssd_D=ssd_D,
                   ssd_norm_g=ssd_norm_g, peer_wq=peer_wq, peer_subkeys=peer_subkeys, peer_u=peer_u, peer_v=peer_v)
    weights['peer_u_bf16'] = peer_u.astype(BF16)
    weights['peer_v_bf16'] = peer_v.astype(BF16)
    nb, seq, d = x_prompt.shape
    nd, dseq, _ = x_sample.shape
    cache = (state_rwkv_shift, state_rwkv_wkv, state_mlstm_conv, state_mlstm_C, state_mlstm_n, state_mlstm_m,
             state_ssd_conv, state_ssd)

    mod = _ada_mod(jnp.concatenate([c_prompt, c_sample], axis=0), ada_w, ada_b)
    head_id = jnp.arange(RWKV_W) // HEAD_DIM
    head_sum = (head_id[:, None] == head_id[None, :]).astype(F32)
    fg = final_g.reshape(1, d)

    xp = x_prompt.reshape(nb * seq, d)
    xs = x_sample.reshape(nd * dseq, d)
    new_p, new_s = [], []
    for l in range(DEPTH):
        lw = _layer_weights(l, weights)
        last = l == DEPTH - 1
        mod_p = mod[l, :nb].reshape(nb, 6, 1, d)
        mods_p = tuple(mod_p[:, i] for i in range(6))
        mod_s = jnp.repeat(mod[l, nb:].reshape(nd, 6, d), dseq, axis=0)
        mods_s = tuple(mod_s[:, i] for i in range(6))
        zeros = tuple(jnp.zeros((nb,) + s.shape[2:], F32) for s in cache)
        xp, sp = _run_layer(xp, mods_p, zeros, lw, head_sum, fg, batch=nb, seq=seq, rows=CHUNK,
                            nseq=PROMPT_SEQS_PER_STEP, per_token=False, final_norm=last)
        xs, ss = _run_layer(xs, mods_s, tuple(s[l] for s in cache), lw, head_sum, fg, batch=nd, seq=dseq,
                            rows=SAMPLE_ROWS, nseq=SAMPLE_SEQS_PER_STEP, per_token=True, final_norm=last)
        new_p.append(sp)
        new_s.append(ss)

    def stk(states, i):
        return jnp.stack([st[i] for st in states])

    return ((xp.reshape(nb, seq, d), xs.reshape(nd, dseq, d))
            + tuple(stk(new_p, i) for i in range(8)) + tuple(stk(new_s, i) for i in range(8)))
```
